```python
import math
import jax, jax.numpy as jnp
from jax import lax
import numpy as np

D_MODEL = 1024
BATCH = 2
SEQ = 8192
DEPTH = 2

GRID_W = 64
HEAD_DIM = 64
ATTN_WIDTH = D_MODEL // 2
N_ATTN_HEADS = ATTN_WIDTH // HEAD_DIM
N_KV_HEADS = N_ATTN_HEADS // 4
KV_WIDTH = N_KV_HEADS * HEAD_DIM
SSM_WIDTH = D_MODEL // 2
SSM_HEAD_DIM = 64
SSM_HEADS = SSM_WIDTH // SSM_HEAD_DIM
SSM_GROUPS = 2
SSM_STATE = 128
SSM_CHUNK = 128
CONV_K = 5
XBC_WIDTH = SSM_WIDTH + 2 * SSM_GROUPS * SSM_STATE
D_MIX = ATTN_WIDTH + SSM_WIDTH
IN_SPLITS = (ATTN_WIDTH,
             ATTN_WIDTH + KV_WIDTH,
             ATTN_WIDTH + 2 * KV_WIDTH,
             ATTN_WIDTH + 2 * KV_WIDTH + SSM_WIDTH,
             ATTN_WIDTH + 2 * KV_WIDTH + SSM_WIDTH + XBC_WIDTH)
IN_WIDTH = IN_SPLITS[-1] + 2 * SSM_HEADS
Q_BLOCK = 128
ROPE_THETA = 10000.0
ROPE_AXIS_DIM = HEAD_DIM // 2
N_EXPERTS = 32
TOP_K = 4
D_EXPERT = D_MODEL
SWIGLU_LIMIT = 7.0
SWIGLU_ALPHA = 1.702
MOE_BLOCK = 128
EPS = 1e-6

kernel_name = "hymba_ssd_axial_gqa_moe_encoder"


def rms_norm(x, gain):
    xf = x.astype(jnp.float32)
    y = xf * lax.rsqrt(jnp.mean(xf * xf, axis=-1, keepdims=True) + EPS)
    return (y * gain.astype(jnp.float32)).astype(x.dtype)


def axial_rope_tables(seq_len):
    rows = seq_len // GRID_W
    row = jnp.repeat(jnp.arange(rows, dtype=jnp.float32), GRID_W)
    col = jnp.tile(jnp.arange(GRID_W, dtype=jnp.float32), rows)
    inv_freq = ROPE_THETA ** (-jnp.arange(0, ROPE_AXIS_DIM, 2, dtype=jnp.float32) / ROPE_AXIS_DIM)
    ang_r = row[:, None] * inv_freq
    ang_c = col[:, None] * inv_freq
    return jnp.cos(ang_r), jnp.sin(ang_r), jnp.cos(ang_c), jnp.sin(ang_c)


def rope_1d(x, cos, sin):
    x1, x2 = jnp.split(x, 2, axis=-1)
    c = cos[:, None, :].astype(x.dtype)
    s = sin[:, None, :].astype(x.dtype)
    return jnp.concatenate([x1 * c - x2 * s, x2 * c + x1 * s], axis=-1)


def axial_rope(x, tables):
    cr, sr, cc, sc = tables
    return jnp.concatenate([rope_1d(x[..., :ROPE_AXIS_DIM], cr, sr),
                            rope_1d(x[..., ROPE_AXIS_DIM:], cc, sc)], axis=-1)


def block_attention(q, k, v):
    b, l, h, d = q.shape
    g = h // N_KV_HEADS
    nb = l // Q_BLOCK
    scale = 1.0 / math.sqrt(d)
    qb = q.reshape(b, nb, Q_BLOCK, N_KV_HEADS, g, d).transpose(1, 0, 2, 3, 4, 5)

    def one_block(qi):
        s = jnp.einsum('bqkgd,bskd->bkgqs', qi, k).astype(jnp.float32) * scale
        p = jax.nn.softmax(s, axis=-1).astype(v.dtype)
        return jnp.einsum('bkgqs,bskd->bqkgd', p, v)

    o = lax.map(one_block, qb)
    return o.transpose(1, 0, 2, 3, 4, 5).reshape(b, l, h * d)


def ssd_chunked(x, dt, a, bm, cm):
    b, l, h, p = x.shape
    n = bm.shape[-1]
    c = l // SSM_CHUNK
    x = x.reshape(b, c, SSM_CHUNK, h, p)
    dt = dt.reshape(b, c, SSM_CHUNK, h)
    bm = bm.reshape(b, c, SSM_CHUNK, h, n)
    cm = cm.reshape(b, c, SSM_CHUNK, h, n)
    a_cum = jnp.cumsum(jnp.moveaxis(dt * a, 2, -1), axis=-1)
    xdt = x * dt[..., None]
    seg = a_cum[..., :, None] - a_cum[..., None, :]
    tri = jnp.tril(jnp.ones((SSM_CHUNK, SSM_CHUNK), dtype=bool))
    decay = jnp.exp(jnp.where(tri, seg, -jnp.inf))
    cb = jnp.einsum('bcqhn,bcshn->bchqs', cm, bm)
    y_diag = jnp.einsum('bchqs,bcshp->bcqhp', cb * decay, xdt)
    decay_to_end = jnp.exp(a_cum[..., -1:] - a_cum)
    states = jnp.einsum('bcshn,bchs,bcshp->bchpn', bm, decay_to_end, xdt)
    chunk_decay = jnp.exp(a_cum[..., -1])

    def step(carry, inp):
        st, dec = inp
        return carry * dec[..., None, None] + st, carry

    init = jnp.zeros((b, h, p, n), jnp.float32)
    _, prev = lax.scan(step, init, (jnp.moveaxis(states, 1, 0), jnp.moveaxis(chunk_decay, 1, 0)))
    prev = jnp.moveaxis(prev, 0, 1)
    y_off = jnp.einsum('bcqhn,bchpn,bchq->bcqhp', cm, prev, jnp.exp(a_cum))
    return (y_diag + y_off).reshape(b, l, h, p)


def ssm_branch(z, xbc, dt_raw, conv_w, conv_b, dt_bias, a_log, d_skip, norm_gain):
    b, l, _ = xbc.shape
    f32 = jnp.float32
    xbc = lax.conv_general_dilated(xbc, conv_w[:, None, :].astype(xbc.dtype), window_strides=(1,),
                                   padding=[(CONV_K // 2, CONV_K // 2)],
                                   dimension_numbers=('NWC', 'WIO', 'NWC'),
                                   feature_group_count=XBC_WIDTH) + conv_b
    xbc = jax.nn.silu(xbc)
    xs, bm, cm = jnp.split(xbc, [SSM_WIDTH, SSM_WIDTH + SSM_GROUPS * SSM_STATE], axis=-1)
    rep = SSM_HEADS // SSM_GROUPS
    xs = xs.reshape(b, l, SSM_HEADS, SSM_HEAD_DIM).astype(f32)
    bm = jnp.repeat(bm.reshape(b, l, SSM_GROUPS, SSM_STATE).astype(f32), rep, axis=2)
    cm = jnp.repeat(cm.reshape(b, l, SSM_GROUPS, SSM_STATE).astype(f32), rep, axis=2)
    dt = jax.nn.softplus(dt_raw.astype(f32).reshape(b, l, 2, SSM_HEADS) + dt_bias.astype(f32))
    a = -jnp.exp(a_log.astype(f32))
    y_fwd = ssd_chunked(xs, dt[:, :, 0], a[0], bm, cm)
    flip = lambda t: jnp.flip(t, axis=1)
    y_bwd = flip(ssd_chunked(flip(xs), flip(dt[:, :, 1]), a[1], flip(bm), flip(cm)))
    y = y_fwd + y_bwd + xs * d_skip.astype(f32)[:, None]
    y = y.reshape(b, l, SSM_WIDTH) * jax.nn.silu(z.astype(f32))
    y = rms_norm(y.reshape(b, l, SSM_GROUPS, SSM_WIDTH // SSM_GROUPS),
                 norm_gain.reshape(SSM_GROUPS, SSM_WIDTH // SSM_GROUPS))
    return y.reshape(b, l, SSM_WIDTH).astype(z.dtype)


def moe_ffn(h, w_router, b_router, w_gate, b_gate, w_up, b_up, w_down, b_down):
    b, l, d = h.shape
    t = h.reshape(-1, d)
    n_tok = t.shape[0]
    logits = (t @ w_router + b_router).astype(jnp.float32)
    top_val, top_idx = lax.top_k(logits, TOP_K)
    top_w = jax.nn.softmax(top_val, axis=-1)
    n_assign = n_tok * TOP_K
    flat_e = top_idx.reshape(-1)
    flat_tok = jnp.repeat(jnp.arange(n_tok, dtype=jnp.int32), TOP_K)
    flat_w = top_w.reshape(-1)
    counts = jnp.bincount(flat_e, length=N_EXPERTS)
    padded = (counts + MOE_BLOCK - 1) // MOE_BLOCK * MOE_BLOCK
    pad_end = jnp.cumsum(padded)
    pad_start = pad_end - padded
    start = jnp.cumsum(counts) - counts
    order = jnp.argsort(flat_e)
    sorted_e = flat_e[order]
    dest = pad_start[sorted_e] + jnp.arange(n_assign) - start[sorted_e]
    n_rows = n_assign + N_EXPERTS * MOE_BLOCK
    row_tok = jnp.zeros((n_rows,), jnp.int32).at[dest].set(flat_tok[order])
    row_w = jnp.zeros((n_rows,), jnp.float32).at[dest].set(flat_w[order])
    n_blocks = n_rows // MOE_BLOCK
    block_e = jnp.minimum(jnp.searchsorted(pad_end, jnp.arange(n_blocks) * MOE_BLOCK, side='right'),
                          N_EXPERTS - 1)
    xs = t[row_tok].reshape(n_blocks, MOE_BLOCK, d)

    def expert_block(args):
        xb, e = args
        g = xb @ w_gate[e] + b_gate[e]
        u = xb @ w_up[e] + b_up[e]
        g = jnp.minimum(g, SWIGLU_LIMIT)
        u = jnp.clip(u, -SWIGLU_LIMIT, SWIGLU_LIMIT)
        act = g * jax.nn.sigmoid(SWIGLU_ALPHA * g) * (u + 1.0)
        return act @ w_down[e] + b_down[e]

    ys = lax.map(expert_block, (xs, block_e)).reshape(n_rows, d)
    out = jnp.zeros_like(t).at[row_tok].add(ys * row_w[:, None].astype(t.dtype))
    return out.reshape(b, l, d)


def setup_inputs(seed: int = 0) -> dict:
    key = jax.random.key(seed)
    ks = jax.random.split(key, 24)
    f32 = jnp.float32
    nrm = lambda k, shape, s: jax.random.normal(k, shape, f32) * s
    gain = lambda k, shape: 1.0 + 0.02 * jax.random.normal(k, shape, f32)
    dt0 = jnp.exp(jax.random.uniform(ks[7], (DEPTH, 2, SSM_HEADS), f32) * (math.log(0.1) - math.log(0.001)) + math.log(0.001))
    return {
        "x": nrm(ks[0], (BATCH, SEQ, D_MODEL), 1.0),
        "norm_mix": gain(ks[1], (DEPTH, D_MODEL)),
        "w_in": nrm(ks[2], (DEPTH, D_MODEL, IN_WIDTH), D_MODEL ** -0.5),
        "q_norm": gain(ks[3], (DEPTH, HEAD_DIM)),
        "k_norm": gain(ks[4], (DEPTH, HEAD_DIM)),
        "conv_w": nrm(ks[5], (DEPTH, CONV_K, XBC_WIDTH), CONV_K ** -0.5),
        "conv_b": nrm(ks[6], (DEPTH, XBC_WIDTH), 0.02),
        "dt_bias": dt0 + jnp.log(-jnp.expm1(-dt0)),
        "a_log": jnp.log(jax.random.uniform(ks[8], (DEPTH, 2, SSM_HEADS), f32, 1.0, 16.0)),
        "d_skip": gain(ks[9], (DEPTH, SSM_HEADS)),
        "ssm_norm": gain(ks[10], (DEPTH, SSM_WIDTH)),
        "attn_norm": gain(ks[11], (DEPTH, ATTN_WIDTH)),
        "w_out": nrm(ks[12], (DEPTH, D_MIX, D_MODEL), D_MIX ** -0.5),
        "norm_ffn": gain(ks[13], (DEPTH, D_MODEL)),
        "w_router": nrm(ks[14], (DEPTH, D_MODEL, N_EXPERTS), D_MODEL ** -0.5),
        "b_router": nrm(ks[15], (DEPTH, N_EXPERTS), 0.01),
        "w_gate": nrm(ks[16], (DEPTH, N_EXPERTS, D_MODEL, D_EXPERT), D_MODEL ** -0.5),
        "b_gate": nrm(ks[17], (DEPTH, N_EXPERTS, D_EXPERT), 0.01),
        "w_up": nrm(ks[18], (DEPTH, N_EXPERTS, D_MODEL, D_EXPERT), D_MODEL ** -0.5),
        "b_up": nrm(ks[19], (DEPTH, N_EXPERTS, D_EXPERT), 0.01),
        "w_down": nrm(ks[20], (DEPTH, N_EXPERTS, D_EXPERT, D_MODEL), D_EXPERT ** -0.5),
        "b_down": nrm(ks[21], (DEPTH, N_EXPERTS, D_MODEL), 0.01),
    }


def reference(x, norm_mix, w_in, q_norm, k_norm, conv_w, conv_b, dt_bias, a_log, d_skip,
              ssm_norm, attn_norm, w_out, norm_ffn, w_router, b_router, w_gate, b_gate,
              w_up, b_up, w_down, b_down):
    b, l, _ = x.shape
    rope = axial_rope_tables(l)
    for i in range(DEPTH):
        hn = rms_norm(x, norm_mix[i])
        proj = hn @ w_in[i]
        q, k, v, z, xbc, dt_raw = jnp.split(proj, IN_SPLITS, axis=-1)
        q = axial_rope(rms_norm(q.reshape(b, l, N_ATTN_HEADS, HEAD_DIM), q_norm[i]), rope)
        k = axial_rope(rms_norm(k.reshape(b, l, N_KV_HEADS, HEAD_DIM), k_norm[i]), rope)
        v = v.reshape(b, l, N_KV_HEADS, HEAD_DIM)
        attn = rms_norm(block_attention(q, k, v), attn_norm[i])
        ssm = ssm_branch(z, xbc, dt_raw, conv_w[i], conv_b[i], dt_bias[i], a_log[i],
                         d_skip[i], ssm_norm[i])
        x = x + jnp.concatenate([attn, ssm], axis=-1) @ w_out[i]
        x = x + moe_ffn(rms_norm(x, norm_ffn[i]), w_router[i], b_router[i], w_gate[i], b_gate[i],
                        w_up[i], b_up[i], w_down[i], b_down[i])
    return x
```

```python
import functools
import math

import jax
import jax.numpy as jnp
from jax import lax
from jax.experimental import pallas as pl
from jax.experimental.pallas import tpu as pltpu

F32 = jnp.float32
BF16 = jnp.bfloat16

D_MODEL = 1024
GRID_W = 64
HEAD_DIM = 64
N_HEADS = 8
N_KV = 2
HEADS_PER_KV = N_HEADS // N_KV
ATTN_WIDTH = N_HEADS * HEAD_DIM
KV_WIDTH = N_KV * HEAD_DIM
SSM_WIDTH = 512
SSM_HEADS = 8
SSM_HEAD_DIM = 64
SSM_GROUPS = 2
SSM_HEADS_PER_GROUP = SSM_HEADS // SSM_GROUPS
SSM_STATE = 128
CHUNK = 128
CONV_K = 5
XBC_WIDTH = SSM_WIDTH + 2 * SSM_GROUPS * SSM_STATE
ROPE_THETA = 10000.0
ROPE_AXIS_DIM = HEAD_DIM // 2
N_EXPERTS = 32
TOP_K = 4
SWIGLU_LIMIT = 7.0
SWIGLU_ALPHA = 1.702
EPS = 1e-6

LANES = 128
SUBLANES = 8
VMEM_LIMIT = 56 * 1024 * 1024

Q_PAD = N_HEADS * LANES
KV_PAD = N_KV * LANES


def _cparams(sem):
    return pltpu.CompilerParams(dimension_semantics=sem, vmem_limit_bytes=VMEM_LIMIT)


def _dot(a, b):
    return jnp.dot(a, b, preferred_element_type=F32)


def _dot_nt(a, b):
    return lax.dot_general(a, b, (((1,), (1,)), ((), ())), preferred_element_type=F32)


def _split3(x):
    hi = x.astype(BF16)
    r1 = x - hi.astype(F32)
    mid = r1.astype(BF16)
    lo = (r1 - mid.astype(F32)).astype(BF16)
    return hi, mid, lo


def _sigmoid(x):
    return 1.0 / (1.0 + jnp.exp(-x))


def _softplus(x):
    return jnp.maximum(x, 0.0) + jnp.log(1.0 + jnp.exp(-jnp.abs(x)))


def _inproj_kernel(x_ref, g_ref, wq_ref, wkv_ref, wz_ref, wxbc_ref, wdt_ref, wdtT_ref,
                   qg_ref, kg_ref, cos_ref, s1_ref, s2_ref,
                   q_ref, k_ref, v_ref, z_ref, xbc_ref, dt_ref, dtT_ref):
    x = x_ref[0]
    hn = (x * lax.rsqrt(jnp.mean(x * x, axis=-1, keepdims=True) + EPS) * g_ref[...]).astype(BF16)
    cos = cos_ref[...]
    s1 = s1_ref[...]
    s2 = s2_ref[...]

    def norm_rope(a, gain):
        r = lax.rsqrt(jnp.sum(a * a, axis=-1, keepdims=True) * (1.0 / HEAD_DIM) + EPS)
        y = a * r * gain
        return y * cos + pltpu.roll(y, LANES - 16, 1) * s1 + pltpu.roll(y, 16, 1) * s2

    accq = _dot(hn, wq_ref[...])
    scale = 1.0 / math.sqrt(HEAD_DIM)
    for h in range(N_HEADS):
        q_ref[0, h] = (norm_rope(accq[:, LANES * h:LANES * (h + 1)], qg_ref[...]) * scale).astype(BF16)
    acckv = _dot(hn, wkv_ref[...])
    for g in range(N_KV):
        k_ref[0, g] = norm_rope(acckv[:, LANES * g:LANES * (g + 1)], kg_ref[...]).astype(BF16)
        v_ref[0, g] = acckv[:, KV_PAD + LANES * g:KV_PAD + LANES * (g + 1)].astype(BF16)
    z_ref[0] = _dot(hn, wz_ref[...])
    xbc_ref[0] = _dot(hn, wxbc_ref[...])
    dt_ref[0] = _dot(hn, wdt_ref[...])
    dtT_ref[0] = _dot_nt(wdtT_ref[...], hn)


def _inproj(x, gain, wq, wkv, wz, wxbc, wdt, wdtT, qg, kg, cos, s1, s2, tm):
    b, l, d = x.shape
    nt = l // tm
    const = lambda shape: pl.BlockSpec(shape, lambda bi, i: (0,) * len(shape))
    tab = pl.BlockSpec((tm, LANES), lambda bi, i: (i, 0))
    return pl.pallas_call(
        _inproj_kernel,
        grid=(b, nt),
        in_specs=[
            pl.BlockSpec((1, tm, d), lambda bi, i: (bi, i, 0)),
            const((1, d)), const(wq.shape), const(wkv.shape), const(wz.shape), const(wxbc.shape),
            const(wdt.shape), const(wdtT.shape), const((1, LANES)), const((1, LANES)),
            tab, tab, tab,
        ],
        out_specs=[
            pl.BlockSpec((1, N_HEADS, tm, LANES), lambda bi, i: (bi, 0, i, 0)),
            pl.BlockSpec((1, N_KV, tm, LANES), lambda bi, i: (bi, 0, i, 0)),
            pl.BlockSpec((1, N_KV, tm, LANES), lambda bi, i: (bi, 0, i, 0)),
            pl.BlockSpec((1, tm, SSM_WIDTH), lambda bi, i: (bi, i, 0)),
            pl.BlockSpec((1, tm, XBC_WIDTH), lambda bi, i: (bi, i, 0)),
            pl.BlockSpec((1, tm, 2 * SSM_HEADS), lambda bi, i: (bi, i, 0)),
            pl.BlockSpec((1, 2 * SSM_HEADS, tm), lambda bi, i: (bi, 0, i)),
        ],
        out_shape=[
            jax.ShapeDtypeStruct((b, N_HEADS, l, LANES), BF16),
            jax.ShapeDtypeStruct((b, N_KV, l, LANES), BF16),
            jax.ShapeDtypeStruct((b, N_KV, l, LANES), BF16),
            jax.ShapeDtypeStruct((b, l, SSM_WIDTH), F32),
            jax.ShapeDtypeStruct((b, l, XBC_WIDTH), F32),
            jax.ShapeDtypeStruct((b, l, 2 * SSM_HEADS), F32),
            jax.ShapeDtypeStruct((b, 2 * SSM_HEADS, l), F32),
        ],
        compiler_params=_cparams(("parallel", "parallel")),
        name="inproj",
    )(x, gain, wq, wkv, wz, wxbc, wdt, wdtT, qg, kg, cos, s1, s2)


def _attn_kernel(q_ref, k_ref, v_ref, o_ref, m_ref, l_ref, acc_ref, *, tk):
    tq = q_ref.shape[2]
    seq = k_ref.shape[2]
    rows = HEADS_PER_KV * tq
    q = q_ref[0].reshape(rows, LANES)
    m_ref[...] = jnp.full((rows, 1), -jnp.inf, F32)
    l_ref[...] = jnp.zeros((rows, 1), F32)
    acc_ref[...] = jnp.zeros((rows, LANES), F32)

    def body(j, carry):
        off = pl.multiple_of(j * tk, tk)
        kc = k_ref[0, 0, pl.ds(off, tk), :]
        vc = v_ref[0, 0, pl.ds(off, tk), :]
        s = _dot_nt(q, kc)
        m_old = m_ref[...]
        m_new = jnp.maximum(m_old, jnp.max(s, axis=-1, keepdims=True))
        alpha = jnp.exp(m_old - m_new)
        p = jnp.exp(s - m_new)
        l_ref[...] = alpha * l_ref[...] + jnp.sum(p, axis=-1, keepdims=True)
        acc_ref[...] = alpha * acc_ref[...] + _dot(p.astype(BF16), vc)
        m_ref[...] = m_new
        return carry

    lax.fori_loop(0, seq // tk, body, 0)
    o_ref[0] = (acc_ref[...] / l_ref[...]).reshape(HEADS_PER_KV, tq, LANES)


def _attention(q, k, v, tq, tk):
    b, _, l, _ = q.shape
    rows = HEADS_PER_KV * tq
    return pl.pallas_call(
        functools.partial(_attn_kernel, tk=tk),
        grid=(b, N_KV, l // tq),
        in_specs=[
            pl.BlockSpec((1, HEADS_PER_KV, tq, LANES), lambda bi, g, i: (bi, g, i, 0)),
            pl.BlockSpec((1, 1, l, LANES), lambda bi, g, i: (bi, g, 0, 0)),
            pl.BlockSpec((1, 1, l, LANES), lambda bi, g, i: (bi, g, 0, 0)),
        ],
        out_specs=pl.BlockSpec((1, HEADS_PER_KV, tq, LANES), lambda bi, g, i: (bi, g, i, 0)),
        out_shape=jax.ShapeDtypeStruct((b, N_HEADS, l, LANES), F32),
        scratch_shapes=[
            pltpu.VMEM((rows, 1), F32),
            pltpu.VMEM((rows, 1), F32),
            pltpu.VMEM((rows, LANES), F32),
        ],
        compiler_params=_cparams(("parallel", "parallel", "parallel")),
        name="attn",
    )(q, k, v)


def _ssd_kernel(*refs, reverse):
    if reverse:
        (cur_ref, prev_ref, next_ref, dt_ref, dtT_ref, convw_ref, convb_ref, dtb_row_ref, dtb_col_ref,
         alog_row_ref, alog_col_ref, yin_ref, z_ref, dskip_ref, gain_ref,
         y_ref, pad_ref, state_ref) = refs
    else:
        (cur_ref, prev_ref, next_ref, dt_ref, dtT_ref, convw_ref, convb_ref, dtb_row_ref, dtb_col_ref,
         alog_row_ref, alog_col_ref, y_ref, pad_ref, state_ref) = refs
    c = pl.program_id(1)
    nc = pl.num_programs(1)
    chunk = (nc - 1 - c) if reverse else c

    @pl.when(c == 0)
    def _():
        state_ref[...] = jnp.zeros(state_ref.shape, F32)

    halo = SUBLANES
    pad_ref[0:halo, :] = jnp.where(chunk > 0, prev_ref[0], 0.0)
    pad_ref[halo:halo + CHUNK, :] = cur_ref[0]
    pad_ref[halo + CHUNK:, :] = jnp.where(chunk < nc - 1, next_ref[0], 0.0)
    conv = jnp.broadcast_to(convb_ref[...], (CHUNK, XBC_WIDTH))
    for j in range(CONV_K):
        conv = conv + convw_ref[j:j + 1, :] * pad_ref[pl.ds(halo - CONV_K // 2 + j, CHUNK), :]
    xc = conv * _sigmoid(conv)
    xs = xc[:, :SSM_WIDTH]

    dt_col = _softplus(dt_ref[0, 0] + dtb_row_ref[0])
    dt_row = _softplus(dtT_ref[0, 0] + dtb_col_ref[0])
    da_col = dt_col * -jnp.exp(alog_row_ref[0])
    da_row = dt_row * -jnp.exp(alog_col_ref[0])

    ri = lax.broadcasted_iota(jnp.int32, (CHUNK, CHUNK), 0)
    ci = lax.broadcasted_iota(jnp.int32, (CHUNK, CHUNK), 1)
    low = ci <= ri
    up = ci >= ri
    low_m = jnp.where(low, 1.0, 0.0).astype(BF16)
    up_m = jnp.where(up, 1.0, 0.0).astype(BF16)
    col_m, row_m, mask = (up_m, low_m, up) if reverse else (low_m, up_m, low)
    ch, cm, cl = _split3(da_col)
    acum_col = _dot(col_m, ch) + _dot(col_m, cm) + _dot(col_m, cl)
    rh, rm, rl = _split3(da_row)
    acum_row = _dot(rh, row_m) + _dot(rm, row_m) + _dot(rl, row_m)
    end = 0 if reverse else CHUNK - 1
    a_end_row = acum_col[end:end + 1, :]
    w_col = dt_col * jnp.exp(a_end_row - acum_col)
    e_col = jnp.exp(acum_col)
    chunk_decay = jnp.exp(a_end_row)

    pieces = []
    for g in range(SSM_GROUPS):
        bm = xc[:, SSM_WIDTH + g * SSM_STATE:SSM_WIDTH + (g + 1) * SSM_STATE]
        cmat = xc[:, SSM_WIDTH + (SSM_GROUPS + g) * SSM_STATE:SSM_WIDTH + (SSM_GROUPS + g + 1) * SSM_STATE]
        bm16 = bm.astype(BF16)
        cm16 = cmat.astype(BF16)
        cb = _dot_nt(cm16, bm16)
        bt16 = bm.T.astype(BF16)
        for hh in range(SSM_HEADS_PER_GROUP):
            h = g * SSM_HEADS_PER_GROUP + hh
            xh = xs[:, h * SSM_HEAD_DIM:(h + 1) * SSM_HEAD_DIM]
            seg = acum_col[:, h:h + 1] - acum_row[h:h + 1, :]
            decay = jnp.exp(jnp.where(mask, seg, -jnp.inf))
            mat = cb * decay * dt_row[h:h + 1, :]
            y_diag = _dot(mat.astype(BF16), xh.astype(BF16))
            prev_state = state_ref[h]
            y_off = _dot(cm16, prev_state.astype(BF16)) * e_col[:, h:h + 1]
            pieces.append(y_diag + y_off)
            xw = xh * w_col[:, h:h + 1]
            state_ref[h] = prev_state * chunk_decay[:, h:h + 1] + _dot(bt16, xw.astype(BF16))
    y = jnp.concatenate(pieces, axis=1)

    if not reverse:
        y_ref[0] = y
    else:
        y = y + yin_ref[0] + xs * dskip_ref[...]
        zz = z_ref[0]
        y = y * (zz * _sigmoid(zz))
        gw = SSM_WIDTH // SSM_GROUPS
        outs = []
        for g in range(SSM_GROUPS):
            yg = y[:, g * gw:(g + 1) * gw]
            outs.append(yg * lax.rsqrt(jnp.mean(yg * yg, axis=-1, keepdims=True) + EPS))
        y_ref[0] = jnp.concatenate(outs, axis=1) * gain_ref[...]


def _ssd(xbc, dt, dtT, convw, convb, dtb_row, dtb_col, alog_row, alog_col, direction,
         y_fwd=None, z=None, dskip=None, gain=None):
    b, l, _ = xbc.shape
    nc = l // CHUNK
    reverse = direction == 1
    blocks_per_chunk = CHUNK // SUBLANES
    nblk = l // SUBLANES

    def pos(c):
        return (nc - 1 - c) if reverse else c

    const = lambda shape: pl.BlockSpec(shape, lambda bi, c: (0,) * len(shape))
    dsel = lambda shape: pl.BlockSpec(shape, lambda bi, c: (direction,) + (0,) * (len(shape) - 1))
    in_specs = [
        pl.BlockSpec((1, CHUNK, XBC_WIDTH), lambda bi, c: (bi, pos(c), 0)),
        pl.BlockSpec((1, SUBLANES, XBC_WIDTH),
                     lambda bi, c: (bi, jnp.maximum(pos(c) * blocks_per_chunk - 1, 0), 0)),
        pl.BlockSpec((1, SUBLANES, XBC_WIDTH),
                     lambda bi, c: (bi, jnp.minimum((pos(c) + 1) * blocks_per_chunk, nblk - 1), 0)),
        pl.BlockSpec((1, 1, CHUNK, SSM_HEADS), lambda bi, c: (direction, bi, pos(c), 0)),
        pl.BlockSpec((1, 1, SSM_HEADS, CHUNK), lambda bi, c: (direction, bi, 0, pos(c))),
        const((CONV_K, XBC_WIDTH)), const((1, XBC_WIDTH)),
        dsel((1, 1, SSM_HEADS)), dsel((1, SSM_HEADS, 1)), dsel((1, 1, SSM_HEADS)), dsel((1, SSM_HEADS, 1)),
    ]
    args = [xbc, xbc, xbc, dt, dtT, convw, convb, dtb_row, dtb_col, alog_row, alog_col]
    if reverse:
        in_specs += [
            pl.BlockSpec((1, CHUNK, SSM_WIDTH), lambda bi, c: (bi, pos(c), 0)),
            pl.BlockSpec((1, CHUNK, SSM_WIDTH), lambda bi, c: (bi, pos(c), 0)),
            const((1, SSM_WIDTH)), const((1, SSM_WIDTH)),
        ]
        args += [y_fwd, z, dskip, gain]
    return pl.pallas_call(
        functools.partial(_ssd_kernel, reverse=reverse),
        grid=(b, nc),
        in_specs=in_specs,
        out_specs=pl.BlockSpec((1, CHUNK, SSM_WIDTH), lambda bi, c: (bi, pos(c), 0)),
        out_shape=jax.ShapeDtypeStruct((b, l, SSM_WIDTH), F32),
        scratch_shapes=[
            pltpu.VMEM((CHUNK + 2 * SUBLANES, XBC_WIDTH), F32),
            pltpu.VMEM((SSM_HEADS, SSM_STATE, SSM_HEAD_DIM), F32),
        ],
        compiler_params=_cparams(("parallel", "arbitrary")),
        name="ssd_bwd" if reverse else "ssd_fwd",
    )(*args)


def _outproj_kernel(attn_ref, ssm_ref, x_ref, again_ref, wa_ref, ws_ref, fgain_ref, wr_hi_ref, wr_lo_ref,
                    x1_ref, hn_ref, logit_ref):
    a = jnp.concatenate([attn_ref[0, h] for h in range(N_HEADS)], axis=1)
    r = lax.rsqrt(jnp.sum(a * a, axis=-1, keepdims=True) * (1.0 / ATTN_WIDTH) + EPS)
    an = (a * r * again_ref[...]).astype(BF16)
    x1 = x_ref[...] + _dot(an, wa_ref[...]) + _dot(ssm_ref[...].astype(BF16), ws_ref[...])
    x1_ref[...] = x1
    hn = x1 * lax.rsqrt(jnp.mean(x1 * x1, axis=-1, keepdims=True) + EPS) * fgain_ref[...]
    hn_ref[...] = hn
    h_hi = hn.astype(BF16)
    h_lo = (hn - h_hi.astype(F32)).astype(BF16)
    w_hi = wr_hi_ref[...]
    logit_ref[...] = _dot_nt(w_hi, h_hi) + _dot_nt(w_hi, h_lo) + _dot_nt(wr_lo_ref[...], h_hi)


def _outproj(attn, ssm, x, again, wa, ws, fgain, wr_hi, wr_lo, tm):
    b, _, l, _ = attn.shape
    t = b * l
    nt = l // tm
    const = lambda shape: pl.BlockSpec(shape, lambda i: (0,) * len(shape))
    return pl.pallas_call(
        _outproj_kernel,
        grid=(t // tm,),
        in_specs=[
            pl.BlockSpec((1, N_HEADS, tm, LANES), lambda i: (i // nt, 0, i % nt, 0)),
            pl.BlockSpec((tm, SSM_WIDTH), lambda i: (i, 0)),
            pl.BlockSpec((tm, D_MODEL), lambda i: (i, 0)),
            const((1, Q_PAD)), const(wa.shape), const(ws.shape), const((1, D_MODEL)),
            const(wr_hi.shape), const(wr_lo.shape),
        ],
        out_specs=[
            pl.BlockSpec((tm, D_MODEL), lambda i: (i, 0)),
            pl.BlockSpec((tm, D_MODEL), lambda i: (i, 0)),
            pl.BlockSpec((N_EXPERTS, tm), lambda i: (0, i)),
        ],
        out_shape=[
            jax.ShapeDtypeStruct((t, D_MODEL), F32),
            jax.ShapeDtypeStruct((t, D_MODEL), F32),
            jax.ShapeDtypeStruct((N_EXPERTS, t), F32),
        ],
        compiler_params=_cparams(("parallel",)),
        name="outproj",
    )(attn, ssm, x, again, wa, ws, fgain, wr_hi, wr_lo)


def _route_kernel(logit_ref, bias_ref, idx_ref, w_ref, rank_ref, count_ref, carry_ref):
    i = pl.program_id(0)
    tn = logit_ref.shape[1]

    @pl.when(i == 0)
    def _():
        carry_ref[...] = jnp.zeros(carry_ref.shape, F32)

    lg = logit_ref[...] + bias_ref[...]
    eid = lax.broadcasted_iota(jnp.int32, (N_EXPERTS, tn), 0).astype(F32)
    chosen = jnp.zeros((N_EXPERTS, tn), F32)
    vals, ids = [], []
    for _ in range(TOP_K):
        mx = jnp.max(lg, axis=0, keepdims=True)
        sel = jnp.min(jnp.where(lg == mx, eid, float(N_EXPERTS)), axis=0, keepdims=True)
        hit = eid == sel
        chosen = jnp.where(hit, 1.0, chosen)
        lg = jnp.where(hit, -jnp.inf, lg)
        vals.append(mx)
        ids.append(sel)
    ex = [jnp.exp(v - vals[0]) for v in vals]
    den = ex[0] + ex[1] + ex[2] + ex[3]
    ri = lax.broadcasted_iota(jnp.int32, (tn, tn), 0)
    ci = lax.broadcasted_iota(jnp.int32, (tn, tn), 1)
    upper = jnp.where(ri <= ci, 1.0, 0.0).astype(BF16)
    incl = _dot(chosen.astype(BF16), upper)
    rank_e = incl - chosen + carry_ref[...]
    carry_ref[...] = carry_ref[...] + incl[:, tn - 1:tn]
    for k in range(TOP_K):
        idx_ref[k:k + 1, :] = ids[k].astype(jnp.int32)
        w_ref[k:k + 1, :] = ex[k] / den
        rank_ref[k:k + 1, :] = jnp.sum(jnp.where(eid == ids[k], rank_e, 0.0), axis=0,
                                       keepdims=True).astype(jnp.int32)
    count_ref[...] = carry_ref[...].astype(jnp.int32)


def _route(logits_t, bias_col, tn):
    t = logits_t.shape[1]
    tok = pl.BlockSpec((TOP_K, tn), lambda i: (0, i))
    return pl.pallas_call(
        _route_kernel,
        grid=(t // tn,),
        in_specs=[pl.BlockSpec((N_EXPERTS, tn), lambda i: (0, i)),
                  pl.BlockSpec((N_EXPERTS, 1), lambda i: (0, 0))],
        out_specs=[tok, tok, tok, pl.BlockSpec((N_EXPERTS, 1), lambda i: (0, 0))],
        out_shape=[
            jax.ShapeDtypeStruct((TOP_K, t), jnp.int32),
            jax.ShapeDtypeStruct((TOP_K, t), F32),
            jax.ShapeDtypeStruct((TOP_K, t), jnp.int32),
            jax.ShapeDtypeStruct((N_EXPERTS, 1), jnp.int32),
        ],
        scratch_shapes=[pltpu.VMEM((N_EXPERTS, 1), F32)],
        compiler_params=_cparams(("arbitrary",)),
        name="route",
    )(logits_t, bias_col)


def _dispatch_kernel(dest_ref, hn_ref, xs_in_ref, xs_ref, sem):
    del xs_in_ref
    tm = hn_ref.shape[0]

    def row_copy(k, t):
        return pltpu.make_async_copy(hn_ref.at[pl.ds(t, 1), :],
                                     xs_ref.at[pl.ds(dest_ref[k, t], 1), :], sem)

    def issue(t, carry):
        for k in range(TOP_K):
            row_copy(k, t).start()
        return carry

    lax.fori_loop(0, tm, issue, 0)
    for _ in range(TOP_K):
        pltpu.make_async_copy(hn_ref, xs_ref.at[pl.ds(0, tm), :], sem).wait()


def _dispatch(dest, hn, xs_init, tm):
    t, d = hn.shape
    return pl.pallas_call(
        _dispatch_kernel,
        grid=(t // tm,),
        in_specs=[
            pl.BlockSpec((TOP_K, tm), lambda i: (0, i), memory_space=pltpu.SMEM),
            pl.BlockSpec((tm, d), lambda i: (i, 0)),
            pl.BlockSpec(memory_space=pl.ANY),
        ],
        out_specs=pl.BlockSpec(memory_space=pl.ANY),
        out_shape=jax.ShapeDtypeStruct(xs_init.shape, F32),
        scratch_shapes=[pltpu.SemaphoreType.DMA(())],
        input_output_aliases={2: 0},
        compiler_params=_cparams(("arbitrary",)),
        name="dispatch",
    )(dest, hn, xs_init)


def _expert_kernel(block_e_ref, nused_ref, xs_ref, wg_ref, bg_ref, wu_ref, bu_ref, wd_ref, bd_ref,
                   ys_ref, wg16, wu16, wd16):
    b = pl.program_id(0)
    used = b < nused_ref[0]
    prev_e = block_e_ref[jnp.maximum(b - 1, 0)]
    fresh = jnp.logical_or(b == 0, block_e_ref[b] != prev_e)

    @pl.when(jnp.logical_and(used, fresh))
    def _():
        wg16[...] = wg_ref[0].astype(BF16)
        wu16[...] = wu_ref[0].astype(BF16)
        wd16[...] = wd_ref[0].astype(BF16)

    @pl.when(used)
    def _():
        x = xs_ref[...].astype(BF16)
        g = _dot(x, wg16[...]) + bg_ref[0]
        u = _dot(x, wu16[...]) + bu_ref[0]
        g = jnp.minimum(g, SWIGLU_LIMIT)
        u = jnp.clip(u, -SWIGLU_LIMIT, SWIGLU_LIMIT)
        act = g * _sigmoid(SWIGLU_ALPHA * g) * (u + 1.0)
        ys_ref[...] = _dot(act.astype(BF16), wd16[...]) + bd_ref[0]

    @pl.when(jnp.logical_not(used))
    def _():
        ys_ref[...] = jnp.zeros(ys_ref.shape, F32)


def _experts(block_e, nused, xs, wg, bg, wu, bu, wd, bd, bm):
    n_rows, d = xs.shape
    nb = n_rows // bm
    wspec = pl.BlockSpec((1, d, d), lambda i, be, nu: (be[i], 0, 0))
    bspec = pl.BlockSpec((1, 1, d), lambda i, be, nu: (be[i], 0, 0))
    row = pl.BlockSpec((bm, d), lambda i, be, nu: (i, 0))
    return pl.pallas_call(
        _expert_kernel,
        grid_spec=pltpu.PrefetchScalarGridSpec(
            num_scalar_prefetch=2,
            grid=(nb,),
            in_specs=[row, wspec, bspec, wspec, bspec, wspec, bspec],
            out_specs=row,
            scratch_shapes=[pltpu.VMEM((d, d), BF16)] * 3,
        ),
        out_shape=jax.ShapeDtypeStruct((n_rows, d), F32),
        compiler_params=_cparams(("arbitrary",)),
        name="experts",
    )(block_e, nused, xs, wg, bg, wu, bu, wd, bd)


def _combine_kernel(dest_ref, w_ref, x1_ref, ys_ref, out_ref, buf, sem):
    tm = x1_ref.shape[0]

    def issue(t, carry):
        for k in range(TOP_K):
            pltpu.make_async_copy(ys_ref.at[pl.ds(dest_ref[k, t], 1), :],
                                  buf.at[k, pl.ds(t, 1), :], sem).start()
        return carry

    lax.fori_loop(0, tm, issue, 0)
    for k in range(TOP_K):
        pltpu.make_async_copy(ys_ref.at[pl.ds(0, tm), :], buf.at[k], sem).wait()
    w = w_ref[...]
    acc = x1_ref[...]
    for k in range(TOP_K):
        acc = acc + w[:, k:k + 1] * buf[k]
    out_ref[...] = acc


def _combine(dest, w_tok, x1, ys, tm):
    t, d = x1.shape
    return pl.pallas_call(
        _combine_kernel,
        grid=(t // tm,),
        in_specs=[
            pl.BlockSpec((TOP_K, tm), lambda i: (0, i), memory_space=pltpu.SMEM),
            pl.BlockSpec((tm, TOP_K), lambda i: (i, 0)),
            pl.BlockSpec((tm, d), lambda i: (i, 0)),
            pl.BlockSpec(memory_space=pl.ANY),
        ],
        out_specs=pl.BlockSpec((tm, d), lambda i: (i, 0)),
        out_shape=jax.ShapeDtypeStruct((t, d), F32),
        scratch_shapes=[pltpu.VMEM((TOP_K, tm, d), F32), pltpu.SemaphoreType.DMA(())],
        compiler_params=_cparams(("arbitrary",)),
        name="combine",
    )(dest, w_tok, x1, ys)


def _pad_heads(w, n_heads):
    d = w.shape[0]
    w = w.reshape(d, n_heads, HEAD_DIM)
    return jnp.pad(w, ((0, 0), (0, 0), (0, LANES - HEAD_DIM))).reshape(d, n_heads * LANES)


def _rope_tables(seq_len):
    rows = seq_len // GRID_W
    row = jnp.repeat(jnp.arange(rows, dtype=F32), GRID_W)
    col = jnp.tile(jnp.arange(GRID_W, dtype=F32), rows)
    inv_freq = ROPE_THETA ** (-jnp.arange(0, ROPE_AXIS_DIM, 2, dtype=F32) / ROPE_AXIS_DIM)
    ang_r = row[:, None] * inv_freq
    ang_c = col[:, None] * inv_freq
    zeros = jnp.zeros_like(ang_r)
    cr, sr, cc, sc = jnp.cos(ang_r), jnp.sin(ang_r), jnp.cos(ang_c), jnp.sin(ang_c)
    pad = jnp.zeros((seq_len, LANES - HEAD_DIM), F32)
    cos = jnp.concatenate([cr, cr, cc, cc, pad], axis=1)
    s_up = jnp.concatenate([-sr, zeros, -sc, zeros, pad], axis=1)
    s_dn = jnp.concatenate([zeros, sr, zeros, sc, pad], axis=1)
    return cos, s_up, s_dn


def _tiles(seq_len, n_tok):
    return dict(
        tm_in=min(512, seq_len),
        tq=min(256, seq_len),
        tk=min(512, seq_len),
        tm_out=min(512, seq_len),
        tn_route=min(1024, n_tok),
        tm_rows=min(256, n_tok),
        bm=256,
    )


def _layer(x, p, tabs, tl):
    b, l, d = x.shape
    t = b * l
    w_in = p["w_in"]
    o_q, o_k, o_v, o_z, o_xbc = ATTN_WIDTH, ATTN_WIDTH + KV_WIDTH, ATTN_WIDTH + 2 * KV_WIDTH, \
        ATTN_WIDTH + 2 * KV_WIDTH + SSM_WIDTH, ATTN_WIDTH + 2 * KV_WIDTH + SSM_WIDTH + XBC_WIDTH
    wq = _pad_heads(w_in[:, :o_q], N_HEADS).astype(BF16)
    wkv = jnp.concatenate([_pad_heads(w_in[:, o_q:o_k], N_KV), _pad_heads(w_in[:, o_k:o_v], N_KV)],
                          axis=1).astype(BF16)
    wz = w_in[:, o_v:o_z].astype(BF16)
    wxbc = w_in[:, o_z:o_xbc].astype(BF16)
    wdt = w_in[:, o_xbc:].astype(BF16)
    pad_gain = lambda g: jnp.pad(g, (0, LANES - HEAD_DIM)).reshape(1, LANES)
    q4, k4, v4, z, xbc, dt, dt_t = _inproj(
        x, p["norm_mix"].reshape(1, d), wq, wkv, wz, wxbc, wdt, wdt.T,
        pad_gain(p["q_norm"]), pad_gain(p["k_norm"]), *tabs, tl["tm_in"])

    attn = _attention(q4, k4, v4, tl["tq"], tl["tk"])

    dt_dir = dt.reshape(b, l, 2, SSM_HEADS).transpose(2, 0, 1, 3)
    dtt_dir = dt_t.reshape(b, 2, SSM_HEADS, l).transpose(1, 0, 2, 3)
    ssd_common = (xbc, dt_dir, dtt_dir, p["conv_w"], p["conv_b"].reshape(1, XBC_WIDTH),
                  p["dt_bias"].reshape(2, 1, SSM_HEADS), p["dt_bias"].reshape(2, SSM_HEADS, 1),
                  p["a_log"].reshape(2, 1, SSM_HEADS), p["a_log"].reshape(2, SSM_HEADS, 1))
    y_fwd = _ssd(*ssd_common, 0)
    ssm = _ssd(*ssd_common, 1, y_fwd=y_fwd, z=z,
               dskip=jnp.repeat(p["d_skip"], SSM_HEAD_DIM).reshape(1, SSM_WIDTH),
               gain=p["ssm_norm"].reshape(1, SSM_WIDTH))

    w_out = p["w_out"]
    wa = jnp.pad(w_out[:ATTN_WIDTH].reshape(N_HEADS, HEAD_DIM, d),
                 ((0, 0), (0, LANES - HEAD_DIM), (0, 0))).reshape(Q_PAD, d).astype(BF16)
    ws = w_out[ATTN_WIDTH:].astype(BF16)
    again = jnp.pad(p["attn_norm"].reshape(N_HEADS, HEAD_DIM),
                    ((0, 0), (0, LANES - HEAD_DIM))).reshape(1, Q_PAD)
    wr_t = p["w_router"].T
    wr_hi = wr_t.astype(BF16)
    wr_lo = (wr_t - wr_hi.astype(F32)).astype(BF16)
    x1, hn, logits_t = _outproj(attn, ssm.reshape(t, SSM_WIDTH), x.reshape(t, d), again, wa, ws,
                                p["norm_ffn"].reshape(1, d), wr_hi, wr_lo, tl["tm_out"])

    idx_t, w_t, rank_t, counts = _route(logits_t, p["b_router"].reshape(N_EXPERTS, 1), tl["tn_route"])

    bm = tl["bm"]
    counts = counts.reshape(N_EXPERTS)
    padded = (counts + bm - 1) // bm * bm
    pad_end = jnp.cumsum(padded)
    pad_start = pad_end - padded
    dest = pad_start[idx_t] + rank_t
    n_rows = t * TOP_K + N_EXPERTS * bm
    nb = n_rows // bm
    block_e = jnp.minimum(jnp.searchsorted(pad_end, jnp.arange(nb, dtype=jnp.int32) * bm, side="right"),
                          N_EXPERTS - 1).astype(jnp.int32)
    nused = (pad_end[-1:] // bm).astype(jnp.int32)

    xs = _dispatch(dest, hn, jnp.zeros((n_rows, d), F32), tl["tm_rows"])
    ys = _experts(block_e, nused, xs, p["w_gate"], p["b_gate"].reshape(N_EXPERTS, 1, d),
                  p["w_up"], p["b_up"].reshape(N_EXPERTS, 1, d),
                  p["w_down"], p["b_down"].reshape(N_EXPERTS, 1, d), bm)
    x2 = _combine(dest, w_t.T, x1, ys, tl["tm_rows"])
    return x2.reshape(b, l, d)


_PARAM_NAMES = ("norm_mix", "w_in", "q_norm", "k_norm", "conv_w", "conv_b", "dt_bias", "a_log", "d_skip",
                "ssm_norm", "attn_norm", "w_out", "norm_ffn", "w_router", "b_router", "w_gate", "b_gate",
                "w_up", "b_up", "w_down", "b_down")


def kernel(x, norm_mix, w_in, q_norm, k_norm, conv_w, conv_b, dt_bias, a_log, d_skip, ssm_norm, attn_norm,
           w_out, norm_ffn, w_router, b_router, w_gate, b_gate, w_up, b_up, w_down, b_down):
    params = dict(zip(_PARAM_NAMES, (norm_mix, w_in, q_norm, k_norm, conv_w, conv_b, dt_bias, a_log, d_skip,
                                     ssm_norm, attn_norm, w_out, norm_ffn, w_router, b_router, w_gate, b_gate,
                                     w_up, b_up, w_down, b_down)))
    b, l, _ = x.shape
    tabs = _rope_tables(l)
    tl = _tiles(l, b * l)
    for i in range(norm_mix.shape[0]):
        x = _layer(x, {k: v[i] for k, v in params.items()}, tabs, tl)
    return x
```

```python
import functools
import math

import jax
import jax.numpy as jnp
from jax import lax
from jax.experimental import pallas as pl
from jax.experimental.pallas import tpu as pltpu

F32 = jnp.float32
BF16 = jnp.bfloat16

D_MODEL = 1024
GRID_W = 64
HEAD_DIM = 64
N_HEADS = 8
N_KV = 2
HEADS_PER_KV = N_HEADS // N_KV
ATTN_WIDTH = N_HEADS * HEAD_DIM
KV_WIDTH = N_KV * HEAD_DIM
SSM_WIDTH = 512
SSM_HEADS = 8
SSM_HEAD_DIM = 64
SSM_GROUPS = 2
SSM_HEADS_PER_GROUP = SSM_HEADS // SSM_GROUPS
SSM_STATE = 128
CHUNK = 128
CONV_K = 5
XBC_WIDTH = SSM_WIDTH + 2 * SSM_GROUPS * SSM_STATE
ROPE_THETA = 10000.0
ROPE_AXIS_DIM = HEAD_DIM // 2
N_EXPERTS = 32
TOP_K = 4
SWIGLU_LIMIT = 7.0
SWIGLU_ALPHA = 1.702
EPS = 1e-6

LANES = 128
SUBLANES = 8
VMEM_LIMIT = 56 * 1024 * 1024

Q_PAD = N_HEADS * LANES


def _cparams(sem):
    return pltpu.CompilerParams(dimension_semantics=sem, vmem_limit_bytes=VMEM_LIMIT)


def _dot(a, b):
    return jnp.dot(a, b, preferred_element_type=F32)


def _dot_nt(a, b):
    return lax.dot_general(a, b, (((1,), (1,)), ((), ())), preferred_element_type=F32)


def _split3(x):
    hi = x.astype(BF16)
    r1 = x - hi.astype(F32)
    mid = r1.astype(BF16)
    lo = (r1 - mid.astype(F32)).astype(BF16)
    return hi, mid, lo


def _sigmoid(x):
    return 1.0 / (1.0 + jnp.exp(-x))


def _softplus(x):
    return jnp.maximum(x, 0.0) + jnp.log(1.0 + jnp.exp(-jnp.abs(x)))


def _inproj_kernel(x_ref, g_ref, wqT_ref, wk_ref, wvT_ref, wz_ref, wxbc_ref, wdt_ref, wdtT_ref,
                   qg_ref, kg_ref, cos_ref, s1_ref, s2_ref, cosT_ref, sinT_ref,
                   q_ref, k_ref, v_ref, z_ref, xbc_ref, dt_ref, dtT_ref):
    x = x_ref[0]
    tm = x.shape[0]
    hn = (x * lax.rsqrt(jnp.mean(x * x, axis=-1, keepdims=True) + EPS) * g_ref[...]).astype(BF16)

    cos_t = cosT_ref[...]
    sin_t = sinT_ref[...]
    q_scale = math.log2(math.e) / math.sqrt(HEAD_DIM)
    acc_qt = _dot_nt(wqT_ref[...], hn)
    half = ROPE_AXIS_DIM // 2
    for h in range(N_HEADS):
        a = acc_qt[HEAD_DIM * h:HEAD_DIM * (h + 1)]
        r = lax.rsqrt(jnp.sum(a * a, axis=0, keepdims=True) * (1.0 / HEAD_DIM) + EPS)
        y = a * r * qg_ref[...]
        swapped = jnp.concatenate([y[half:2 * half], y[0:half], y[3 * half:4 * half], y[2 * half:3 * half]], axis=0)
        y = (y * cos_t + swapped * sin_t) * q_scale
        q_ref[0, h, 0:HEAD_DIM, :] = y.astype(BF16)
        q_ref[0, h, HEAD_DIM:, :] = jnp.zeros((LANES - HEAD_DIM, tm), BF16)

    cos = cos_ref[...]
    s1 = s1_ref[...]
    s2 = s2_ref[...]
    acc_k = _dot(hn, wk_ref[...])
    for g in range(N_KV):
        a = acc_k[:, LANES * g:LANES * (g + 1)]
        r = lax.rsqrt(jnp.sum(a * a, axis=-1, keepdims=True) * (1.0 / HEAD_DIM) + EPS)
        y = a * r * kg_ref[...]
        y = y * cos + pltpu.roll(y, LANES - 16, 1) * s1 + pltpu.roll(y, 16, 1) * s2
        k_ref[0, g] = y.astype(BF16)

    acc_vt = _dot_nt(wvT_ref[...], hn)
    ones_row = jnp.where(lax.broadcasted_iota(jnp.int32, (LANES - HEAD_DIM, tm), 0) == 0, 1.0, 0.0).astype(BF16)
    for g in range(N_KV):
        v_ref[0, g, 0:HEAD_DIM, :] = acc_vt[HEAD_DIM * g:HEAD_DIM * (g + 1)].astype(BF16)
        v_ref[0, g, HEAD_DIM:, :] = ones_row

    z_ref[0] = _dot(hn, wz_ref[...])
    xbc_ref[0] = _dot(hn, wxbc_ref[...])
    dt_ref[0] = _dot(hn, wdt_ref[...])
    dtT_ref[0] = _dot_nt(wdtT_ref[...], hn)


def _inproj(x, gain, wqT, wk, wvT, wz, wxbc, wdt, wdtT, qg, kg, tabs, tm):
    b, l, d = x.shape
    nt = l // tm
    const = lambda shape: pl.BlockSpec(shape, lambda bi, i: (0,) * len(shape))
    tab = pl.BlockSpec((tm, LANES), lambda bi, i: (i, 0))
    tab_t = pl.BlockSpec((HEAD_DIM, tm), lambda bi, i: (0, i))
    return pl.pallas_call(
        _inproj_kernel,
        grid=(b, nt),
        in_specs=[
            pl.BlockSpec((1, tm, d), lambda bi, i: (bi, i, 0)),
            const((1, d)), const(wqT.shape), const(wk.shape), const(wvT.shape), const(wz.shape),
            const(wxbc.shape), const(wdt.shape), const(wdtT.shape), const((HEAD_DIM, 1)), const((1, LANES)),
            tab, tab, tab, tab_t, tab_t,
        ],
        out_specs=[
            pl.BlockSpec((1, N_HEADS, LANES, tm), lambda bi, i: (bi, 0, 0, i)),
            pl.BlockSpec((1, N_KV, tm, LANES), lambda bi, i: (bi, 0, i, 0)),
            pl.BlockSpec((1, N_KV, LANES, tm), lambda bi, i: (bi, 0, 0, i)),
            pl.BlockSpec((1, tm, SSM_WIDTH), lambda bi, i: (bi, i, 0)),
            pl.BlockSpec((1, tm, XBC_WIDTH), lambda bi, i: (bi, i, 0)),
            pl.BlockSpec((1, tm, 2 * SSM_HEADS), lambda bi, i: (bi, i, 0)),
            pl.BlockSpec((1, 2 * SSM_HEADS, tm), lambda bi, i: (bi, 0, i)),
        ],
        out_shape=[
            jax.ShapeDtypeStruct((b, N_HEADS, LANES, l), BF16),
            jax.ShapeDtypeStruct((b, N_KV, l, LANES), BF16),
            jax.ShapeDtypeStruct((b, N_KV, LANES, l), BF16),
            jax.ShapeDtypeStruct((b, l, SSM_WIDTH), F32),
            jax.ShapeDtypeStruct((b, l, XBC_WIDTH), F32),
            jax.ShapeDtypeStruct((b, l, 2 * SSM_HEADS), F32),
            jax.ShapeDtypeStruct((b, 2 * SSM_HEADS, l), F32),
        ],
        compiler_params=_cparams(("parallel", "parallel")),
        name="inproj",
    )(x, gain, wqT, wk, wvT, wz, wxbc, wdt, wdtT, qg, kg, *tabs)


def _attn_kernel(q_ref, k_ref, v_ref, o_ref, qcat_ref, m_ref, acc_ref, *, tk):
    tq = q_ref.shape[3]
    seq = k_ref.shape[2]
    cols = HEADS_PER_KV * tq
    for hh in range(HEADS_PER_KV):
        qcat_ref[:, hh * tq:(hh + 1) * tq] = q_ref[0, hh]
    m_ref[...] = jnp.full((1, cols), -jnp.inf, F32)
    acc_ref[...] = jnp.zeros((LANES, cols), F32)

    def body(j, carry):
        off = pl.multiple_of(j * tk, tk)
        kc = k_ref[0, 0, pl.ds(off, tk), :]
        vc = v_ref[0, 0, :, pl.ds(off, tk)]
        s = _dot(kc, qcat_ref[...])
        m_old = m_ref[...]
        m_new = jnp.maximum(m_old, jnp.max(s, axis=0, keepdims=True))
        p = jnp.exp2(s - m_new).astype(BF16)
        acc_ref[...] = jnp.exp2(m_old - m_new) * acc_ref[...] + _dot(vc, p)
        m_ref[...] = m_new
        return carry

    lax.fori_loop(0, seq // tk, body, 0)
    acc = acc_ref[...]
    out = acc[0:HEAD_DIM] / acc[HEAD_DIM:HEAD_DIM + 1]
    out = jnp.concatenate([out, jnp.zeros((LANES - HEAD_DIM, cols), F32)], axis=0)
    for hh in range(HEADS_PER_KV):
        o_ref[0, hh] = out[:, hh * tq:(hh + 1) * tq].T


def _attention(q, k, v, tq, tk):
    b, _, _, l = q.shape
    cols = HEADS_PER_KV * tq
    return pl.pallas_call(
        functools.partial(_attn_kernel, tk=tk),
        grid=(b, N_KV, l // tq),
        in_specs=[
            pl.BlockSpec((1, HEADS_PER_KV, LANES, tq), lambda bi, g, i: (bi, g, 0, i)),
            pl.BlockSpec((1, 1, l, LANES), lambda bi, g, i: (bi, g, 0, 0)),
            pl.BlockSpec((1, 1, LANES, l), lambda bi, g, i: (bi, g, 0, 0)),
        ],
        out_specs=pl.BlockSpec((1, HEADS_PER_KV, tq, LANES), lambda bi, g, i: (bi, g, i, 0)),
        out_shape=jax.ShapeDtypeStruct((b, N_HEADS, l, LANES), F32),
        scratch_shapes=[
            pltpu.VMEM((LANES, cols), BF16),
            pltpu.VMEM((1, cols), F32),
            pltpu.VMEM((LANES, cols), F32),
        ],
        compiler_params=_cparams(("parallel", "parallel", "parallel")),
        name="attn",
    )(q, k, v)


def _ssd_kernel(*refs, reverse):
    if reverse:
        (cur_ref, prev_ref, next_ref, dt_ref, dtT_ref, convw_ref, convb_ref, dtb_row_ref, dtb_col_ref,
         alog_row_ref, alog_col_ref, yin_ref, z_ref, dskip_ref, gain_ref,
         y_ref, pad_ref, state_ref) = refs
    else:
        (cur_ref, prev_ref, next_ref, dt_ref, dtT_ref, convw_ref, convb_ref, dtb_row_ref, dtb_col_ref,
         alog_row_ref, alog_col_ref, y_ref, pad_ref, state_ref) = refs
    c = pl.program_id(1)
    nc = pl.num_programs(1)
    chunk = (nc - 1 - c) if reverse else c

    @pl.when(c == 0)
    def _():
        state_ref[...] = jnp.zeros(state_ref.shape, F32)

    halo = SUBLANES
    pad_ref[0:halo, :] = jnp.where(chunk > 0, prev_ref[0], 0.0)
    pad_ref[halo:halo + CHUNK, :] = cur_ref[0]
    pad_ref[halo + CHUNK:, :] = jnp.where(chunk < nc - 1, next_ref[0], 0.0)
    conv = jnp.broadcast_to(convb_ref[...], (CHUNK, XBC_WIDTH))
    for j in range(CONV_K):
        conv = conv + convw_ref[j:j + 1, :] * pad_ref[pl.ds(halo - CONV_K // 2 + j, CHUNK), :]
    xc = conv * _sigmoid(conv)
    xs = xc[:, :SSM_WIDTH]

    dt_col = _softplus(dt_ref[0, 0] + dtb_row_ref[0])
    dt_row = _softplus(dtT_ref[0, 0] + dtb_col_ref[0])
    da_col = dt_col * -jnp.exp(alog_row_ref[0])
    da_row = dt_row * -jnp.exp(alog_col_ref[0])

    ri = lax.broadcasted_iota(jnp.int32, (CHUNK, CHUNK), 0)
    ci = lax.broadcasted_iota(jnp.int32, (CHUNK, CHUNK), 1)
    low = ci <= ri
    up = ci >= ri
    low_m = jnp.where(low, 1.0, 0.0).astype(BF16)
    up_m = jnp.where(up, 1.0, 0.0).astype(BF16)
    col_m, row_m, mask = (up_m, low_m, up) if reverse else (low_m, up_m, low)
    ch, cm, cl = _split3(da_col)
    acum_col = _dot(col_m, ch) + _dot(col_m, cm) + _dot(col_m, cl)
    rh, rm, rl = _split3(da_row)
    acum_row = _dot(rh, row_m) + _dot(rm, row_m) + _dot(rl, row_m)
    end = 0 if reverse else CHUNK - 1
    a_end_row = acum_col[end:end + 1, :]
    w_col = dt_col * jnp.exp(a_end_row - acum_col)
    e_col = jnp.exp(acum_col)
    chunk_decay = jnp.exp(a_end_row)

    pieces = []
    for g in range(SSM_GROUPS):
        bm = xc[:, SSM_WIDTH + g * SSM_STATE:SSM_WIDTH + (g + 1) * SSM_STATE]
        cmat = xc[:, SSM_WIDTH + (SSM_GROUPS + g) * SSM_STATE:SSM_WIDTH + (SSM_GROUPS + g + 1) * SSM_STATE]
        bm16 = bm.astype(BF16)
        cm16 = cmat.astype(BF16)
        cb = _dot_nt(cm16, bm16)
        bt16 = bm.T.astype(BF16)
        for hh in range(SSM_HEADS_PER_GROUP):
            h = g * SSM_HEADS_PER_GROUP + hh
            xh = xs[:, h * SSM_HEAD_DIM:(h + 1) * SSM_HEAD_DIM]
            seg = acum_col[:, h:h + 1] - acum_row[h:h + 1, :]
            decay = jnp.exp(jnp.where(mask, seg, -jnp.inf))
            mat = cb * decay * dt_row[h:h + 1, :]
            y_diag = _dot(mat.astype(BF16), xh.astype(BF16))
            prev_state = state_ref[h]
            y_off = _dot(cm16, prev_state.astype(BF16)) * e_col[:, h:h + 1]
            pieces.append(y_diag + y_off)
            xw = xh * w_col[:, h:h + 1]
            state_ref[h] = prev_state * chunk_decay[:, h:h + 1] + _dot(bt16, xw.astype(BF16))
    y = jnp.concatenate(pieces, axis=1)

    if not reverse:
        y_ref[0] = y
    else:
        y = y + yin_ref[0] + xs * dskip_ref[...]
        zz = z_ref[0]
        y = y * (zz * _sigmoid(zz))
        gw = SSM_WIDTH // SSM_GROUPS
        outs = []
        for g in range(SSM_GROUPS):
            yg = y[:, g * gw:(g + 1) * gw]
            outs.append(yg * lax.rsqrt(jnp.mean(yg * yg, axis=-1, keepdims=True) + EPS))
        y_ref[0] = jnp.concatenate(outs, axis=1) * gain_ref[...]


def _ssd(xbc, dt, dtT, convw, convb, dtb_row, dtb_col, alog_row, alog_col, direction,
         y_fwd=None, z=None, dskip=None, gain=None):
    b, l, _ = xbc.shape
    nc = l // CHUNK
    reverse = direction == 1
    blocks_per_chunk = CHUNK // SUBLANES
    nblk = l // SUBLANES

    def pos(c):
        return (nc - 1 - c) if reverse else c

    const = lambda shape: pl.BlockSpec(shape, lambda bi, c: (0,) * len(shape))
    dsel = lambda shape: pl.BlockSpec(shape, lambda bi, c: (direction,) + (0,) * (len(shape) - 1))
    in_specs = [
        pl.BlockSpec((1, CHUNK, XBC_WIDTH), lambda bi, c: (bi, pos(c), 0)),
        pl.BlockSpec((1, SUBLANES, XBC_WIDTH),
                     lambda bi, c: (bi, jnp.maximum(pos(c) * blocks_per_chunk - 1, 0), 0)),
        pl.BlockSpec((1, SUBLANES, XBC_WIDTH),
                     lambda bi, c: (bi, jnp.minimum((pos(c) + 1) * blocks_per_chunk, nblk - 1), 0)),
        pl.BlockSpec((1, 1, CHUNK, SSM_HEADS), lambda bi, c: (direction, bi, pos(c), 0)),
        pl.BlockSpec((1, 1, SSM_HEADS, CHUNK), lambda bi, c: (direction, bi, 0, pos(c))),
        const((CONV_K, XBC_WIDTH)), const((1, XBC_WIDTH)),
        dsel((1, 1, SSM_HEADS)), dsel((1, SSM_HEADS, 1)), dsel((1, 1, SSM_HEADS)), dsel((1, SSM_HEADS, 1)),
    ]
    args = [xbc, xbc, xbc, dt, dtT, convw, convb, dtb_row, dtb_col, alog_row, alog_col]
    if reverse:
        in_specs += [
            pl.BlockSpec((1, CHUNK, SSM_WIDTH), lambda bi, c: (bi, pos(c), 0)),
            pl.BlockSpec((1, CHUNK, SSM_WIDTH), lambda bi, c: (bi, pos(c), 0)),
            const((1, SSM_WIDTH)), const((1, SSM_WIDTH)),
        ]
        args += [y_fwd, z, dskip, gain]
    return pl.pallas_call(
        functools.partial(_ssd_kernel, reverse=reverse),
        grid=(b, nc),
        in_specs=in_specs,
        out_specs=pl.BlockSpec((1, CHUNK, SSM_WIDTH), lambda bi, c: (bi, pos(c), 0)),
        out_shape=jax.ShapeDtypeStruct((b, l, SSM_WIDTH), F32),
        scratch_shapes=[
            pltpu.VMEM((CHUNK + 2 * SUBLANES, XBC_WIDTH), F32),
            pltpu.VMEM((SSM_HEADS, SSM_STATE, SSM_HEAD_DIM), F32),
        ],
        compiler_params=_cparams(("parallel", "arbitrary")),
        name="ssd_bwd" if reverse else "ssd_fwd",
    )(*args)


def _outproj_kernel(attn_ref, ssm_ref, x_ref, again_ref, wa_ref, ws_ref, fgain_ref, wr_hi_ref, wr_lo_ref,
                    x1_ref, hn_ref, logit_ref):
    a = jnp.concatenate([attn_ref[0, h] for h in range(N_HEADS)], axis=1)
    r = lax.rsqrt(jnp.sum(a * a, axis=-1, keepdims=True) * (1.0 / ATTN_WIDTH) + EPS)
    an = (a * r * again_ref[...]).astype(BF16)
    x1 = x_ref[...] + _dot(an, wa_ref[...]) + _dot(ssm_ref[...].astype(BF16), ws_ref[...])
    x1_ref[...] = x1
    hn = x1 * lax.rsqrt(jnp.mean(x1 * x1, axis=-1, keepdims=True) + EPS) * fgain_ref[...]
    hn_ref[...] = hn
    h_hi = hn.astype(BF16)
    h_lo = (hn - h_hi.astype(F32)).astype(BF16)
    w_hi = wr_hi_ref[...]
    logit_ref[...] = _dot_nt(w_hi, h_hi) + _dot_nt(w_hi, h_lo) + _dot_nt(wr_lo_ref[...], h_hi)


def _outproj(attn, ssm, x, again, wa, ws, fgain, wr_hi, wr_lo, tm):
    b, _, l, _ = attn.shape
    t = b * l
    nt = l // tm
    const = lambda shape: pl.BlockSpec(shape, lambda i: (0,) * len(shape))
    return pl.pallas_call(
        _outproj_kernel,
        grid=(t // tm,),
        in_specs=[
            pl.BlockSpec((1, N_HEADS, tm, LANES), lambda i: (i // nt, 0, i % nt, 0)),
            pl.BlockSpec((tm, SSM_WIDTH), lambda i: (i, 0)),
            pl.BlockSpec((tm, D_MODEL), lambda i: (i, 0)),
            const((1, Q_PAD)), const(wa.shape), const(ws.shape), const((1, D_MODEL)),
            const(wr_hi.shape), const(wr_lo.shape),
        ],
        out_specs=[
            pl.BlockSpec((tm, D_MODEL), lambda i: (i, 0)),
            pl.BlockSpec((tm, D_MODEL), lambda i: (i, 0)),
            pl.BlockSpec((N_EXPERTS, tm), lambda i: (0, i)),
        ],
        out_shape=[
            jax.ShapeDtypeStruct((t, D_MODEL), F32),
            jax.ShapeDtypeStruct((t, D_MODEL), F32),
            jax.ShapeDtypeStruct((N_EXPERTS, t), F32),
        ],
        compiler_params=_cparams(("parallel",)),
        name="outproj",
    )(attn, ssm, x, again, wa, ws, fgain, wr_hi, wr_lo)


def _route_kernel(logit_ref, bias_ref, idx_ref, w_ref, rank_ref, count_ref, carry_ref):
    i = pl.program_id(0)
    tn = logit_ref.shape[1]

    @pl.when(i == 0)
    def _():
        carry_ref[...] = jnp.zeros(carry_ref.shape, F32)

    lg = logit_ref[...] + bias_ref[...]
    eid = lax.broadcasted_iota(jnp.int32, (N_EXPERTS, tn), 0).astype(F32)
    chosen = jnp.zeros((N_EXPERTS, tn), F32)
    vals, ids = [], []
    for _ in range(TOP_K):
        mx = jnp.max(lg, axis=0, keepdims=True)
        sel = jnp.min(jnp.where(lg == mx, eid, float(N_EXPERTS)), axis=0, keepdims=True)
        hit = eid == sel
        chosen = jnp.where(hit, 1.0, chosen)
        lg = jnp.where(hit, -jnp.inf, lg)
        vals.append(mx)
        ids.append(sel)
    ex = [jnp.exp(v - vals[0]) for v in vals]
    den = ex[0] + ex[1] + ex[2] + ex[3]
    ri = lax.broadcasted_iota(jnp.int32, (tn, tn), 0)
    ci = lax.broadcasted_iota(jnp.int32, (tn, tn), 1)
    upper = jnp.where(ri <= ci, 1.0, 0.0).astype(BF16)
    incl = _dot(chosen.astype(BF16), upper)
    rank_e = incl - chosen + carry_ref[...]
    carry_ref[...] = carry_ref[...] + incl[:, tn - 1:tn]
    for k in range(TOP_K):
        idx_ref[k:k + 1, :] = ids[k].astype(jnp.int32)
        w_ref[k:k + 1, :] = ex[k] / den
        rank_ref[k:k + 1, :] = jnp.sum(jnp.where(eid == ids[k], rank_e, 0.0), axis=0,
                                       keepdims=True).astype(jnp.int32)
    count_ref[...] = carry_ref[...].astype(jnp.int32)


def _route(logits_t, bias_col, tn):
    t = logits_t.shape[1]
    tok = pl.BlockSpec((TOP_K, tn), lambda i: (0, i))
    return pl.pallas_call(
        _route_kernel,
        grid=(t // tn,),
        in_specs=[pl.BlockSpec((N_EXPERTS, tn), lambda i: (0, i)),
                  pl.BlockSpec((N_EXPERTS, 1), lambda i: (0, 0))],
        out_specs=[tok, tok, tok, pl.BlockSpec((N_EXPERTS, 1), lambda i: (0, 0))],
        out_shape=[
            jax.ShapeDtypeStruct((TOP_K, t), jnp.int32),
            jax.ShapeDtypeStruct((TOP_K, t), F32),
            jax.ShapeDtypeStruct((TOP_K, t), jnp.int32),
            jax.ShapeDtypeStruct((N_EXPERTS, 1), jnp.int32),
        ],
        scratch_shapes=[pltpu.VMEM((N_EXPERTS, 1), F32)],
        compiler_params=_cparams(("arbitrary",)),
        name="route",
    )(logits_t, bias_col)


def _dest_kernel(idx_ref, rank_ref, start_ref, dest_ref):
    tn = idx_ref.shape[1]
    eid = lax.broadcasted_iota(jnp.int32, (N_EXPERTS, tn), 0)
    start = start_ref[...].astype(F32)
    for k in range(TOP_K):
        base = jnp.sum(jnp.where(eid == idx_ref[k:k + 1, :], start, 0.0), axis=0, keepdims=True)
        dest_ref[k:k + 1, :] = base.astype(jnp.int32) + rank_ref[k:k + 1, :]


def _dest(idx_t, rank_t, start_col, tn):
    t = idx_t.shape[1]
    tok = pl.BlockSpec((TOP_K, tn), lambda i: (0, i))
    return pl.pallas_call(
        _dest_kernel,
        grid=(t // tn,),
        in_specs=[tok, tok, pl.BlockSpec((N_EXPERTS, 1), lambda i: (0, 0))],
        out_specs=tok,
        out_shape=jax.ShapeDtypeStruct((TOP_K, t), jnp.int32),
        compiler_params=_cparams(("parallel",)),
        name="dest",
    )(idx_t, rank_t, start_col)


def _dispatch_kernel(dest_ref, hn_ref, xs_in_ref, xs_ref, sem):
    del xs_in_ref
    tm = hn_ref.shape[0]

    def row_copy(k, t):
        return pltpu.make_async_copy(hn_ref.at[pl.ds(t, 1), :],
                                     xs_ref.at[pl.ds(dest_ref[k, t], 1), :], sem)

    def issue(t, carry):
        for k in range(TOP_K):
            row_copy(k, t).start()
        return carry

    lax.fori_loop(0, tm, issue, 0)
    for _ in range(TOP_K):
        pltpu.make_async_copy(hn_ref, xs_ref.at[pl.ds(0, tm), :], sem).wait()


def _dispatch(dest, hn, xs_init, tm):
    t, d = hn.shape
    return pl.pallas_call(
        _dispatch_kernel,
        grid=(t // tm,),
        in_specs=[
            pl.BlockSpec((TOP_K, tm), lambda i: (0, i), memory_space=pltpu.SMEM),
            pl.BlockSpec((tm, d), lambda i: (i, 0)),
            pl.BlockSpec(memory_space=pl.ANY),
        ],
        out_specs=pl.BlockSpec(memory_space=pl.ANY),
        out_shape=jax.ShapeDtypeStruct(xs_init.shape, F32),
        scratch_shapes=[pltpu.SemaphoreType.DMA(())],
        input_output_aliases={2: 0},
        compiler_params=_cparams(("arbitrary",)),
        name="dispatch",
    )(dest, hn, xs_init)


def _expert_kernel(block_e_ref, nused_ref, xs_ref, wg_ref, bg_ref, wu_ref, bu_ref, wd_ref, bd_ref,
                   ys_ref, wg16, wu16, wd16):
    b = pl.program_id(0)
    used = b < nused_ref[0]
    prev_e = block_e_ref[jnp.maximum(b - 1, 0)]
    fresh = jnp.logical_or(b == 0, block_e_ref[b] != prev_e)

    @pl.when(jnp.logical_and(used, fresh))
    def _():
        wg16[...] = wg_ref[0].astype(BF16)
        wu16[...] = wu_ref[0].astype(BF16)
        wd16[...] = wd_ref[0].astype(BF16)

    @pl.when(used)
    def _():
        x = xs_ref[...].astype(BF16)
        g = _dot(x, wg16[...]) + bg_ref[0]
        u = _dot(x, wu16[...]) + bu_ref[0]
        g = jnp.minimum(g, SWIGLU_LIMIT)
        u = jnp.clip(u, -SWIGLU_LIMIT, SWIGLU_LIMIT)
        act = g * _sigmoid(SWIGLU_ALPHA * g) * (u + 1.0)
        ys_ref[...] = _dot(act.astype(BF16), wd16[...]) + bd_ref[0]

    @pl.when(jnp.logical_not(used))
    def _():
        ys_ref[...] = jnp.zeros(ys_ref.shape, F32)


def _experts(block_e, nused, xs, wg, bg, wu, bu, wd, bd, bm, layer):
    n_rows, d = xs.shape
    nb = n_rows // bm
    base = layer * N_EXPERTS
    wspec = pl.BlockSpec((1, d, d), lambda i, be, nu: (base + be[i], 0, 0))
    bspec = pl.BlockSpec((1, 1, d), lambda i, be, nu: (base + be[i], 0, 0))
    row = pl.BlockSpec((bm, d), lambda i, be, nu: (i, 0))
    return pl.pallas_call(
        _expert_kernel,
        grid_spec=pltpu.PrefetchScalarGridSpec(
            num_scalar_prefetch=2,
            grid=(nb,),
            in_specs=[row, wspec, bspec, wspec, bspec, wspec, bspec],
            out_specs=row,
            scratch_shapes=[pltpu.VMEM((d, d), BF16)] * 3,
        ),
        out_shape=jax.ShapeDtypeStruct((n_rows, d), F32),
        compiler_params=_cparams(("arbitrary",)),
        name="experts",
    )(block_e, nused, xs, wg, bg, wu, bu, wd, bd)


def _combine_kernel(dest_ref, w_ref, x1_ref, ys_ref, out_ref, buf, sem):
    tm = x1_ref.shape[0]

    def issue(t, carry):
        for k in range(TOP_K):
            pltpu.make_async_copy(ys_ref.at[pl.ds(dest_ref[k, t], 1), :],
                                  buf.at[k, pl.ds(t, 1), :], sem).start()
        return carry

    lax.fori_loop(0, tm, issue, 0)
    for k in range(TOP_K):
        pltpu.make_async_copy(ys_ref.at[pl.ds(0, tm), :], buf.at[k], sem).wait()
    w = w_ref[...]
    acc = x1_ref[...]
    for k in range(TOP_K):
        acc = acc + w[:, k:k + 1] * buf[k]
    out_ref[...] = acc


def _combine(dest, w_tok, x1, ys, tm):
    t, d = x1.shape
    return pl.pallas_call(
        _combine_kernel,
        grid=(t // tm,),
        in_specs=[
            pl.BlockSpec((TOP_K, tm), lambda i: (0, i), memory_space=pltpu.SMEM),
            pl.BlockSpec((tm, TOP_K), lambda i: (i, 0)),
            pl.BlockSpec((tm, d), lambda i: (i, 0)),
            pl.BlockSpec(memory_space=pl.ANY),
        ],
        out_specs=pl.BlockSpec((tm, d), lambda i: (i, 0)),
        out_shape=jax.ShapeDtypeStruct((t, d), F32),
        scratch_shapes=[pltpu.VMEM((TOP_K, tm, d), F32), pltpu.SemaphoreType.DMA(())],
        compiler_params=_cparams(("arbitrary",)),
        name="combine",
    )(dest, w_tok, x1, ys)


def _pad_heads(w, n_heads):
    d = w.shape[0]
    w = w.reshape(d, n_heads, HEAD_DIM)
    return jnp.pad(w, ((0, 0), (0, 0), (0, LANES - HEAD_DIM))).reshape(d, n_heads * LANES)


def _rope_tables(seq_len):
    rows = seq_len // GRID_W
    inv_freq = ROPE_THETA ** (-jnp.arange(0, ROPE_AXIS_DIM, 2, dtype=F32) / ROPE_AXIS_DIM)
    ang_r = jnp.arange(rows, dtype=F32)[:, None] * inv_freq
    ang_c = jnp.arange(GRID_W, dtype=F32)[:, None] * inv_freq
    expand_r = lambda a: jnp.repeat(a, GRID_W, axis=0)
    expand_c = lambda a: jnp.tile(a, (rows, 1))
    cr, sr = expand_r(jnp.cos(ang_r)), expand_r(jnp.sin(ang_r))
    cc, sc = expand_c(jnp.cos(ang_c)), expand_c(jnp.sin(ang_c))
    zeros = jnp.zeros_like(cr)
    pad = jnp.zeros((seq_len, LANES - HEAD_DIM), F32)
    cos = jnp.concatenate([cr, cr, cc, cc, pad], axis=1)
    s_up = jnp.concatenate([-sr, zeros, -sc, zeros, pad], axis=1)
    s_dn = jnp.concatenate([zeros, sr, zeros, sc, pad], axis=1)
    cos_t = jnp.concatenate([cr, cr, cc, cc], axis=1).T
    sin_t = jnp.concatenate([-sr, sr, -sc, sc], axis=1).T
    return cos, s_up, s_dn, cos_t, sin_t


def _tiles(seq_len, n_tok):
    return dict(
        tm_in=min(512, seq_len),
        tq=min(256, seq_len),
        tk=min(256, seq_len),
        tm_out=min(512, seq_len),
        tn_route=min(1024, n_tok),
        tm_rows=min(256, n_tok),
        bm=256,
    )


def _layer(x, p, i, tabs, tl):
    b, l, d = x.shape
    t = b * l
    w_in = p["w_in"][i]
    o_q, o_k, o_v, o_z, o_xbc = ATTN_WIDTH, ATTN_WIDTH + KV_WIDTH, ATTN_WIDTH + 2 * KV_WIDTH, \
        ATTN_WIDTH + 2 * KV_WIDTH + SSM_WIDTH, ATTN_WIDTH + 2 * KV_WIDTH + SSM_WIDTH + XBC_WIDTH
    wq_t = w_in[:, :o_q].T.astype(BF16)
    wk = _pad_heads(w_in[:, o_q:o_k], N_KV).astype(BF16)
    wv_t = w_in[:, o_k:o_v].T.astype(BF16)
    wz = w_in[:, o_v:o_z].astype(BF16)
    wxbc = w_in[:, o_z:o_xbc].astype(BF16)
    wdt = w_in[:, o_xbc:].astype(BF16)
    q4, k4, v4, z, xbc, dt, dt_t = _inproj(
        x, p["norm_mix"][i].reshape(1, d), wq_t, wk, wv_t, wz, wxbc, wdt, wdt.T,
        p["q_norm"][i].reshape(HEAD_DIM, 1), jnp.pad(p["k_norm"][i], (0, LANES - HEAD_DIM)).reshape(1, LANES),
        tabs, tl["tm_in"])

    attn = _attention(q4, k4, v4, tl["tq"], tl["tk"])

    dt_dir = dt.reshape(b, l, 2, SSM_HEADS).transpose(2, 0, 1, 3)
    dtt_dir = dt_t.reshape(b, 2, SSM_HEADS, l).transpose(1, 0, 2, 3)
    ssd_common = (xbc, dt_dir, dtt_dir, p["conv_w"][i], p["conv_b"][i].reshape(1, XBC_WIDTH),
                  p["dt_bias"][i].reshape(2, 1, SSM_HEADS), p["dt_bias"][i].reshape(2, SSM_HEADS, 1),
                  p["a_log"][i].reshape(2, 1, SSM_HEADS), p["a_log"][i].reshape(2, SSM_HEADS, 1))
    y_fwd = _ssd(*ssd_common, 0)
    ssm = _ssd(*ssd_common, 1, y_fwd=y_fwd, z=z,
               dskip=jnp.repeat(p["d_skip"][i], SSM_HEAD_DIM).reshape(1, SSM_WIDTH),
               gain=p["ssm_norm"][i].reshape(1, SSM_WIDTH))

    w_out = p["w_out"][i]
    wa = jnp.pad(w_out[:ATTN_WIDTH].reshape(N_HEADS, HEAD_DIM, d),
                 ((0, 0), (0, LANES - HEAD_DIM), (0, 0))).reshape(Q_PAD, d).astype(BF16)
    ws = w_out[ATTN_WIDTH:].astype(BF16)
    again = jnp.pad(p["attn_norm"][i].reshape(N_HEADS, HEAD_DIM),
                    ((0, 0), (0, LANES - HEAD_DIM))).reshape(1, Q_PAD)
    wr_t = p["w_router"][i].T
    wr_hi = wr_t.astype(BF16)
    wr_lo = (wr_t - wr_hi.astype(F32)).astype(BF16)
    x1, hn, logits_t = _outproj(attn, ssm.reshape(t, SSM_WIDTH), x.reshape(t, d), again, wa, ws,
                                p["norm_ffn"][i].reshape(1, d), wr_hi, wr_lo, tl["tm_out"])

    idx_t, w_t, rank_t, counts = _route(logits_t, p["b_router"][i].reshape(N_EXPERTS, 1), tl["tn_route"])

    bm = tl["bm"]
    padded = (counts.reshape(N_EXPERTS) + bm - 1) // bm * bm
    pad_end = jnp.cumsum(padded)
    pad_start = pad_end - padded
    dest = _dest(idx_t, rank_t, pad_start.reshape(N_EXPERTS, 1).astype(jnp.int32), tl["tn_route"])
    n_rows = t * TOP_K + N_EXPERTS * bm
    nb = n_rows // bm
    block_row = jnp.arange(nb, dtype=jnp.int32) * bm
    block_e = jnp.minimum(jnp.sum(pad_end[None, :] <= block_row[:, None], axis=1), N_EXPERTS - 1).astype(jnp.int32)
    nused = (pad_end[-1:] // bm).astype(jnp.int32)

    n_all = p["w_gate"].shape[0] * N_EXPERTS
    xs = _dispatch(dest, hn, jnp.zeros((n_rows, d), F32), tl["tm_rows"])
    ys = _experts(block_e, nused, xs,
                  p["w_gate"].reshape(n_all, d, d), p["b_gate"].reshape(n_all, 1, d),
                  p["w_up"].reshape(n_all, d, d), p["b_up"].reshape(n_all, 1, d),
                  p["w_down"].reshape(n_all, d, d), p["b_down"].reshape(n_all, 1, d), bm, i)
    x2 = _combine(dest, w_t.T, x1, ys, tl["tm_rows"])
    return x2.reshape(b, l, d)


_PARAM_NAMES = ("norm_mix", "w_in", "q_norm", "k_norm", "conv_w", "conv_b", "dt_bias", "a_log", "d_skip",
                "ssm_norm", "attn_norm", "w_out", "norm_ffn", "w_router", "b_router", "w_gate", "b_gate",
                "w_up", "b_up", "w_down", "b_down")


def kernel(x, norm_mix, w_in, q_norm, k_norm, conv_w, conv_b, dt_bias, a_log, d_skip, ssm_norm, attn_norm,
           w_out, norm_ffn, w_router, b_router, w_gate, b_gate, w_up, b_up, w_down, b_down):
    params = dict(zip(_PARAM_NAMES, (norm_mix, w_in, q_norm, k_norm, conv_w, conv_b, dt_bias, a_log, d_skip,
                                     ssm_norm, attn_norm, w_out, norm_ffn, w_router, b_router, w_gate, b_gate,
                                     w_up, b_up, w_down, b_down)))
    b, l, _ = x.shape
    tabs = _rope_tables(l)
    tl = _tiles(l, b * l)
    for i in range(norm_mix.shape[0]):
        x = _layer(x, params, i, tabs, tl)
    return x
```

```python
import functools
import math

import jax
import jax.numpy as jnp
from jax import lax
from jax.experimental import pallas as pl
from jax.experimental.pallas import tpu as pltpu

F32 = jnp.float32
BF16 = jnp.bfloat16

D_MODEL = 1024
GRID_W = 64
HEAD_DIM = 64
N_HEADS = 8
N_KV = 2
HEADS_PER_KV = N_HEADS // N_KV
ATTN_WIDTH = N_HEADS * HEAD_DIM
KV_WIDTH = N_KV * HEAD_DIM
SSM_WIDTH = 512
SSM_HEADS = 8
SSM_HEAD_DIM = 64
SSM_GROUPS = 2
SSM_HEADS_PER_GROUP = SSM_HEADS // SSM_GROUPS
SSM_STATE = 128
CHUNK = 128
CONV_K = 5
XBC_WIDTH = SSM_WIDTH + 2 * SSM_GROUPS * SSM_STATE
ROPE_THETA = 10000.0
ROPE_AXIS_DIM = HEAD_DIM // 2
N_EXPERTS = 32
TOP_K = 4
TOP_K_BITS = TOP_K.bit_length() - 1
SWIGLU_LIMIT = 7.0
SWIGLU_ALPHA = 1.702
EPS = 1e-6

LANES = 128
SUBLANES = 8
VMEM_LIMIT = 56 * 1024 * 1024

Q_PAD = N_HEADS * LANES


def _cparams(sem):
    return pltpu.CompilerParams(dimension_semantics=sem, vmem_limit_bytes=VMEM_LIMIT)


def _dot(a, b):
    return jnp.dot(a, b, preferred_element_type=F32)


def _dot_nt(a, b):
    return lax.dot_general(a, b, (((1,), (1,)), ((), ())), preferred_element_type=F32)


def _split3(x):
    hi = x.astype(BF16)
    r1 = x - hi.astype(F32)
    mid = r1.astype(BF16)
    lo = (r1 - mid.astype(F32)).astype(BF16)
    return hi, mid, lo


def _sigmoid(x):
    return 1.0 / (1.0 + jnp.exp(-x))


def _softplus(x):
    return jnp.maximum(x, 0.0) + jnp.log(1.0 + jnp.exp(-jnp.abs(x)))


def _inproj_kernel(x_ref, g_ref, wqT_ref, wk_ref, wvT_ref, wz_ref, wxbc_ref, wdt_ref, wdtT_ref,
                   qg_ref, kg_ref, cos_ref, s1_ref, s2_ref, cosT_ref, sinT_ref,
                   q_ref, k_ref, v_ref, z_ref, xbc_ref, dt_ref, dtT_ref):
    x = x_ref[0]
    tm = x.shape[0]
    hn = (x * lax.rsqrt(jnp.mean(x * x, axis=-1, keepdims=True) + EPS) * g_ref[...]).astype(BF16)

    cos_t = cosT_ref[...]
    sin_t = sinT_ref[...]
    q_scale = math.log2(math.e) / math.sqrt(HEAD_DIM)
    acc_qt = _dot_nt(wqT_ref[...], hn)
    half = ROPE_AXIS_DIM // 2
    for h in range(N_HEADS):
        a = acc_qt[HEAD_DIM * h:HEAD_DIM * (h + 1)]
        r = lax.rsqrt(jnp.sum(a * a, axis=0, keepdims=True) * (1.0 / HEAD_DIM) + EPS)
        y = a * r * qg_ref[...]
        swapped = jnp.concatenate([y[half:2 * half], y[0:half], y[3 * half:4 * half], y[2 * half:3 * half]], axis=0)
        y = (y * cos_t + swapped * sin_t) * q_scale
        q_ref[0, h, 0:HEAD_DIM, :] = y.astype(BF16)
        q_ref[0, h, HEAD_DIM:, :] = jnp.zeros((LANES - HEAD_DIM, tm), BF16)

    cos = cos_ref[...]
    s1 = s1_ref[...]
    s2 = s2_ref[...]
    acc_k = _dot(hn, wk_ref[...])
    for g in range(N_KV):
        a = acc_k[:, LANES * g:LANES * (g + 1)]
        r = lax.rsqrt(jnp.sum(a * a, axis=-1, keepdims=True) * (1.0 / HEAD_DIM) + EPS)
        y = a * r * kg_ref[...]
        y = y * cos + pltpu.roll(y, LANES - 16, 1) * s1 + pltpu.roll(y, 16, 1) * s2
        k_ref[0, g] = y.astype(BF16)

    acc_vt = _dot_nt(wvT_ref[...], hn)
    ones_row = jnp.where(lax.broadcasted_iota(jnp.int32, (LANES - HEAD_DIM, tm), 0) == 0, 1.0, 0.0).astype(BF16)
    for g in range(N_KV):
        v_ref[0, g, 0:HEAD_DIM, :] = acc_vt[HEAD_DIM * g:HEAD_DIM * (g + 1)].astype(BF16)
        v_ref[0, g, HEAD_DIM:, :] = ones_row

    z_ref[0] = _dot(hn, wz_ref[...])
    xbc_ref[0] = _dot(hn, wxbc_ref[...])
    dt_ref[0] = _dot(hn, wdt_ref[...])
    dtT_ref[0] = _dot_nt(wdtT_ref[...], hn)


def _inproj(x, gain, wqT, wk, wvT, wz, wxbc, wdt, wdtT, qg, kg, tabs, tm):
    b, l, d = x.shape
    nt = l // tm
    const = lambda shape: pl.BlockSpec(shape, lambda bi, i: (0,) * len(shape))
    tab = pl.BlockSpec((tm, LANES), lambda bi, i: (i, 0))
    tab_t = pl.BlockSpec((HEAD_DIM, tm), lambda bi, i: (0, i))
    return pl.pallas_call(
        _inproj_kernel,
        grid=(b, nt),
        in_specs=[
            pl.BlockSpec((1, tm, d), lambda bi, i: (bi, i, 0)),
            const((1, d)), const(wqT.shape), const(wk.shape), const(wvT.shape), const(wz.shape),
            const(wxbc.shape), const(wdt.shape), const(wdtT.shape), const((HEAD_DIM, 1)), const((1, LANES)),
            tab, tab, tab, tab_t, tab_t,
        ],
        out_specs=[
            pl.BlockSpec((1, N_HEADS, LANES, tm), lambda bi, i: (bi, 0, 0, i)),
            pl.BlockSpec((1, N_KV, tm, LANES), lambda bi, i: (bi, 0, i, 0)),
            pl.BlockSpec((1, N_KV, LANES, tm), lambda bi, i: (bi, 0, 0, i)),
            pl.BlockSpec((1, tm, SSM_WIDTH), lambda bi, i: (bi, i, 0)),
            pl.BlockSpec((1, tm, XBC_WIDTH), lambda bi, i: (bi, i, 0)),
            pl.BlockSpec((1, tm, 2 * SSM_HEADS), lambda bi, i: (bi, i, 0)),
            pl.BlockSpec((1, 2 * SSM_HEADS, tm), lambda bi, i: (bi, 0, i)),
        ],
        out_shape=[
            jax.ShapeDtypeStruct((b, N_HEADS, LANES, l), BF16),
            jax.ShapeDtypeStruct((b, N_KV, l, LANES), BF16),
            jax.ShapeDtypeStruct((b, N_KV, LANES, l), BF16),
            jax.ShapeDtypeStruct((b, l, SSM_WIDTH), F32),
            jax.ShapeDtypeStruct((b, l, XBC_WIDTH), F32),
            jax.ShapeDtypeStruct((b, l, 2 * SSM_HEADS), F32),
            jax.ShapeDtypeStruct((b, 2 * SSM_HEADS, l), F32),
        ],
        compiler_params=_cparams(("parallel", "parallel")),
        name="inproj",
    )(x, gain, wqT, wk, wvT, wz, wxbc, wdt, wdtT, qg, kg, *tabs)


def _attn_kernel(q_ref, k_ref, v_ref, o_ref, qcat_ref, m_ref, acc_ref, sa_ref, sb_ref, *, tk):
    tq = q_ref.shape[3]
    seq = k_ref.shape[2]
    cols = HEADS_PER_KV * tq
    n_chunks = seq // tk
    for hh in range(HEADS_PER_KV):
        qcat_ref[:, hh * tq:(hh + 1) * tq] = q_ref[0, hh]
    m_ref[...] = jnp.full((1, cols), -jnp.inf, F32)
    acc_ref[...] = jnp.zeros((LANES, cols), F32)

    def scores(j, s_ref):
        off = pl.multiple_of(j * tk, tk)
        s_ref[...] = _dot(k_ref[0, 0, pl.ds(off, tk), :], qcat_ref[...])

    def accumulate(j, s_ref):
        off = pl.multiple_of(j * tk, tk)
        vc = v_ref[0, 0, :, pl.ds(off, tk)]
        s = s_ref[...]
        m_old = m_ref[...]
        m_new = jnp.maximum(m_old, jnp.max(s, axis=0, keepdims=True))
        p = jnp.exp2(s - m_new).astype(BF16)
        acc_ref[...] = jnp.exp2(m_old - m_new) * acc_ref[...] + _dot(vc, p)
        m_ref[...] = m_new

    scores(0, sa_ref)

    def body(i, carry):
        j = 2 * i
        scores(j + 1, sb_ref)
        accumulate(j, sa_ref)
        scores(jnp.minimum(j + 2, n_chunks - 1), sa_ref)
        accumulate(j + 1, sb_ref)
        return carry

    lax.fori_loop(0, n_chunks // 2, body, 0)
    acc = acc_ref[...]
    out = acc[0:HEAD_DIM] / acc[HEAD_DIM:HEAD_DIM + 1]
    out = jnp.concatenate([out, jnp.zeros((LANES - HEAD_DIM, cols), F32)], axis=0)
    for hh in range(HEADS_PER_KV):
        o_ref[0, hh] = out[:, hh * tq:(hh + 1) * tq].T


def _attention(q, k, v, tq, tk):
    b, _, _, l = q.shape
    cols = HEADS_PER_KV * tq
    return pl.pallas_call(
        functools.partial(_attn_kernel, tk=tk),
        grid=(b, N_KV, l // tq),
        in_specs=[
            pl.BlockSpec((1, HEADS_PER_KV, LANES, tq), lambda bi, g, i: (bi, g, 0, i)),
            pl.BlockSpec((1, 1, l, LANES), lambda bi, g, i: (bi, g, 0, 0)),
            pl.BlockSpec((1, 1, LANES, l), lambda bi, g, i: (bi, g, 0, 0)),
        ],
        out_specs=pl.BlockSpec((1, HEADS_PER_KV, tq, LANES), lambda bi, g, i: (bi, g, i, 0)),
        out_shape=jax.ShapeDtypeStruct((b, N_HEADS, l, LANES), F32),
        scratch_shapes=[
            pltpu.VMEM((LANES, cols), BF16),
            pltpu.VMEM((1, cols), F32),
            pltpu.VMEM((LANES, cols), F32),
            pltpu.VMEM((tk, cols), F32),
            pltpu.VMEM((tk, cols), F32),
        ],
        compiler_params=_cparams(("parallel", "parallel", "parallel")),
        name="attn",
    )(q, k, v)


def _ssd_kernel(*refs, reverse):
    if reverse:
        (cur_ref, prev_ref, next_ref, dt_ref, dtT_ref, convw_ref, convb_ref, dtb_row_ref, dtb_col_ref,
         alog_row_ref, alog_col_ref, yin_ref, z_ref, dskip_ref, gain_ref,
         y_ref, pad_ref, state_ref) = refs
    else:
        (cur_ref, prev_ref, next_ref, dt_ref, dtT_ref, convw_ref, convb_ref, dtb_row_ref, dtb_col_ref,
         alog_row_ref, alog_col_ref, y_ref, pad_ref, state_ref) = refs
    c = pl.program_id(1)
    nc = pl.num_programs(1)
    chunk = (nc - 1 - c) if reverse else c

    @pl.when(c == 0)
    def _():
        state_ref[...] = jnp.zeros(state_ref.shape, F32)

    halo = SUBLANES
    pad_ref[0:halo, :] = jnp.where(chunk > 0, prev_ref[0], 0.0)
    pad_ref[halo:halo + CHUNK, :] = cur_ref[0]
    pad_ref[halo + CHUNK:, :] = jnp.where(chunk < nc - 1, next_ref[0], 0.0)
    conv = jnp.broadcast_to(convb_ref[...], (CHUNK, XBC_WIDTH))
    for j in range(CONV_K):
        conv = conv + convw_ref[j:j + 1, :] * pad_ref[pl.ds(halo - CONV_K // 2 + j, CHUNK), :]
    xc = conv * _sigmoid(conv)
    xs = xc[:, :SSM_WIDTH]

    dt_col = _softplus(dt_ref[0, 0] + dtb_row_ref[0])
    dt_row = _softplus(dtT_ref[0, 0] + dtb_col_ref[0])
    da_col = dt_col * -jnp.exp(alog_row_ref[0])
    da_row = dt_row * -jnp.exp(alog_col_ref[0])

    ri = lax.broadcasted_iota(jnp.int32, (CHUNK, CHUNK), 0)
    ci = lax.broadcasted_iota(jnp.int32, (CHUNK, CHUNK), 1)
    low = ci <= ri
    up = ci >= ri
    low_m = jnp.where(low, 1.0, 0.0).astype(BF16)
    up_m = jnp.where(up, 1.0, 0.0).astype(BF16)
    col_m, row_m, mask = (up_m, low_m, up) if reverse else (low_m, up_m, low)
    ch, cm, cl = _split3(da_col)
    acum_col = _dot(col_m, ch) + _dot(col_m, cm) + _dot(col_m, cl)
    rh, rm, rl = _split3(da_row)
    acum_row = _dot(rh, row_m) + _dot(rm, row_m) + _dot(rl, row_m)
    end = 0 if reverse else CHUNK - 1
    a_end_row = acum_col[end:end + 1, :]
    w_col = dt_col * jnp.exp(a_end_row - acum_col)
    e_col = jnp.exp(acum_col)
    chunk_decay = jnp.exp(a_end_row)

    pieces = []
    for g in range(SSM_GROUPS):
        bm = xc[:, SSM_WIDTH + g * SSM_STATE:SSM_WIDTH + (g + 1) * SSM_STATE]
        cmat = xc[:, SSM_WIDTH + (SSM_GROUPS + g) * SSM_STATE:SSM_WIDTH + (SSM_GROUPS + g + 1) * SSM_STATE]
        bm16 = bm.astype(BF16)
        cm16 = cmat.astype(BF16)
        cb = _dot_nt(cm16, bm16)
        bt16 = bm.T.astype(BF16)
        for hh in range(SSM_HEADS_PER_GROUP):
            h = g * SSM_HEADS_PER_GROUP + hh
            xh = xs[:, h * SSM_HEAD_DIM:(h + 1) * SSM_HEAD_DIM]
            seg = acum_col[:, h:h + 1] - acum_row[h:h + 1, :]
            decay = jnp.exp(jnp.where(mask, seg, -jnp.inf))
            mat = cb * decay * dt_row[h:h + 1, :]
            y_diag = _dot(mat.astype(BF16), xh.astype(BF16))
            prev_state = state_ref[h]
            y_off = _dot(cm16, prev_state.astype(BF16)) * e_col[:, h:h + 1]
            pieces.append(y_diag + y_off)
            xw = xh * w_col[:, h:h + 1]
            state_ref[h] = prev_state * chunk_decay[:, h:h + 1] + _dot(bt16, xw.astype(BF16))
    y = jnp.concatenate(pieces, axis=1)

    if not reverse:
        y_ref[0] = y
    else:
        y = y + yin_ref[0] + xs * dskip_ref[...]
        zz = z_ref[0]
        y = y * (zz * _sigmoid(zz))
        gw = SSM_WIDTH // SSM_GROUPS
        outs = []
        for g in range(SSM_GROUPS):
            yg = y[:, g * gw:(g + 1) * gw]
            outs.append(yg * lax.rsqrt(jnp.mean(yg * yg, axis=-1, keepdims=True) + EPS))
        y_ref[0] = jnp.concatenate(outs, axis=1) * gain_ref[...]


def _ssd(xbc, dt, dtT, convw, convb, dtb_row, dtb_col, alog_row, alog_col, direction,
         y_fwd=None, z=None, dskip=None, gain=None):
    b, l, _ = xbc.shape
    nc = l // CHUNK
    reverse = direction == 1
    blocks_per_chunk = CHUNK // SUBLANES
    nblk = l // SUBLANES

    def pos(c):
        return (nc - 1 - c) if reverse else c

    const = lambda shape: pl.BlockSpec(shape, lambda bi, c: (0,) * len(shape))
    dsel = lambda shape: pl.BlockSpec(shape, lambda bi, c: (direction,) + (0,) * (len(shape) - 1))
    in_specs = [
        pl.BlockSpec((1, CHUNK, XBC_WIDTH), lambda bi, c: (bi, pos(c), 0)),
        pl.BlockSpec((1, SUBLANES, XBC_WIDTH),
                     lambda bi, c: (bi, jnp.maximum(pos(c) * blocks_per_chunk - 1, 0), 0)),
        pl.BlockSpec((1, SUBLANES, XBC_WIDTH),
                     lambda bi, c: (bi, jnp.minimum((pos(c) + 1) * blocks_per_chunk, nblk - 1), 0)),
        pl.BlockSpec((1, 1, CHUNK, SSM_HEADS), lambda bi, c: (direction, bi, pos(c), 0)),
        pl.BlockSpec((1, 1, SSM_HEADS, CHUNK), lambda bi, c: (direction, bi, 0, pos(c))),
        const((CONV_K, XBC_WIDTH)), const((1, XBC_WIDTH)),
        dsel((1, 1, SSM_HEADS)), dsel((1, SSM_HEADS, 1)), dsel((1, 1, SSM_HEADS)), dsel((1, SSM_HEADS, 1)),
    ]
    args = [xbc, xbc, xbc, dt, dtT, convw, convb, dtb_row, dtb_col, alog_row, alog_col]
    if reverse:
        in_specs += [
            pl.BlockSpec((1, CHUNK, SSM_WIDTH), lambda bi, c: (bi, pos(c), 0)),
            pl.BlockSpec((1, CHUNK, SSM_WIDTH), lambda bi, c: (bi, pos(c), 0)),
            const((1, SSM_WIDTH)), const((1, SSM_WIDTH)),
        ]
        args += [y_fwd, z, dskip, gain]
    return pl.pallas_call(
        functools.partial(_ssd_kernel, reverse=reverse),
        grid=(b, nc),
        in_specs=in_specs,
        out_specs=pl.BlockSpec((1, CHUNK, SSM_WIDTH), lambda bi, c: (bi, pos(c), 0)),
        out_shape=jax.ShapeDtypeStruct((b, l, SSM_WIDTH), F32),
        scratch_shapes=[
            pltpu.VMEM((CHUNK + 2 * SUBLANES, XBC_WIDTH), F32),
            pltpu.VMEM((SSM_HEADS, SSM_STATE, SSM_HEAD_DIM), F32),
        ],
        compiler_params=_cparams(("parallel", "arbitrary")),
        name="ssd_bwd" if reverse else "ssd_fwd",
    )(*args)


def _outproj_kernel(attn_ref, ssm_ref, x_ref, again_ref, wa_ref, ws_ref, fgain_ref, wr_hi_ref, wr_lo_ref,
                    x1_ref, hn_ref, logit_ref):
    a = jnp.concatenate([attn_ref[0, h] for h in range(N_HEADS)], axis=1)
    r = lax.rsqrt(jnp.sum(a * a, axis=-1, keepdims=True) * (1.0 / ATTN_WIDTH) + EPS)
    an = (a * r * again_ref[...]).astype(BF16)
    x1 = x_ref[...] + _dot(an, wa_ref[...]) + _dot(ssm_ref[...].astype(BF16), ws_ref[...])
    x1_ref[...] = x1
    hn = x1 * lax.rsqrt(jnp.mean(x1 * x1, axis=-1, keepdims=True) + EPS) * fgain_ref[...]
    hn_ref[...] = hn
    h_hi = hn.astype(BF16)
    h_lo = (hn - h_hi.astype(F32)).astype(BF16)
    w_hi = wr_hi_ref[...]
    logit_ref[...] = _dot_nt(w_hi, h_hi) + _dot_nt(w_hi, h_lo) + _dot_nt(wr_lo_ref[...], h_hi)


def _outproj(attn, ssm, x, again, wa, ws, fgain, wr_hi, wr_lo, tm):
    b, _, l, _ = attn.shape
    t = b * l
    nt = l // tm
    const = lambda shape: pl.BlockSpec(shape, lambda i: (0,) * len(shape))
    return pl.pallas_call(
        _outproj_kernel,
        grid=(t // tm,),
        in_specs=[
            pl.BlockSpec((1, N_HEADS, tm, LANES), lambda i: (i // nt, 0, i % nt, 0)),
            pl.BlockSpec((tm, SSM_WIDTH), lambda i: (i, 0)),
            pl.BlockSpec((tm, D_MODEL), lambda i: (i, 0)),
            const((1, Q_PAD)), const(wa.shape), const(ws.shape), const((1, D_MODEL)),
            const(wr_hi.shape), const(wr_lo.shape),
        ],
        out_specs=[
            pl.BlockSpec((tm, D_MODEL), lambda i: (i, 0)),
            pl.BlockSpec((tm, D_MODEL), lambda i: (i, 0)),
            pl.BlockSpec((N_EXPERTS, tm), lambda i: (0, i)),
        ],
        out_shape=[
            jax.ShapeDtypeStruct((t, D_MODEL), F32),
            jax.ShapeDtypeStruct((t, D_MODEL), F32),
            jax.ShapeDtypeStruct((N_EXPERTS, t), F32),
        ],
        compiler_params=_cparams(("parallel",)),
        name="outproj",
    )(attn, ssm, x, again, wa, ws, fgain, wr_hi, wr_lo)


def _route_kernel(logit_ref, bias_ref, idx_ref, w_ref, rank_ref, count_ref, carry_ref):
    i = pl.program_id(0)
    tn = logit_ref.shape[1]

    @pl.when(i == 0)
    def _():
        carry_ref[...] = jnp.zeros(carry_ref.shape, F32)

    lg = logit_ref[...] + bias_ref[...]
    eid = lax.broadcasted_iota(jnp.int32, (N_EXPERTS, tn), 0).astype(F32)
    chosen = jnp.zeros((N_EXPERTS, tn), F32)
    vals, ids = [], []
    for _ in range(TOP_K):
        mx = jnp.max(lg, axis=0, keepdims=True)
        sel = jnp.min(jnp.where(lg == mx, eid, float(N_EXPERTS)), axis=0, keepdims=True)
        hit = eid == sel
        chosen = jnp.where(hit, 1.0, chosen)
        lg = jnp.where(hit, -jnp.inf, lg)
        vals.append(mx)
        ids.append(sel)
    ex = [jnp.exp(v - vals[0]) for v in vals]
    den = ex[0] + ex[1] + ex[2] + ex[3]
    ri = lax.broadcasted_iota(jnp.int32, (tn, tn), 0)
    ci = lax.broadcasted_iota(jnp.int32, (tn, tn), 1)
    upper = jnp.where(ri <= ci, 1.0, 0.0).astype(BF16)
    incl = _dot(chosen.astype(BF16), upper)
    rank_e = incl - chosen + carry_ref[...]
    carry_ref[...] = carry_ref[...] + incl[:, tn - 1:tn]
    for k in range(TOP_K):
        idx_ref[k:k + 1, :] = ids[k].astype(jnp.int32)
        w_ref[k:k + 1, :] = ex[k] / den
        rank_ref[k:k + 1, :] = jnp.sum(jnp.where(eid == ids[k], rank_e, 0.0), axis=0,
                                       keepdims=True).astype(jnp.int32)
    count_ref[...] = carry_ref[...].astype(jnp.int32)


def _route(logits_t, bias_col, tn):
    t = logits_t.shape[1]
    tok = pl.BlockSpec((TOP_K, tn), lambda i: (0, i))
    return pl.pallas_call(
        _route_kernel,
        grid=(t // tn,),
        in_specs=[pl.BlockSpec((N_EXPERTS, tn), lambda i: (0, i)),
                  pl.BlockSpec((N_EXPERTS, 1), lambda i: (0, 0))],
        out_specs=[tok, tok, tok, pl.BlockSpec((N_EXPERTS, 1), lambda i: (0, 0))],
        out_shape=[
            jax.ShapeDtypeStruct((TOP_K, t), jnp.int32),
            jax.ShapeDtypeStruct((TOP_K, t), F32),
            jax.ShapeDtypeStruct((TOP_K, t), jnp.int32),
            jax.ShapeDtypeStruct((N_EXPERTS, 1), jnp.int32),
        ],
        scratch_shapes=[pltpu.VMEM((N_EXPERTS, 1), F32)],
        compiler_params=_cparams(("arbitrary",)),
        name="route",
    )(logits_t, bias_col)


def _dest_kernel(idx_ref, rank_ref, start_ref, dest_ref):
    tn = idx_ref.shape[1]
    eid = lax.broadcasted_iota(jnp.int32, (N_EXPERTS, tn), 0)
    start = start_ref[...].astype(F32)
    for k in range(TOP_K):
        base = jnp.sum(jnp.where(eid == idx_ref[k:k + 1, :], start, 0.0), axis=0, keepdims=True)
        dest_ref[k:k + 1, :] = base.astype(jnp.int32) + rank_ref[k:k + 1, :]


def _dest(idx_t, rank_t, start_col, tn):
    t = idx_t.shape[1]
    tok = pl.BlockSpec((TOP_K, tn), lambda i: (0, i))
    return pl.pallas_call(
        _dest_kernel,
        grid=(t // tn,),
        in_specs=[tok, tok, pl.BlockSpec((N_EXPERTS, 1), lambda i: (0, 0))],
        out_specs=tok,
        out_shape=jax.ShapeDtypeStruct((TOP_K, t), jnp.int32),
        compiler_params=_cparams(("parallel",)),
        name="dest",
    )(idx_t, rank_t, start_col)


def _rowmap_kernel(lo_ref, hi_ref, dest_ref, code_ref, *, n_tok, bm):
    i = pl.program_id(0)
    tm = dest_ref.shape[1]
    spare = TOP_K * n_tok

    @pl.when(i == 0)
    def _():
        def lead(r, carry):
            code_ref[r] = spare + r
            return carry

        lax.fori_loop(0, bm, lead, 0)

        def segment(e, carry):
            def fill(r, c):
                code_ref[bm + r] = spare + (r & (bm - 1))
                return c

            lax.fori_loop(lo_ref[e], hi_ref[e], fill, 0)
            return carry

        lax.fori_loop(0, N_EXPERTS + 1, segment, 0)

    def token(t, carry):
        for k in range(TOP_K):
            code_ref[bm + dest_ref[k, t]] = TOP_K * (i * tm + t) + k
        return carry

    lax.fori_loop(0, tm, token, 0)


def _rowmap(seg_lo, seg_hi, dest, n_rows, bm, tm):
    t = dest.shape[1]
    return pl.pallas_call(
        functools.partial(_rowmap_kernel, n_tok=t, bm=bm),
        grid_spec=pltpu.PrefetchScalarGridSpec(
            num_scalar_prefetch=2,
            grid=(t // tm,),
            in_specs=[pl.BlockSpec((TOP_K, tm), lambda i, lo, hi: (0, i), memory_space=pltpu.SMEM)],
            out_specs=pl.BlockSpec(memory_space=pltpu.SMEM),
        ),
        out_shape=jax.ShapeDtypeStruct((n_rows + bm,), jnp.int32),
        compiler_params=_cparams(("arbitrary",)),
        name="rowmap",
    )(seg_lo, seg_hi, dest)


def _expert_kernel(block_e_ref, rows_prev_ref, rows_cur_ref, rows_next_ref, hn_ref,
                   wg_ref, bg_ref, wu_ref, bu_ref, wd_ref, bd_ref, ys_ref,
                   wg16, wu16, wd16, xa, xb, ya, yb, sems, *, n_tok, plane):
    b = pl.program_id(0)
    nb = pl.num_programs(0)
    bm = xa.shape[0]

    def gather_copy(rows_ref, r, xbuf, sem):
        tok = jnp.minimum(rows_ref[0, 0, r] >> TOP_K_BITS, n_tok - 1)
        return pltpu.make_async_copy(hn_ref.at[pl.ds(tok, 1), :], xbuf.at[pl.ds(r, 1), :], sem)

    def scatter_copy(rows_ref, r, ybuf, sem):
        code = rows_ref[0, 0, r]
        row = (code & (TOP_K - 1)) * plane + (code >> TOP_K_BITS)
        return pltpu.make_async_copy(ybuf.at[pl.ds(r, 1), :], ys_ref.at[pl.ds(row, 1), :], sem)

    def wait_block(buf, sem):
        pltpu.make_async_copy(hn_ref.at[pl.ds(0, bm), :], buf, sem).wait()

    @pl.when(b == 0)
    def _():
        def first(r, carry):
            gather_copy(rows_cur_ref, r, xa, sems.at[0]).start()
            return carry

        lax.fori_loop(0, bm, first, 0)
        yb[...] = jnp.zeros(yb.shape, F32)
        for k in range(TOP_K):
            spare = pltpu.make_async_copy(yb, ys_ref.at[pl.ds(k * plane + n_tok, bm), :], sems.at[3])
            spare.start()
            spare.wait()

    prev_e = block_e_ref[jnp.maximum(b - 1, 0)]

    @pl.when(jnp.logical_or(b == 0, block_e_ref[b] != prev_e))
    def _():
        wg16[...] = wg_ref[0].astype(BF16)
        wu16[...] = wu_ref[0].astype(BF16)
        wd16[...] = wd_ref[0].astype(BF16)

    def step(x_cur, y_cur, x_next, y_prev, g_cur, g_next, s_cur, s_prev):
        wait_block(x_cur, g_cur)

        @pl.when(b > 0)
        def _():
            wait_block(y_cur, s_cur)

        for r in range(bm):
            gather_copy(rows_next_ref, r, x_next, g_next).start()
            scatter_copy(rows_prev_ref, r, y_prev, s_prev).start()
        x = x_cur[...].astype(BF16)
        g = _dot(x, wg16[...]) + bg_ref[0]
        u = _dot(x, wu16[...]) + bu_ref[0]
        g = jnp.minimum(g, SWIGLU_LIMIT)
        u = jnp.clip(u, -SWIGLU_LIMIT, SWIGLU_LIMIT)
        act = g * _sigmoid(SWIGLU_ALPHA * g) * (u + 1.0)
        y_cur[...] = _dot(act.astype(BF16), wd16[...]) + bd_ref[0]

        @pl.when(b == nb - 1)
        def _():
            def last(r, carry):
                scatter_copy(rows_cur_ref, r, y_cur, s_cur).start()
                return carry

            wait_block(y_prev, s_prev)
            lax.fori_loop(0, bm, last, 0)
            wait_block(y_cur, s_cur)
            wait_block(x_next, g_next)

    @pl.when(b % 2 == 0)
    def _():
        step(xa, ya, xb, yb, sems.at[0], sems.at[1], sems.at[2], sems.at[3])

    @pl.when(b % 2 == 1)
    def _():
        step(xb, yb, xa, ya, sems.at[1], sems.at[0], sems.at[3], sems.at[2])


def _experts(block_e, codes, hn, wg, bg, wu, bu, wd, bd, bm, layer):
    n_tok, d = hn.shape
    nb = codes.shape[0] - 1
    plane = n_tok + bm
    base = layer * N_EXPERTS
    wspec = pl.BlockSpec((1, d, d), lambda i, be: (base + be[i], 0, 0))
    bspec = pl.BlockSpec((1, 1, d), lambda i, be: (base + be[i], 0, 0))
    rows = lambda shift: pl.BlockSpec((1, 1, bm), lambda i, be: (jnp.minimum(i + shift, nb), 0, 0),
                                      memory_space=pltpu.SMEM)
    return pl.pallas_call(
        functools.partial(_expert_kernel, n_tok=n_tok, plane=plane),
        grid_spec=pltpu.PrefetchScalarGridSpec(
            num_scalar_prefetch=1,
            grid=(nb,),
            in_specs=[rows(0), rows(1), rows(2), pl.BlockSpec(memory_space=pl.ANY),
                      wspec, bspec, wspec, bspec, wspec, bspec],
            out_specs=pl.BlockSpec(memory_space=pl.ANY),
            scratch_shapes=[pltpu.VMEM((d, d), BF16)] * 3 + [pltpu.VMEM((bm, d), F32)] * 4
            + [pltpu.SemaphoreType.DMA((4,))],
        ),
        out_shape=jax.ShapeDtypeStruct((TOP_K * plane, d), F32),
        compiler_params=_cparams(("arbitrary",)),
        name="experts",
    )(block_e, codes, codes, codes, hn, wg, bg, wu, bu, wd, bd)


def _combine_kernel(w_ref, x1_ref, ys_ref, out_ref):
    w = w_ref[...]
    acc = x1_ref[...]
    for k in range(TOP_K):
        acc = acc + w[:, k:k + 1] * ys_ref[k]
    out_ref[...] = acc


def _combine(w_tok, x1, ys, tm):
    t, d = x1.shape
    return pl.pallas_call(
        _combine_kernel,
        grid=(t // tm,),
        in_specs=[
            pl.BlockSpec((tm, TOP_K), lambda i: (i, 0)),
            pl.BlockSpec((tm, d), lambda i: (i, 0)),
            pl.BlockSpec((TOP_K, tm, d), lambda i: (0, i, 0)),
        ],
        out_specs=pl.BlockSpec((tm, d), lambda i: (i, 0)),
        out_shape=jax.ShapeDtypeStruct((t, d), F32),
        compiler_params=_cparams(("parallel",)),
        name="combine",
    )(w_tok, x1, ys)


def _pad_heads(w, n_heads):
    d = w.shape[0]
    w = w.reshape(d, n_heads, HEAD_DIM)
    return jnp.pad(w, ((0, 0), (0, 0), (0, LANES - HEAD_DIM))).reshape(d, n_heads * LANES)


def _rope_tables(seq_len):
    rows = seq_len // GRID_W
    inv_freq = ROPE_THETA ** (-jnp.arange(0, ROPE_AXIS_DIM, 2, dtype=F32) / ROPE_AXIS_DIM)
    ang_r = jnp.arange(rows, dtype=F32)[:, None] * inv_freq
    ang_c = jnp.arange(GRID_W, dtype=F32)[:, None] * inv_freq
    expand_r = lambda a: jnp.repeat(a, GRID_W, axis=0)
    expand_c = lambda a: jnp.tile(a, (rows, 1))
    cr, sr = expand_r(jnp.cos(ang_r)), expand_r(jnp.sin(ang_r))
    cc, sc = expand_c(jnp.cos(ang_c)), expand_c(jnp.sin(ang_c))
    zeros = jnp.zeros_like(cr)
    pad = jnp.zeros((seq_len, LANES - HEAD_DIM), F32)
    cos = jnp.concatenate([cr, cr, cc, cc, pad], axis=1)
    s_up = jnp.concatenate([-sr, zeros, -sc, zeros, pad], axis=1)
    s_dn = jnp.concatenate([zeros, sr, zeros, sc, pad], axis=1)
    cos_t = jnp.concatenate([cr, cr, cc, cc], axis=1).T
    sin_t = jnp.concatenate([-sr, sr, -sc, sc], axis=1).T
    return cos, s_up, s_dn, cos_t, sin_t


def _tiles(seq_len, n_tok):
    return dict(
        tm_in=min(512, seq_len),
        tq=min(256, seq_len),
        tk=min(256, seq_len),
        tm_out=min(512, seq_len),
        tn_route=min(1024, n_tok),
        tm_rows=min(256, n_tok),
        bm=256,
    )


def _layer(x, p, i, tabs, tl):
    b, l, d = x.shape
    t = b * l
    w_in = p["w_in"][i]
    o_q, o_k, o_v, o_z, o_xbc = ATTN_WIDTH, ATTN_WIDTH + KV_WIDTH, ATTN_WIDTH + 2 * KV_WIDTH, \
        ATTN_WIDTH + 2 * KV_WIDTH + SSM_WIDTH, ATTN_WIDTH + 2 * KV_WIDTH + SSM_WIDTH + XBC_WIDTH
    wq_t = w_in[:, :o_q].T.astype(BF16)
    wk = _pad_heads(w_in[:, o_q:o_k], N_KV).astype(BF16)
    wv_t = w_in[:, o_k:o_v].T.astype(BF16)
    wz = w_in[:, o_v:o_z].astype(BF16)
    wxbc = w_in[:, o_z:o_xbc].astype(BF16)
    wdt = w_in[:, o_xbc:].astype(BF16)
    q4, k4, v4, z, xbc, dt, dt_t = _inproj(
        x, p["norm_mix"][i].reshape(1, d), wq_t, wk, wv_t, wz, wxbc, wdt, wdt.T,
        p["q_norm"][i].reshape(HEAD_DIM, 1), jnp.pad(p["k_norm"][i], (0, LANES - HEAD_DIM)).reshape(1, LANES),
        tabs, tl["tm_in"])

    attn = _attention(q4, k4, v4, tl["tq"], tl["tk"])

    dt_dir = dt.reshape(b, l, 2, SSM_HEADS).transpose(2, 0, 1, 3)
    dtt_dir = dt_t.reshape(b, 2, SSM_HEADS, l).transpose(1, 0, 2, 3)
    ssd_common = (xbc, dt_dir, dtt_dir, p["conv_w"][i], p["conv_b"][i].reshape(1, XBC_WIDTH),
                  p["dt_bias"][i].reshape(2, 1, SSM_HEADS), p["dt_bias"][i].reshape(2, SSM_HEADS, 1),
                  p["a_log"][i].reshape(2, 1, SSM_HEADS), p["a_log"][i].reshape(2, SSM_HEADS, 1))
    y_fwd = _ssd(*ssd_common, 0)
    ssm = _ssd(*ssd_common, 1, y_fwd=y_fwd, z=z,
               dskip=jnp.repeat(p["d_skip"][i], SSM_HEAD_DIM).reshape(1, SSM_WIDTH),
               gain=p["ssm_norm"][i].reshape(1, SSM_WIDTH))

    w_out = p["w_out"][i]
    wa = jnp.pad(w_out[:ATTN_WIDTH].reshape(N_HEADS, HEAD_DIM, d),
                 ((0, 0), (0, LANES - HEAD_DIM), (0, 0))).reshape(Q_PAD, d).astype(BF16)
    ws = w_out[ATTN_WIDTH:].astype(BF16)
    again = jnp.pad(p["attn_norm"][i].reshape(N_HEADS, HEAD_DIM),
                    ((0, 0), (0, LANES - HEAD_DIM))).reshape(1, Q_PAD)
    wr_t = p["w_router"][i].T
    wr_hi = wr_t.astype(BF16)
    wr_lo = (wr_t - wr_hi.astype(F32)).astype(BF16)
    x1, hn, logits_t = _outproj(attn, ssm.reshape(t, SSM_WIDTH), x.reshape(t, d), again, wa, ws,
                                p["norm_ffn"][i].reshape(1, d), wr_hi, wr_lo, tl["tm_out"])

    idx_t, w_t, rank_t, counts = _route(logits_t, p["b_router"][i].reshape(N_EXPERTS, 1), tl["tn_route"])

    bm = tl["bm"]
    padded = (counts.reshape(N_EXPERTS) + bm - 1) // bm * bm
    pad_end = jnp.cumsum(padded)
    pad_start = pad_end - padded
    dest = _dest(idx_t, rank_t, pad_start.reshape(N_EXPERTS, 1).astype(jnp.int32), tl["tn_route"])
    n_rows = t * TOP_K + N_EXPERTS * bm
    nb = n_rows // bm
    block_row = jnp.arange(nb, dtype=jnp.int32) * bm
    block_e = jnp.minimum(jnp.sum(pad_end[None, :] <= block_row[:, None], axis=1), N_EXPERTS - 1).astype(jnp.int32)
    tail = jnp.full((1,), n_rows, jnp.int32)
    seg_lo = jnp.concatenate([pad_start + counts.reshape(N_EXPERTS), pad_end[-1:]]).astype(jnp.int32)
    seg_hi = jnp.concatenate([pad_end, tail]).astype(jnp.int32)
    codes = _rowmap(seg_lo, seg_hi, dest, n_rows, bm, tl["tn_route"]).reshape(nb + 1, 1, bm)

    n_all = p["w_gate"].shape[0] * N_EXPERTS
    ys = _experts(block_e, codes, hn,
                  p["w_gate"].reshape(n_all, d, d), p["b_gate"].reshape(n_all, 1, d),
                  p["w_up"].reshape(n_all, d, d), p["b_up"].reshape(n_all, 1, d),
                  p["w_down"].reshape(n_all, d, d), p["b_down"].reshape(n_all, 1, d), bm, i)
    x2 = _combine(w_t.T, x1, ys.reshape(TOP_K, t + bm, d), tl["tm_rows"])
    return x2.reshape(b, l, d)


_PARAM_NAMES = ("norm_mix", "w_in", "q_norm", "k_norm", "conv_w", "conv_b", "dt_bias", "a_log", "d_skip",
                "ssm_norm", "attn_norm", "w_out", "norm_ffn", "w_router", "b_router", "w_gate", "b_gate",
                "w_up", "b_up", "w_down", "b_down")


def kernel(x, norm_mix, w_in, q_norm, k_norm, conv_w, conv_b, dt_bias, a_log, d_skip, ssm_norm, attn_norm,
           w_out, norm_ffn, w_router, b_router, w_gate, b_gate, w_up, b_up, w_down, b_down):
    params = dict(zip(_PARAM_NAMES, (norm_mix, w_in, q_norm, k_norm, conv_w, conv_b, dt_bias, a_log, d_skip,
                                     ssm_norm, attn_norm, w_out, norm_ffn, w_router, b_router, w_gate, b_gate,
                                     w_up, b_up, w_down, b_down)))
    b, l, _ = x.shape
    tabs = _rope_tables(l)
    tl = _tiles(l, b * l)
    for i in range(norm_mix.shape[0]):
        x = _layer(x, params, i, tabs, tl)
    return x
```

```python
import functools
import math

import jax
import jax.numpy as jnp
from jax import lax
from jax.experimental import pallas as pl
from jax.experimental.pallas import tpu as pltpu

F32 = jnp.float32
BF16 = jnp.bfloat16

D_MODEL = 1024
GRID_W = 64
HEAD_DIM = 64
N_HEADS = 8
N_KV = 2
HEADS_PER_KV = N_HEADS // N_KV
ATTN_WIDTH = N_HEADS * HEAD_DIM
KV_WIDTH = N_KV * HEAD_DIM
SSM_WIDTH = 512
SSM_HEADS = 8
SSM_HEAD_DIM = 64
SSM_GROUPS = 2
SSM_HEADS_PER_GROUP = SSM_HEADS // SSM_GROUPS
SSM_STATE = 128
CHUNK = 128
CONV_K = 5
XBC_WIDTH = SSM_WIDTH + 2 * SSM_GROUPS * SSM_STATE
ROPE_THETA = 10000.0
ROPE_AXIS_DIM = HEAD_DIM // 2
N_EXPERTS = 32
TOP_K = 4
TOP_K_BITS = TOP_K.bit_length() - 1
SWIGLU_LIMIT = 7.0
SWIGLU_ALPHA = 1.702
EPS = 1e-6

LANES = 128
SUBLANES = 8
VMEM_LIMIT = 56 * 1024 * 1024

Q_PAD = N_HEADS * LANES


def _cparams(sem):
    return pltpu.CompilerParams(dimension_semantics=sem, vmem_limit_bytes=VMEM_LIMIT)


def _dot(a, b):
    return jnp.dot(a, b, preferred_element_type=F32)


def _dot_nt(a, b):
    return lax.dot_general(a, b, (((1,), (1,)), ((), ())), preferred_element_type=F32)


def _split3(x):
    hi = x.astype(BF16)
    r1 = x - hi.astype(F32)
    mid = r1.astype(BF16)
    lo = (r1 - mid.astype(F32)).astype(BF16)
    return hi, mid, lo


def _sigmoid(x):
    return 1.0 / (1.0 + jnp.exp(-x))


def _softplus(x):
    return jnp.maximum(x, 0.0) + jnp.log(1.0 + jnp.exp(-jnp.abs(x)))


def _inproj_kernel(x_ref, g_ref, wqT_ref, wk_ref, wvT_ref, wz_ref, wxbc_ref, wdt_ref, wdtT_ref,
                   qg_ref, kg_ref, cos_ref, s1_ref, s2_ref, cosT_ref, sinT_ref,
                   q_ref, k_ref, v_ref, z_ref, xbc_ref, dt_ref, dtT_ref):
    x = x_ref[0]
    tm = x.shape[0]
    hn = (x * lax.rsqrt(jnp.mean(x * x, axis=-1, keepdims=True) + EPS) * g_ref[...]).astype(BF16)

    cos_t = cosT_ref[...]
    sin_t = sinT_ref[...]
    q_scale = math.log2(math.e) / math.sqrt(HEAD_DIM)
    acc_qt = _dot_nt(wqT_ref[...], hn)
    half = ROPE_AXIS_DIM // 2
    for h in range(N_HEADS):
        a = acc_qt[HEAD_DIM * h:HEAD_DIM * (h + 1)]
        r = lax.rsqrt(jnp.sum(a * a, axis=0, keepdims=True) * (1.0 / HEAD_DIM) + EPS)
        y = a * r * qg_ref[...]
        swapped = jnp.concatenate([y[half:2 * half], y[0:half], y[3 * half:4 * half], y[2 * half:3 * half]], axis=0)
        y = (y * cos_t + swapped * sin_t) * q_scale
        q_ref[0, h, 0:HEAD_DIM, :] = y.astype(BF16)
        q_ref[0, h, HEAD_DIM:, :] = jnp.zeros((LANES - HEAD_DIM, tm), BF16)

    cos = cos_ref[...]
    s1 = s1_ref[...]
    s2 = s2_ref[...]
    acc_k = _dot(hn, wk_ref[...])
    for g in range(N_KV):
        a = acc_k[:, LANES * g:LANES * (g + 1)]
        r = lax.rsqrt(jnp.sum(a * a, axis=-1, keepdims=True) * (1.0 / HEAD_DIM) + EPS)
        y = a * r * kg_ref[...]
        y = y * cos + pltpu.roll(y, LANES - 16, 1) * s1 + pltpu.roll(y, 16, 1) * s2
        k_ref[0, g] = y.astype(BF16)

    acc_vt = _dot_nt(wvT_ref[...], hn)
    ones_row = jnp.where(lax.broadcasted_iota(jnp.int32, (LANES - HEAD_DIM, tm), 0) == 0, 1.0, 0.0).astype(BF16)
    for g in range(N_KV):
        v_ref[0, g, 0:HEAD_DIM, :] = acc_vt[HEAD_DIM * g:HEAD_DIM * (g + 1)].astype(BF16)
        v_ref[0, g, HEAD_DIM:, :] = ones_row

    z_ref[0] = _dot(hn, wz_ref[...])
    xbc_ref[0] = _dot(hn, wxbc_ref[...])
    dt_ref[0] = _dot(hn, wdt_ref[...])
    dtT_ref[0] = _dot_nt(wdtT_ref[...], hn)


def _inproj(x, gain, wqT, wk, wvT, wz, wxbc, wdt, wdtT, qg, kg, tabs, tm):
    b, l, d = x.shape
    nt = l // tm
    const = lambda shape: pl.BlockSpec(shape, lambda bi, i: (0,) * len(shape))
    tab = pl.BlockSpec((tm, LANES), lambda bi, i: (i, 0))
    tab_t = pl.BlockSpec((HEAD_DIM, tm), lambda bi, i: (0, i))
    return pl.pallas_call(
        _inproj_kernel,
        grid=(b, nt),
        in_specs=[
            pl.BlockSpec((1, tm, d), lambda bi, i: (bi, i, 0)),
            const((1, d)), const(wqT.shape), const(wk.shape), const(wvT.shape), const(wz.shape),
            const(wxbc.shape), const(wdt.shape), const(wdtT.shape), const((HEAD_DIM, 1)), const((1, LANES)),
            tab, tab, tab, tab_t, tab_t,
        ],
        out_specs=[
            pl.BlockSpec((1, N_HEADS, LANES, tm), lambda bi, i: (bi, 0, 0, i)),
            pl.BlockSpec((1, N_KV, tm, LANES), lambda bi, i: (bi, 0, i, 0)),
            pl.BlockSpec((1, N_KV, LANES, tm), lambda bi, i: (bi, 0, 0, i)),
            pl.BlockSpec((1, tm, SSM_WIDTH), lambda bi, i: (bi, i, 0)),
            pl.BlockSpec((1, tm, XBC_WIDTH), lambda bi, i: (bi, i, 0)),
            pl.BlockSpec((1, tm, 2 * SSM_HEADS), lambda bi, i: (bi, i, 0)),
            pl.BlockSpec((1, 2 * SSM_HEADS, tm), lambda bi, i: (bi, 0, i)),
        ],
        out_shape=[
            jax.ShapeDtypeStruct((b, N_HEADS, LANES, l), BF16),
            jax.ShapeDtypeStruct((b, N_KV, l, LANES), BF16),
            jax.ShapeDtypeStruct((b, N_KV, LANES, l), BF16),
            jax.ShapeDtypeStruct((b, l, SSM_WIDTH), F32),
            jax.ShapeDtypeStruct((b, l, XBC_WIDTH), F32),
            jax.ShapeDtypeStruct((b, l, 2 * SSM_HEADS), F32),
            jax.ShapeDtypeStruct((b, 2 * SSM_HEADS, l), F32),
        ],
        compiler_params=_cparams(("parallel", "parallel")),
        name="inproj",
    )(x, gain, wqT, wk, wvT, wz, wxbc, wdt, wdtT, qg, kg, *tabs)


def _attn_kernel(q_ref, k_ref, v_ref, o_ref, qcat_ref, m_ref, acc_ref, sa_ref, sb_ref, *, tk):
    tq = q_ref.shape[3]
    seq = k_ref.shape[2]
    cols = HEADS_PER_KV * tq
    n_chunks = seq // tk
    for hh in range(HEADS_PER_KV):
        qcat_ref[:, hh * tq:(hh + 1) * tq] = q_ref[0, hh]
    m_ref[...] = jnp.full((1, cols), -jnp.inf, F32)
    acc_ref[...] = jnp.zeros((LANES, cols), F32)

    def scores(j, s_ref):
        off = pl.multiple_of(j * tk, tk)
        s_ref[...] = _dot(k_ref[0, 0, pl.ds(off, tk), :], qcat_ref[...])

    def accumulate(j, s_ref):
        off = pl.multiple_of(j * tk, tk)
        vc = v_ref[0, 0, :, pl.ds(off, tk)]
        s = s_ref[...]
        m_old = m_ref[...]
        m_new = jnp.maximum(m_old, jnp.max(s, axis=0, keepdims=True))
        p = jnp.exp2(s - m_new).astype(BF16)
        acc_ref[...] = jnp.exp2(m_old - m_new) * acc_ref[...] + _dot(vc, p)
        m_ref[...] = m_new

    scores(0, sa_ref)

    def body(i, carry):
        j = 2 * i
        scores(j + 1, sb_ref)
        accumulate(j, sa_ref)
        scores(jnp.minimum(j + 2, n_chunks - 1), sa_ref)
        accumulate(j + 1, sb_ref)
        return carry

    lax.fori_loop(0, n_chunks // 2, body, 0)
    acc = acc_ref[...]
    out = acc[0:HEAD_DIM] / acc[HEAD_DIM:HEAD_DIM + 1]
    out = jnp.concatenate([out, jnp.zeros((LANES - HEAD_DIM, cols), F32)], axis=0)
    for hh in range(HEADS_PER_KV):
        o_ref[0, hh] = out[:, hh * tq:(hh + 1) * tq].T


def _attention(q, k, v, tq, tk):
    b, _, _, l = q.shape
    cols = HEADS_PER_KV * tq
    return pl.pallas_call(
        functools.partial(_attn_kernel, tk=tk),
        grid=(b, N_KV, l // tq),
        in_specs=[
            pl.BlockSpec((1, HEADS_PER_KV, LANES, tq), lambda bi, g, i: (bi, g, 0, i)),
            pl.BlockSpec((1, 1, l, LANES), lambda bi, g, i: (bi, g, 0, 0)),
            pl.BlockSpec((1, 1, LANES, l), lambda bi, g, i: (bi, g, 0, 0)),
        ],
        out_specs=pl.BlockSpec((1, HEADS_PER_KV, tq, LANES), lambda bi, g, i: (bi, g, i, 0)),
        out_shape=jax.ShapeDtypeStruct((b, N_HEADS, l, LANES), F32),
        scratch_shapes=[
            pltpu.VMEM((LANES, cols), BF16),
            pltpu.VMEM((1, cols), F32),
            pltpu.VMEM((LANES, cols), F32),
            pltpu.VMEM((tk, cols), F32),
            pltpu.VMEM((tk, cols), F32),
        ],
        compiler_params=_cparams(("parallel", "parallel", "parallel")),
        name="attn",
    )(q, k, v)


def _ssd_kernel(*refs, reverse):
    if reverse:
        (cur_ref, prev_ref, next_ref, dt_ref, dtT_ref, convw_ref, convb_ref, dtb_row_ref, dtb_col_ref,
         alog_row_ref, alog_col_ref, yin_ref, z_ref, dskip_ref, gain_ref,
         y_ref, pad_ref, state_ref) = refs
    else:
        (cur_ref, prev_ref, next_ref, dt_ref, dtT_ref, convw_ref, convb_ref, dtb_row_ref, dtb_col_ref,
         alog_row_ref, alog_col_ref, y_ref, pad_ref, state_ref) = refs
    c = pl.program_id(1)
    nc = pl.num_programs(1)
    chunk = (nc - 1 - c) if reverse else c

    @pl.when(c == 0)
    def _():
        state_ref[...] = jnp.zeros(state_ref.shape, F32)

    halo = SUBLANES
    pad_ref[0:halo, :] = jnp.where(chunk > 0, prev_ref[0], 0.0)
    pad_ref[halo:halo + CHUNK, :] = cur_ref[0]
    pad_ref[halo + CHUNK:, :] = jnp.where(chunk < nc - 1, next_ref[0], 0.0)
    padded = pad_ref[...]
    rows_padded = CHUNK + 2 * halo
    conv = jnp.broadcast_to(convb_ref[...], (CHUNK, XBC_WIDTH))
    for j in range(CONV_K):
        shifted = pltpu.roll(padded, (CONV_K // 2 - j) % rows_padded, 0) if j != CONV_K // 2 else padded
        conv = conv + convw_ref[j:j + 1, :] * shifted[halo:halo + CHUNK]
    xc = conv * _sigmoid(conv)
    xs = xc[:, :SSM_WIDTH]

    dt_col = _softplus(dt_ref[0, 0] + dtb_row_ref[0])
    dt_row = _softplus(dtT_ref[0, 0] + dtb_col_ref[0])
    da_col = dt_col * -jnp.exp(alog_row_ref[0])
    da_row = dt_row * -jnp.exp(alog_col_ref[0])

    ri = lax.broadcasted_iota(jnp.int32, (CHUNK, CHUNK), 0)
    ci = lax.broadcasted_iota(jnp.int32, (CHUNK, CHUNK), 1)
    low = ci <= ri
    up = ci >= ri
    low_m = jnp.where(low, 1.0, 0.0).astype(BF16)
    up_m = jnp.where(up, 1.0, 0.0).astype(BF16)
    col_m, row_m, mask = (up_m, low_m, up) if reverse else (low_m, up_m, low)
    ch, cm, cl = _split3(da_col)
    acum_col = _dot(col_m, ch) + _dot(col_m, cm) + _dot(col_m, cl)
    rh, rm, rl = _split3(da_row)
    acum_row = _dot(rh, row_m) + _dot(rm, row_m) + _dot(rl, row_m)
    end = 0 if reverse else CHUNK - 1
    a_end_row = acum_col[end:end + 1, :]
    w_col = dt_col * jnp.exp(a_end_row - acum_col)
    e_col = jnp.exp(acum_col)
    chunk_decay = jnp.exp(a_end_row)

    pieces = []
    for g in range(SSM_GROUPS):
        bm = xc[:, SSM_WIDTH + g * SSM_STATE:SSM_WIDTH + (g + 1) * SSM_STATE]
        cmat = xc[:, SSM_WIDTH + (SSM_GROUPS + g) * SSM_STATE:SSM_WIDTH + (SSM_GROUPS + g + 1) * SSM_STATE]
        bm16 = bm.astype(BF16)
        cm16 = cmat.astype(BF16)
        cb = _dot_nt(cm16, bm16)
        bt16 = bm.T.astype(BF16)
        for hh in range(SSM_HEADS_PER_GROUP):
            h = g * SSM_HEADS_PER_GROUP + hh
            xh = xs[:, h * SSM_HEAD_DIM:(h + 1) * SSM_HEAD_DIM]
            seg = acum_col[:, h:h + 1] - acum_row[h:h + 1, :]
            decay = jnp.exp(jnp.where(mask, seg, -jnp.inf))
            mat = cb * decay * dt_row[h:h + 1, :]
            y_diag = _dot(mat.astype(BF16), xh.astype(BF16))
            prev_state = state_ref[h]
            y_off = _dot(cm16, prev_state.astype(BF16)) * e_col[:, h:h + 1]
            pieces.append(y_diag + y_off)
            xw = xh * w_col[:, h:h + 1]
            state_ref[h] = prev_state * chunk_decay[:, h:h + 1] + _dot(bt16, xw.astype(BF16))
    y = jnp.concatenate(pieces, axis=1)

    if not reverse:
        y_ref[0] = y
    else:
        y = y + yin_ref[0] + xs * dskip_ref[...]
        zz = z_ref[0]
        y = y * (zz * _sigmoid(zz))
        gw = SSM_WIDTH // SSM_GROUPS
        outs = []
        for g in range(SSM_GROUPS):
            yg = y[:, g * gw:(g + 1) * gw]
            outs.append(yg * lax.rsqrt(jnp.mean(yg * yg, axis=-1, keepdims=True) + EPS))
        y_ref[0] = jnp.concatenate(outs, axis=1) * gain_ref[...]


def _ssd(xbc, dt, dtT, convw, convb, dtb_row, dtb_col, alog_row, alog_col, direction,
         y_fwd=None, z=None, dskip=None, gain=None):
    b, l, _ = xbc.shape
    nc = l // CHUNK
    reverse = direction == 1
    blocks_per_chunk = CHUNK // SUBLANES
    nblk = l // SUBLANES

    def pos(c):
        return (nc - 1 - c) if reverse else c

    const = lambda shape: pl.BlockSpec(shape, lambda bi, c: (0,) * len(shape))
    dsel = lambda shape: pl.BlockSpec(shape, lambda bi, c: (direction,) + (0,) * (len(shape) - 1))
    in_specs = [
        pl.BlockSpec((1, CHUNK, XBC_WIDTH), lambda bi, c: (bi, pos(c), 0)),
        pl.BlockSpec((1, SUBLANES, XBC_WIDTH),
                     lambda bi, c: (bi, jnp.maximum(pos(c) * blocks_per_chunk - 1, 0), 0)),
        pl.BlockSpec((1, SUBLANES, XBC_WIDTH),
                     lambda bi, c: (bi, jnp.minimum((pos(c) + 1) * blocks_per_chunk, nblk - 1), 0)),
        pl.BlockSpec((1, 1, CHUNK, SSM_HEADS), lambda bi, c: (direction, bi, pos(c), 0)),
        pl.BlockSpec((1, 1, SSM_HEADS, CHUNK), lambda bi, c: (direction, bi, 0, pos(c))),
        const((CONV_K, XBC_WIDTH)), const((1, XBC_WIDTH)),
        dsel((1, 1, SSM_HEADS)), dsel((1, SSM_HEADS, 1)), dsel((1, 1, SSM_HEADS)), dsel((1, SSM_HEADS, 1)),
    ]
    args = [xbc, xbc, xbc, dt, dtT, convw, convb, dtb_row, dtb_col, alog_row, alog_col]
    if reverse:
        in_specs += [
            pl.BlockSpec((1, CHUNK, SSM_WIDTH), lambda bi, c: (bi, pos(c), 0)),
            pl.BlockSpec((1, CHUNK, SSM_WIDTH), lambda bi, c: (bi, pos(c), 0)),
            const((1, SSM_WIDTH)), const((1, SSM_WIDTH)),
        ]
        args += [y_fwd, z, dskip, gain]
    return pl.pallas_call(
        functools.partial(_ssd_kernel, reverse=reverse),
        grid=(b, nc),
        in_specs=in_specs,
        out_specs=pl.BlockSpec((1, CHUNK, SSM_WIDTH), lambda bi, c: (bi, pos(c), 0)),
        out_shape=jax.ShapeDtypeStruct((b, l, SSM_WIDTH), F32),
        scratch_shapes=[
            pltpu.VMEM((CHUNK + 2 * SUBLANES, XBC_WIDTH), F32),
            pltpu.VMEM((SSM_HEADS, SSM_STATE, SSM_HEAD_DIM), F32),
        ],
        compiler_params=_cparams(("parallel", "arbitrary")),
        name="ssd_bwd" if reverse else "ssd_fwd",
    )(*args)


def _outproj_kernel(attn_ref, ssm_ref, x_ref, again_ref, wa_ref, ws_ref, fgain_ref, wr_hi_ref, wr_lo_ref,
                    x1_ref, hn_ref, logit_ref):
    a = jnp.concatenate([attn_ref[0, h] for h in range(N_HEADS)], axis=1)
    r = lax.rsqrt(jnp.sum(a * a, axis=-1, keepdims=True) * (1.0 / ATTN_WIDTH) + EPS)
    an = (a * r * again_ref[...]).astype(BF16)
    x1 = x_ref[...] + _dot(an, wa_ref[...]) + _dot(ssm_ref[...].astype(BF16), ws_ref[...])
    x1_ref[...] = x1
    hn = x1 * lax.rsqrt(jnp.mean(x1 * x1, axis=-1, keepdims=True) + EPS) * fgain_ref[...]
    hn_ref[...] = hn
    h_hi = hn.astype(BF16)
    h_lo = (hn - h_hi.astype(F32)).astype(BF16)
    w_hi = wr_hi_ref[...]
    logit_ref[...] = _dot_nt(w_hi, h_hi) + _dot_nt(w_hi, h_lo) + _dot_nt(wr_lo_ref[...], h_hi)


def _outproj(attn, ssm, x, again, wa, ws, fgain, wr_hi, wr_lo, tm):
    b, _, l, _ = attn.shape
    t = b * l
    nt = l // tm
    const = lambda shape: pl.BlockSpec(shape, lambda i: (0,) * len(shape))
    return pl.pallas_call(
        _outproj_kernel,
        grid=(t // tm,),
        in_specs=[
            pl.BlockSpec((1, N_HEADS, tm, LANES), lambda i: (i // nt, 0, i % nt, 0)),
            pl.BlockSpec((tm, SSM_WIDTH), lambda i: (i, 0)),
            pl.BlockSpec((tm, D_MODEL), lambda i: (i, 0)),
            const((1, Q_PAD)), const(wa.shape), const(ws.shape), const((1, D_MODEL)),
            const(wr_hi.shape), const(wr_lo.shape),
        ],
        out_specs=[
            pl.BlockSpec((tm, D_MODEL), lambda i: (i, 0)),
            pl.BlockSpec((tm, D_MODEL), lambda i: (i, 0)),
            pl.BlockSpec((N_EXPERTS, tm), lambda i: (0, i)),
        ],
        out_shape=[
            jax.ShapeDtypeStruct((t, D_MODEL), F32),
            jax.ShapeDtypeStruct((t, D_MODEL), F32),
            jax.ShapeDtypeStruct((N_EXPERTS, t), F32),
        ],
        compiler_params=_cparams(("parallel",)),
        name="outproj",
    )(attn, ssm, x, again, wa, ws, fgain, wr_hi, wr_lo)


def _route_kernel(logit_ref, bias_ref, idx_ref, w_ref, rank_ref, count_ref, carry_ref):
    i = pl.program_id(0)
    tn = logit_ref.shape[1]

    @pl.when(i == 0)
    def _():
        carry_ref[...] = jnp.zeros(carry_ref.shape, F32)

    lg = logit_ref[...] + bias_ref[...]
    eid = lax.broadcasted_iota(jnp.int32, (N_EXPERTS, tn), 0).astype(F32)
    chosen = jnp.zeros((N_EXPERTS, tn), F32)
    vals, ids = [], []
    for _ in range(TOP_K):
        mx = jnp.max(lg, axis=0, keepdims=True)
        sel = jnp.min(jnp.where(lg == mx, eid, float(N_EXPERTS)), axis=0, keepdims=True)
        hit = eid == sel
        chosen = jnp.where(hit, 1.0, chosen)
        lg = jnp.where(hit, -jnp.inf, lg)
        vals.append(mx)
        ids.append(sel)
    ex = [jnp.exp(v - vals[0]) for v in vals]
    den = ex[0] + ex[1] + ex[2] + ex[3]
    ri = lax.broadcasted_iota(jnp.int32, (tn, tn), 0)
    ci = lax.broadcasted_iota(jnp.int32, (tn, tn), 1)
    upper = jnp.where(ri <= ci, 1.0, 0.0).astype(BF16)
    incl = _dot(chosen.astype(BF16), upper)
    rank_e = incl - chosen + carry_ref[...]
    carry_ref[...] = carry_ref[...] + incl[:, tn - 1:tn]
    for k in range(TOP_K):
        idx_ref[k:k + 1, :] = ids[k].astype(jnp.int32)
        w_ref[k:k + 1, :] = ex[k] / den
        rank_ref[k:k + 1, :] = jnp.sum(jnp.where(eid == ids[k], rank_e, 0.0), axis=0,
                                       keepdims=True).astype(jnp.int32)
    count_ref[...] = carry_ref[...].astype(jnp.int32)


def _route(logits_t, bias_col, tn):
    t = logits_t.shape[1]
    tok = pl.BlockSpec((TOP_K, tn), lambda i: (0, i))
    return pl.pallas_call(
        _route_kernel,
        grid=(t // tn,),
        in_specs=[pl.BlockSpec((N_EXPERTS, tn), lambda i: (0, i)),
                  pl.BlockSpec((N_EXPERTS, 1), lambda i: (0, 0))],
        out_specs=[tok, tok, tok, pl.BlockSpec((N_EXPERTS, 1), lambda i: (0, 0))],
        out_shape=[
            jax.ShapeDtypeStruct((TOP_K, t), jnp.int32),
            jax.ShapeDtypeStruct((TOP_K, t), F32),
            jax.ShapeDtypeStruct((TOP_K, t), jnp.int32),
            jax.ShapeDtypeStruct((N_EXPERTS, 1), jnp.int32),
        ],
        scratch_shapes=[pltpu.VMEM((N_EXPERTS, 1), F32)],
        compiler_params=_cparams(("arbitrary",)),
        name="route",
    )(logits_t, bias_col)


def _dest_kernel(idx_ref, rank_ref, start_ref, dest_ref):
    tn = idx_ref.shape[1]
    eid = lax.broadcasted_iota(jnp.int32, (N_EXPERTS, tn), 0)
    start = start_ref[...].astype(F32)
    for k in range(TOP_K):
        base = jnp.sum(jnp.where(eid == idx_ref[k:k + 1, :], start, 0.0), axis=0, keepdims=True)
        dest_ref[k:k + 1, :] = base.astype(jnp.int32) + rank_ref[k:k + 1, :]


def _dest(idx_t, rank_t, start_col, tn):
    t = idx_t.shape[1]
    tok = pl.BlockSpec((TOP_K, tn), lambda i: (0, i))
    return pl.pallas_call(
        _dest_kernel,
        grid=(t // tn,),
        in_specs=[tok, tok, pl.BlockSpec((N_EXPERTS, 1), lambda i: (0, 0))],
        out_specs=tok,
        out_shape=jax.ShapeDtypeStruct((TOP_K, t), jnp.int32),
        compiler_params=_cparams(("parallel",)),
        name="dest",
    )(idx_t, rank_t, start_col)


def _rowmap_kernel(lo_ref, hi_ref, dest_ref, code_ref, *, n_tok, bm):
    i = pl.program_id(0)
    tm = dest_ref.shape[1]
    spare = TOP_K * n_tok

    @pl.when(i == 0)
    def _():
        def lead(r, carry):
            code_ref[r] = spare + r
            return carry

        lax.fori_loop(0, bm, lead, 0)

        def segment(e, carry):
            def fill(r, c):
                code_ref[bm + r] = spare + (r & (bm - 1))
                return c

            lax.fori_loop(lo_ref[e], hi_ref[e], fill, 0)
            return carry

        lax.fori_loop(0, N_EXPERTS + 1, segment, 0)

    def token(t, carry):
        for k in range(TOP_K):
            code_ref[bm + dest_ref[k, t]] = TOP_K * (i * tm + t) + k
        return carry

    lax.fori_loop(0, tm, token, 0, unroll=8)


def _rowmap(seg_lo, seg_hi, dest, n_rows, bm, tm):
    t = dest.shape[1]
    return pl.pallas_call(
        functools.partial(_rowmap_kernel, n_tok=t, bm=bm),
        grid_spec=pltpu.PrefetchScalarGridSpec(
            num_scalar_prefetch=2,
            grid=(t // tm,),
            in_specs=[pl.BlockSpec((TOP_K, tm), lambda i, lo, hi: (0, i), memory_space=pltpu.SMEM)],
            out_specs=pl.BlockSpec(memory_space=pltpu.SMEM),
        ),
        out_shape=jax.ShapeDtypeStruct((n_rows + bm,), jnp.int32),
        compiler_params=_cparams(("arbitrary",)),
        name="rowmap",
    )(seg_lo, seg_hi, dest)


def _expert_kernel(block_e_ref, nused_ref, rows_prev_ref, rows_cur_ref, rows_next_ref, hn_ref,
                   wg_ref, bg_ref, wu_ref, bu_ref, wd_ref, bd_ref, ys_ref,
                   wg16, wu16, wd16, xa, xb, ya, yb, sems, *, n_tok, plane):
    b = pl.program_id(0)
    nb = pl.num_programs(0)
    bm = xa.shape[0]

    def gather_copy(rows_ref, r, xbuf, sem):
        tok = jnp.minimum(rows_ref[0, 0, r] >> TOP_K_BITS, n_tok - 1)
        return pltpu.make_async_copy(hn_ref.at[pl.ds(tok, 1), :], xbuf.at[pl.ds(r, 1), :], sem)

    def scatter_copy(rows_ref, r, ybuf, sem):
        code = rows_ref[0, 0, r]
        row = (code & (TOP_K - 1)) * plane + (code >> TOP_K_BITS)
        return pltpu.make_async_copy(ybuf.at[pl.ds(r, 1), :], ys_ref.at[pl.ds(row, 1), :], sem)

    def wait_block(buf, sem):
        pltpu.make_async_copy(hn_ref.at[pl.ds(0, bm), :], buf, sem).wait()

    @pl.when(b == 0)
    def _():
        def first(r, carry):
            gather_copy(rows_cur_ref, r, xa, sems.at[0]).start()
            return carry

        lax.fori_loop(0, bm, first, 0)
        yb[...] = jnp.zeros(yb.shape, F32)
        for k in range(TOP_K):
            spare = pltpu.make_async_copy(yb, ys_ref.at[pl.ds(k * plane + n_tok, bm), :], sems.at[3])
            spare.start()
            spare.wait()

    prev_e = block_e_ref[jnp.maximum(b - 1, 0)]

    @pl.when(jnp.logical_or(b == 0, block_e_ref[b] != prev_e))
    def _():
        wg16[...] = wg_ref[0].astype(BF16)
        wu16[...] = wu_ref[0].astype(BF16)
        wd16[...] = wd_ref[0].astype(BF16)

    def step(x_cur, y_cur, x_next, y_prev, g_cur, g_next, s_cur, s_prev):
        wait_block(x_cur, g_cur)

        @pl.when(b > 0)
        def _():
            wait_block(y_cur, s_cur)

        for r in range(bm):
            gather_copy(rows_next_ref, r, x_next, g_next).start()
            scatter_copy(rows_prev_ref, r, y_prev, s_prev).start()

        @pl.when(b < nused_ref[0])
        def _():
            x = x_cur[...].astype(BF16)
            g = _dot(x, wg16[...]) + bg_ref[0]
            u = _dot(x, wu16[...]) + bu_ref[0]
            g = jnp.minimum(g, SWIGLU_LIMIT)
            u = jnp.clip(u, -SWIGLU_LIMIT, SWIGLU_LIMIT)
            act = g * _sigmoid(SWIGLU_ALPHA * g) * (u + 1.0)
            y_cur[...] = _dot(act.astype(BF16), wd16[...]) + bd_ref[0]

        @pl.when(b == nb - 1)
        def _():
            def last(r, carry):
                scatter_copy(rows_cur_ref, r, y_cur, s_cur).start()
                return carry

            wait_block(y_prev, s_prev)
            lax.fori_loop(0, bm, last, 0)
            wait_block(y_cur, s_cur)
            wait_block(x_next, g_next)

    @pl.when(b % 2 == 0)
    def _():
        step(xa, ya, xb, yb, sems.at[0], sems.at[1], sems.at[2], sems.at[3])

    @pl.when(b % 2 == 1)
    def _():
        step(xb, yb, xa, ya, sems.at[1], sems.at[0], sems.at[3], sems.at[2])


def _experts(block_e, nused, codes, hn, wg, bg, wu, bu, wd, bd, bm, layer):
    n_tok, d = hn.shape
    nb = codes.shape[0] - 1
    plane = n_tok + bm
    base = layer * N_EXPERTS
    wspec = pl.BlockSpec((1, d, d), lambda i, be, nu: (base + be[i], 0, 0))
    bspec = pl.BlockSpec((1, 1, d), lambda i, be, nu: (base + be[i], 0, 0))
    rows = lambda shift: pl.BlockSpec((1, 1, bm), lambda i, be, nu: (jnp.minimum(i + shift, nb), 0, 0),
                                      memory_space=pltpu.SMEM)
    return pl.pallas_call(
        functools.partial(_expert_kernel, n_tok=n_tok, plane=plane),
        grid_spec=pltpu.PrefetchScalarGridSpec(
            num_scalar_prefetch=2,
            grid=(nb,),
            in_specs=[rows(0), rows(1), rows(2), pl.BlockSpec(memory_space=pl.ANY),
                      wspec, bspec, wspec, bspec, wspec, bspec],
            out_specs=pl.BlockSpec(memory_space=pl.ANY),
            scratch_shapes=[pltpu.VMEM((d, d), BF16)] * 3 + [pltpu.VMEM((bm, d), F32)] * 4
            + [pltpu.SemaphoreType.DMA((4,))],
        ),
        out_shape=jax.ShapeDtypeStruct((TOP_K * plane, d), F32),
        compiler_params=_cparams(("arbitrary",)),
        name="experts",
    )(block_e, nused, codes, codes, codes, hn, wg, bg, wu, bu, wd, bd)


def _combine_kernel(w_ref, x1_ref, ys_ref, out_ref):
    w = w_ref[...]
    acc = x1_ref[...]
    for k in range(TOP_K):
        acc = acc + w[:, k:k + 1] * ys_ref[k]
    out_ref[...] = acc


def _combine(w_tok, x1, ys, tm):
    t, d = x1.shape
    return pl.pallas_call(
        _combine_kernel,
        grid=(t // tm,),
        in_specs=[
            pl.BlockSpec((tm, TOP_K), lambda i: (i, 0)),
            pl.BlockSpec((tm, d), lambda i: (i, 0)),
            pl.BlockSpec((TOP_K, tm, d), lambda i: (0, i, 0)),
        ],
        out_specs=pl.BlockSpec((tm, d), lambda i: (i, 0)),
        out_shape=jax.ShapeDtypeStruct((t, d), F32),
        compiler_params=_cparams(("parallel",)),
        name="combine",
    )(w_tok, x1, ys)


def _pad_heads(w, n_heads):
    d = w.shape[0]
    w = w.reshape(d, n_heads, HEAD_DIM)
    return jnp.pad(w, ((0, 0), (0, 0), (0, LANES - HEAD_DIM))).reshape(d, n_heads * LANES)


def _rope_tables(seq_len):
    rows = seq_len // GRID_W
    inv_freq = ROPE_THETA ** (-jnp.arange(0, ROPE_AXIS_DIM, 2, dtype=F32) / ROPE_AXIS_DIM)
    ang_r = jnp.arange(rows, dtype=F32)[:, None] * inv_freq
    ang_c = jnp.arange(GRID_W, dtype=F32)[:, None] * inv_freq
    expand_r = lambda a: jnp.repeat(a, GRID_W, axis=0)
    expand_c = lambda a: jnp.tile(a, (rows, 1))
    cr, sr = expand_r(jnp.cos(ang_r)), expand_r(jnp.sin(ang_r))
    cc, sc = expand_c(jnp.cos(ang_c)), expand_c(jnp.sin(ang_c))
    zeros = jnp.zeros_like(cr)
    pad = jnp.zeros((seq_len, LANES - HEAD_DIM), F32)
    cos = jnp.concatenate([cr, cr, cc, cc, pad], axis=1)
    s_up = jnp.concatenate([-sr, zeros, -sc, zeros, pad], axis=1)
    s_dn = jnp.concatenate([zeros, sr, zeros, sc, pad], axis=1)
    cos_t = jnp.concatenate([cr, cr, cc, cc], axis=1).T
    sin_t = jnp.concatenate([-sr, sr, -sc, sc], axis=1).T
    return cos, s_up, s_dn, cos_t, sin_t


def _tiles(seq_len, n_tok):
    return dict(
        tm_in=min(512, seq_len),
        tq=min(256, seq_len),
        tk=min(512, seq_len // 2),
        tm_out=min(512, seq_len),
        tn_route=min(1024, n_tok),
        tm_rows=min(256, n_tok),
        bm=256,
    )


def _layer(x, p, i, tabs, tl):
    b, l, d = x.shape
    t = b * l
    w_in = p["w_in"][i]
    o_q, o_k, o_v, o_z, o_xbc = ATTN_WIDTH, ATTN_WIDTH + KV_WIDTH, ATTN_WIDTH + 2 * KV_WIDTH, \
        ATTN_WIDTH + 2 * KV_WIDTH + SSM_WIDTH, ATTN_WIDTH + 2 * KV_WIDTH + SSM_WIDTH + XBC_WIDTH
    wq_t = w_in[:, :o_q].T.astype(BF16)
    wk = _pad_heads(w_in[:, o_q:o_k], N_KV).astype(BF16)
    wv_t = w_in[:, o_k:o_v].T.astype(BF16)
    wz = w_in[:, o_v:o_z].astype(BF16)
    wxbc = w_in[:, o_z:o_xbc].astype(BF16)
    wdt = w_in[:, o_xbc:].astype(BF16)
    q4, k4, v4, z, xbc, dt, dt_t = _inproj(
        x, p["norm_mix"][i].reshape(1, d), wq_t, wk, wv_t, wz, wxbc, wdt, wdt.T,
        p["q_norm"][i].reshape(HEAD_DIM, 1), jnp.pad(p["k_norm"][i], (0, LANES - HEAD_DIM)).reshape(1, LANES),
        tabs, tl["tm_in"])

    attn = _attention(q4, k4, v4, tl["tq"], tl["tk"])

    dt_dir = dt.reshape(b, l, 2, SSM_HEADS).transpose(2, 0, 1, 3)
    dtt_dir = dt_t.reshape(b, 2, SSM_HEADS, l).transpose(1, 0, 2, 3)
    ssd_common = (xbc, dt_dir, dtt_dir, p["conv_w"][i], p["conv_b"][i].reshape(1, XBC_WIDTH),
                  p["dt_bias"][i].reshape(2, 1, SSM_HEADS), p["dt_bias"][i].reshape(2, SSM_HEADS, 1),
                  p["a_log"][i].reshape(2, 1, SSM_HEADS), p["a_log"][i].reshape(2, SSM_HEADS, 1))
    y_fwd = _ssd(*ssd_common, 0)
    ssm = _ssd(*ssd_common, 1, y_fwd=y_fwd, z=z,
               dskip=jnp.repeat(p["d_skip"][i], SSM_HEAD_DIM).reshape(1, SSM_WIDTH),
               gain=p["ssm_norm"][i].reshape(1, SSM_WIDTH))

    w_out = p["w_out"][i]
    wa = jnp.pad(w_out[:ATTN_WIDTH].reshape(N_HEADS, HEAD_DIM, d),
                 ((0, 0), (0, LANES - HEAD_DIM), (0, 0))).reshape(Q_PAD, d).astype(BF16)
    ws = w_out[ATTN_WIDTH:].astype(BF16)
    again = jnp.pad(p["attn_norm"][i].reshape(N_HEADS, HEAD_DIM),
                    ((0, 0), (0, LANES - HEAD_DIM))).reshape(1, Q_PAD)
    wr_t = p["w_router"][i].T
    wr_hi = wr_t.astype(BF16)
    wr_lo = (wr_t - wr_hi.astype(F32)).astype(BF16)
    x1, hn, logits_t = _outproj(attn, ssm.reshape(t, SSM_WIDTH), x.reshape(t, d), again, wa, ws,
                                p["norm_ffn"][i].reshape(1, d), wr_hi, wr_lo, tl["tm_out"])

    idx_t, w_t, rank_t, counts = _route(logits_t, p["b_router"][i].reshape(N_EXPERTS, 1), tl["tn_route"])

    bm = tl["bm"]
    padded = (counts.reshape(N_EXPERTS) + bm - 1) // bm * bm
    pad_end = jnp.cumsum(padded)
    pad_start = pad_end - padded
    dest = _dest(idx_t, rank_t, pad_start.reshape(N_EXPERTS, 1).astype(jnp.int32), tl["tn_route"])
    n_rows = t * TOP_K + N_EXPERTS * bm
    nb = n_rows // bm
    block_row = jnp.arange(nb, dtype=jnp.int32) * bm
    block_e = jnp.minimum(jnp.sum(pad_end[None, :] <= block_row[:, None], axis=1), N_EXPERTS - 1).astype(jnp.int32)
    tail = jnp.full((1,), n_rows, jnp.int32)
    seg_lo = jnp.concatenate([pad_start + counts.reshape(N_EXPERTS), pad_end[-1:]]).astype(jnp.int32)
    seg_hi = jnp.concatenate([pad_end, tail]).astype(jnp.int32)
    codes = _rowmap(seg_lo, seg_hi, dest, n_rows, bm, tl["tn_route"]).reshape(nb + 1, 1, bm)

    n_all = p["w_gate"].shape[0] * N_EXPERTS
    nused = (pad_end[-1:] // bm).astype(jnp.int32)
    ys = _experts(block_e, nused, codes, hn,
                  p["w_gate"].reshape(n_all, d, d), p["b_gate"].reshape(n_all, 1, d),
                  p["w_up"].reshape(n_all, d, d), p["b_up"].reshape(n_all, 1, d),
                  p["w_down"].reshape(n_all, d, d), p["b_down"].reshape(n_all, 1, d), bm, i)
    x2 = _combine(w_t.T, x1, ys.reshape(TOP_K, t + bm, d), tl["tm_rows"])
    return x2.reshape(b, l, d)


_PARAM_NAMES = ("norm_mix", "w_in", "q_norm", "k_norm", "conv_w", "conv_b", "dt_bias", "a_log", "d_skip",
                "ssm_norm", "attn_norm", "w_out", "norm_ffn", "w_router", "b_router", "w_gate", "b_gate",
                "w_up", "b_up", "w_down", "b_down")


def kernel(x, norm_mix, w_in, q_norm, k_norm, conv_w, conv_b, dt_bias, a_log, d_skip, ssm_norm, attn_norm,
           w_out, norm_ffn, w_router, b_router, w_gate, b_gate, w_up, b_up, w_down, b_down):
    params = dict(zip(_PARAM_NAMES, (norm_mix, w_in, q_norm, k_norm, conv_w, conv_b, dt_bias, a_log, d_skip,
                                     ssm_norm, attn_norm, w_out, norm_ffn, w_router, b_router, w_gate, b_gate,
                                     w_up, b_up, w_down, b_down)))
    b, l, _ = x.shape
    tabs = _rope_tables(l)
    tl = _tiles(l, b * l)
    for i in range(norm_mix.shape[0]):
        x = _layer(x, params, i, tabs, tl)
    return x
```

```python
import functools
import math

import jax
import jax.numpy as jnp
from jax import lax
from jax.experimental import pallas as pl
from jax.experimental.pallas import tpu as pltpu

F32 = jnp.float32
BF16 = jnp.bfloat16

D_MODEL = 1024
GRID_W = 64
HEAD_DIM = 64
N_HEADS = 8
N_KV = 2
HEADS_PER_KV = N_HEADS // N_KV
ATTN_WIDTH = N_HEADS * HEAD_DIM
KV_WIDTH = N_KV * HEAD_DIM
SSM_WIDTH = 512
SSM_HEADS = 8
SSM_HEAD_DIM = 64
SSM_GROUPS = 2
SSM_HEADS_PER_GROUP = SSM_HEADS // SSM_GROUPS
SSM_STATE = 128
CHUNK = 128
CONV_K = 5
XBC_WIDTH = SSM_WIDTH + 2 * SSM_GROUPS * SSM_STATE
ROPE_THETA = 10000.0
ROPE_AXIS_DIM = HEAD_DIM // 2
N_EXPERTS = 32
TOP_K = 4
SWIGLU_LIMIT = 7.0
SWIGLU_ALPHA = 1.702
EPS = 1e-6

LANES = 128
SUBLANES = 8
VMEM_LIMIT = 56 * 1024 * 1024

Q_PAD = N_HEADS * LANES


def _cparams(sem):
    return pltpu.CompilerParams(dimension_semantics=sem, vmem_limit_bytes=VMEM_LIMIT)


def _dot(a, b):
    return jnp.dot(a, b, preferred_element_type=F32)


def _dot_nt(a, b):
    return lax.dot_general(a, b, (((1,), (1,)), ((), ())), preferred_element_type=F32)


def _split3(x):
    hi = x.astype(BF16)
    r1 = x - hi.astype(F32)
    mid = r1.astype(BF16)
    lo = (r1 - mid.astype(F32)).astype(BF16)
    return hi, mid, lo


def _sigmoid(x):
    return 1.0 / (1.0 + jnp.exp(-x))


def _softplus(x):
    return jnp.maximum(x, 0.0) + jnp.log(1.0 + jnp.exp(-jnp.abs(x)))


def _inproj_kernel(x_ref, g_ref, wqT_ref, wk_ref, wvT_ref, wz_ref, wxbc_ref, wdt_ref, wdtT_ref,
                   qg_ref, kg_ref, cos_ref, s1_ref, s2_ref, cosT_ref, sinT_ref,
                   q_ref, k_ref, v_ref, z_ref, xbc_ref, dt_ref, dtT_ref):
    x = x_ref[0]
    tm = x.shape[0]
    hn = (x * lax.rsqrt(jnp.mean(x * x, axis=-1, keepdims=True) + EPS) * g_ref[...]).astype(BF16)

    cos_t = cosT_ref[...]
    sin_t = sinT_ref[...]
    q_scale = math.log2(math.e) / math.sqrt(HEAD_DIM)
    acc_qt = _dot_nt(wqT_ref[...], hn)
    half = ROPE_AXIS_DIM // 2
    for h in range(N_HEADS):
        a = acc_qt[HEAD_DIM * h:HEAD_DIM * (h + 1)]
        r = lax.rsqrt(jnp.sum(a * a, axis=0, keepdims=True) * (1.0 / HEAD_DIM) + EPS)
        y = a * r * qg_ref[...]
        swapped = jnp.concatenate([y[half:2 * half], y[0:half], y[3 * half:4 * half], y[2 * half:3 * half]], axis=0)
        y = (y * cos_t + swapped * sin_t) * q_scale
        q_ref[0, h, 0:HEAD_DIM, :] = y.astype(BF16)
        q_ref[0, h, HEAD_DIM:, :] = jnp.zeros((LANES - HEAD_DIM, tm), BF16)

    cos = cos_ref[...]
    s1 = s1_ref[...]
    s2 = s2_ref[...]
    acc_k = _dot(hn, wk_ref[...])
    for g in range(N_KV):
        a = acc_k[:, LANES * g:LANES * (g + 1)]
        r = lax.rsqrt(jnp.sum(a * a, axis=-1, keepdims=True) * (1.0 / HEAD_DIM) + EPS)
        y = a * r * kg_ref[...]
        y = y * cos + pltpu.roll(y, LANES - 16, 1) * s1 + pltpu.roll(y, 16, 1) * s2
        k_ref[0, g] = y.astype(BF16)

    acc_vt = _dot_nt(wvT_ref[...], hn)
    ones_row = jnp.where(lax.broadcasted_iota(jnp.int32, (LANES - HEAD_DIM, tm), 0) == 0, 1.0, 0.0).astype(BF16)
    for g in range(N_KV):
        v_ref[0, g, 0:HEAD_DIM, :] = acc_vt[HEAD_DIM * g:HEAD_DIM * (g + 1)].astype(BF16)
        v_ref[0, g, HEAD_DIM:, :] = ones_row

    z_ref[0] = _dot(hn, wz_ref[...])
    xbc_ref[0] = _dot(hn, wxbc_ref[...])
    dt_ref[0] = _dot(hn, wdt_ref[...])
    dtT_ref[0] = _dot_nt(wdtT_ref[...], hn)


def _inproj(x, gain, wqT, wk, wvT, wz, wxbc, wdt, wdtT, qg, kg, tabs, tm):
    b, l, d = x.shape
    nt = l // tm
    const = lambda shape: pl.BlockSpec(shape, lambda bi, i: (0,) * len(shape))
    tab = pl.BlockSpec((tm, LANES), lambda bi, i: (i, 0))
    tab_t = pl.BlockSpec((HEAD_DIM, tm), lambda bi, i: (0, i))
    return pl.pallas_call(
        _inproj_kernel,
        grid=(b, nt),
        in_specs=[
            pl.BlockSpec((1, tm, d), lambda bi, i: (bi, i, 0)),
            const((1, d)), const(wqT.shape), const(wk.shape), const(wvT.shape), const(wz.shape),
            const(wxbc.shape), const(wdt.shape), const(wdtT.shape), const((HEAD_DIM, 1)), const((1, LANES)),
            tab, tab, tab, tab_t, tab_t,
        ],
        out_specs=[
            pl.BlockSpec((1, N_HEADS, LANES, tm), lambda bi, i: (bi, 0, 0, i)),
            pl.BlockSpec((1, N_KV, tm, LANES), lambda bi, i: (bi, 0, i, 0)),
            pl.BlockSpec((1, N_KV, LANES, tm), lambda bi, i: (bi, 0, 0, i)),
            pl.BlockSpec((1, tm, SSM_WIDTH), lambda bi, i: (bi, i, 0)),
            pl.BlockSpec((1, tm, XBC_WIDTH), lambda bi, i: (bi, i, 0)),
            pl.BlockSpec((1, tm, 2 * SSM_HEADS), lambda bi, i: (bi, i, 0)),
            pl.BlockSpec((1, 2 * SSM_HEADS, tm), lambda bi, i: (bi, 0, i)),
        ],
        out_shape=[
            jax.ShapeDtypeStruct((b, N_HEADS, LANES, l), BF16),
            jax.ShapeDtypeStruct((b, N_KV, l, LANES), BF16),
            jax.ShapeDtypeStruct((b, N_KV, LANES, l), BF16),
            jax.ShapeDtypeStruct((b, l, SSM_WIDTH), F32),
            jax.ShapeDtypeStruct((b, l, XBC_WIDTH), F32),
            jax.ShapeDtypeStruct((b, l, 2 * SSM_HEADS), F32),
            jax.ShapeDtypeStruct((b, 2 * SSM_HEADS, l), F32),
        ],
        compiler_params=_cparams(("parallel", "parallel")),
        name="inproj",
    )(x, gain, wqT, wk, wvT, wz, wxbc, wdt, wdtT, qg, kg, *tabs)


def _attn_kernel(q_ref, k_ref, v_ref, o_ref, qcat_ref, m_ref, acc_ref, sa_ref, sb_ref, *, tk):
    tq = q_ref.shape[3]
    seq = k_ref.shape[2]
    cols = HEADS_PER_KV * tq
    n_chunks = seq // tk
    for hh in range(HEADS_PER_KV):
        qcat_ref[:, hh * tq:(hh + 1) * tq] = q_ref[0, hh]
    m_ref[...] = jnp.full((1, cols), -jnp.inf, F32)
    acc_ref[...] = jnp.zeros((LANES, cols), F32)

    def scores(j, s_ref):
        off = pl.multiple_of(j * tk, tk)
        s_ref[...] = _dot(k_ref[0, 0, pl.ds(off, tk), :], qcat_ref[...])

    def accumulate(j, s_ref):
        off = pl.multiple_of(j * tk, tk)
        vc = v_ref[0, 0, :, pl.ds(off, tk)]
        s = s_ref[...]
        m_old = m_ref[...]
        m_new = jnp.maximum(m_old, jnp.max(s, axis=0, keepdims=True))
        p = jnp.exp2(s - m_new).astype(BF16)
        acc_ref[...] = jnp.exp2(m_old - m_new) * acc_ref[...] + _dot(vc, p)
        m_ref[...] = m_new

    scores(0, sa_ref)

    def body(i, carry):
        j = 2 * i
        scores(j + 1, sb_ref)
        accumulate(j, sa_ref)
        scores(jnp.minimum(j + 2, n_chunks - 1), sa_ref)
        accumulate(j + 1, sb_ref)
        return carry

    lax.fori_loop(0, n_chunks // 2, body, 0)
    acc = acc_ref[...]
    out = acc[0:HEAD_DIM] / acc[HEAD_DIM:HEAD_DIM + 1]
    out = jnp.concatenate([out, jnp.zeros((LANES - HEAD_DIM, cols), F32)], axis=0)
    for hh in range(HEADS_PER_KV):
        o_ref[0, hh] = out[:, hh * tq:(hh + 1) * tq].T


def _attention(q, k, v, tq, tk):
    b, _, _, l = q.shape
    cols = HEADS_PER_KV * tq
    return pl.pallas_call(
        functools.partial(_attn_kernel, tk=tk),
        grid=(b, N_KV, l // tq),
        in_specs=[
            pl.BlockSpec((1, HEADS_PER_KV, LANES, tq), lambda bi, g, i: (bi, g, 0, i)),
            pl.BlockSpec((1, 1, l, LANES), lambda bi, g, i: (bi, g, 0, 0)),
            pl.BlockSpec((1, 1, LANES, l), lambda bi, g, i: (bi, g, 0, 0)),
        ],
        out_specs=pl.BlockSpec((1, HEADS_PER_KV, tq, LANES), lambda bi, g, i: (bi, g, i, 0)),
        out_shape=jax.ShapeDtypeStruct((b, N_HEADS, l, LANES), F32),
        scratch_shapes=[
            pltpu.VMEM((LANES, cols), BF16),
            pltpu.VMEM((1, cols), F32),
            pltpu.VMEM((LANES, cols), F32),
            pltpu.VMEM((tk, cols), F32),
            pltpu.VMEM((tk, cols), F32),
        ],
        compiler_params=_cparams(("parallel", "parallel", "parallel")),
        name="attn",
    )(q, k, v)


def _ssd_kernel(*refs, reverse):
    if reverse:
        (cur_ref, prev_ref, next_ref, dt_ref, dtT_ref, convw_ref, convb_ref, dtb_row_ref, dtb_col_ref,
         alog_row_ref, alog_col_ref, yin_ref, z_ref, dskip_ref, gain_ref,
         y_ref, pad_ref, state_ref) = refs
    else:
        (cur_ref, prev_ref, next_ref, dt_ref, dtT_ref, convw_ref, convb_ref, dtb_row_ref, dtb_col_ref,
         alog_row_ref, alog_col_ref, y_ref, pad_ref, state_ref) = refs
    c = pl.program_id(1)
    nc = pl.num_programs(1)
    chunk = (nc - 1 - c) if reverse else c

    @pl.when(c == 0)
    def _():
        state_ref[...] = jnp.zeros(state_ref.shape, F32)

    halo = SUBLANES
    pad_ref[0:halo, :] = jnp.where(chunk > 0, prev_ref[0], 0.0)
    pad_ref[halo:halo + CHUNK, :] = cur_ref[0]
    pad_ref[halo + CHUNK:, :] = jnp.where(chunk < nc - 1, next_ref[0], 0.0)
    padded = pad_ref[...]
    rows_padded = CHUNK + 2 * halo
    conv = jnp.broadcast_to(convb_ref[...], (CHUNK, XBC_WIDTH))
    for j in range(CONV_K):
        shifted = pltpu.roll(padded, (CONV_K // 2 - j) % rows_padded, 0) if j != CONV_K // 2 else padded
        conv = conv + convw_ref[j:j + 1, :] * shifted[halo:halo + CHUNK]
    xc = conv * _sigmoid(conv)
    xs = xc[:, :SSM_WIDTH]

    dt_col = _softplus(dt_ref[0, 0] + dtb_row_ref[0])
    dt_row = _softplus(dtT_ref[0, 0] + dtb_col_ref[0])
    da_col = dt_col * -jnp.exp(alog_row_ref[0])
    da_row = dt_row * -jnp.exp(alog_col_ref[0])

    ri = lax.broadcasted_iota(jnp.int32, (CHUNK, CHUNK), 0)
    ci = lax.broadcasted_iota(jnp.int32, (CHUNK, CHUNK), 1)
    low = ci <= ri
    up = ci >= ri
    low_m = jnp.where(low, 1.0, 0.0).astype(BF16)
    up_m = jnp.where(up, 1.0, 0.0).astype(BF16)
    col_m, row_m, mask = (up_m, low_m, up) if reverse else (low_m, up_m, low)
    ch, cm, cl = _split3(da_col)
    acum_col = _dot(col_m, ch) + _dot(col_m, cm) + _dot(col_m, cl)
    rh, rm, rl = _split3(da_row)
    acum_row = _dot(rh, row_m) + _dot(rm, row_m) + _dot(rl, row_m)
    end = 0 if reverse else CHUNK - 1
    a_end_row = acum_col[end:end + 1, :]
    w_col = dt_col * jnp.exp(a_end_row - acum_col)
    e_col = jnp.exp(acum_col)
    chunk_decay = jnp.exp(a_end_row)

    pieces = []
    for g in range(SSM_GROUPS):
        bm = xc[:, SSM_WIDTH + g * SSM_STATE:SSM_WIDTH + (g + 1) * SSM_STATE]
        cmat = xc[:, SSM_WIDTH + (SSM_GROUPS + g) * SSM_STATE:SSM_WIDTH + (SSM_GROUPS + g + 1) * SSM_STATE]
        bm16 = bm.astype(BF16)
        cm16 = cmat.astype(BF16)
        cb = _dot_nt(cm16, bm16)
        bt16 = bm.T.astype(BF16)
        for hh in range(SSM_HEADS_PER_GROUP):
            h = g * SSM_HEADS_PER_GROUP + hh
            xh = xs[:, h * SSM_HEAD_DIM:(h + 1) * SSM_HEAD_DIM]
            seg = acum_col[:, h:h + 1] - acum_row[h:h + 1, :]
            decay = jnp.exp(jnp.where(mask, seg, -jnp.inf))
            mat = cb * decay * dt_row[h:h + 1, :]
            y_diag = _dot(mat.astype(BF16), xh.astype(BF16))
            prev_state = state_ref[h]
            y_off = _dot(cm16, prev_state.astype(BF16)) * e_col[:, h:h + 1]
            pieces.append(y_diag + y_off)
            xw = xh * w_col[:, h:h + 1]
            state_ref[h] = prev_state * chunk_decay[:, h:h + 1] + _dot(bt16, xw.astype(BF16))
    y = jnp.concatenate(pieces, axis=1)

    if not reverse:
        y_ref[0] = y
    else:
        y = y + yin_ref[0] + xs * dskip_ref[...]
        zz = z_ref[0]
        y = y * (zz * _sigmoid(zz))
        gw = SSM_WIDTH // SSM_GROUPS
        outs = []
        for g in range(SSM_GROUPS):
            yg = y[:, g * gw:(g + 1) * gw]
            outs.append(yg * lax.rsqrt(jnp.mean(yg * yg, axis=-1, keepdims=True) + EPS))
        y_ref[0] = jnp.concatenate(outs, axis=1) * gain_ref[...]


def _ssd(xbc, dt, dtT, convw, convb, dtb_row, dtb_col, alog_row, alog_col, direction,
         y_fwd=None, z=None, dskip=None, gain=None):
    b, l, _ = xbc.shape
    nc = l // CHUNK
    reverse = direction == 1
    blocks_per_chunk = CHUNK // SUBLANES
    nblk = l // SUBLANES

    def pos(c):
        return (nc - 1 - c) if reverse else c

    const = lambda shape: pl.BlockSpec(shape, lambda bi, c: (0,) * len(shape))
    dsel = lambda shape: pl.BlockSpec(shape, lambda bi, c: (direction,) + (0,) * (len(shape) - 1))
    in_specs = [
        pl.BlockSpec((1, CHUNK, XBC_WIDTH), lambda bi, c: (bi, pos(c), 0)),
        pl.BlockSpec((1, SUBLANES, XBC_WIDTH),
                     lambda bi, c: (bi, jnp.maximum(pos(c) * blocks_per_chunk - 1, 0), 0)),
        pl.BlockSpec((1, SUBLANES, XBC_WIDTH),
                     lambda bi, c: (bi, jnp.minimum((pos(c) + 1) * blocks_per_chunk, nblk - 1), 0)),
        pl.BlockSpec((1, 1, CHUNK, SSM_HEADS), lambda bi, c: (direction, bi, pos(c), 0)),
        pl.BlockSpec((1, 1, SSM_HEADS, CHUNK), lambda bi, c: (direction, bi, 0, pos(c))),
        const((CONV_K, XBC_WIDTH)), const((1, XBC_WIDTH)),
        dsel((1, 1, SSM_HEADS)), dsel((1, SSM_HEADS, 1)), dsel((1, 1, SSM_HEADS)), dsel((1, SSM_HEADS, 1)),
    ]
    args = [xbc, xbc, xbc, dt, dtT, convw, convb, dtb_row, dtb_col, alog_row, alog_col]
    if reverse:
        in_specs += [
            pl.BlockSpec((1, CHUNK, SSM_WIDTH), lambda bi, c: (bi, pos(c), 0)),
            pl.BlockSpec((1, CHUNK, SSM_WIDTH), lambda bi, c: (bi, pos(c), 0)),
            const((1, SSM_WIDTH)), const((1, SSM_WIDTH)),
        ]
        args += [y_fwd, z, dskip, gain]
    return pl.pallas_call(
        functools.partial(_ssd_kernel, reverse=reverse),
        grid=(b, nc),
        in_specs=in_specs,
        out_specs=pl.BlockSpec((1, CHUNK, SSM_WIDTH), lambda bi, c: (bi, pos(c), 0)),
        out_shape=jax.ShapeDtypeStruct((b, l, SSM_WIDTH), F32),
        scratch_shapes=[
            pltpu.VMEM((CHUNK + 2 * SUBLANES, XBC_WIDTH), F32),
            pltpu.VMEM((SSM_HEADS, SSM_STATE, SSM_HEAD_DIM), F32),
        ],
        compiler_params=_cparams(("parallel", "arbitrary")),
        name="ssd_bwd" if reverse else "ssd_fwd",
    )(*args)


def _outproj_kernel(attn_ref, ssm_ref, x_ref, again_ref, wa_ref, ws_ref, fgain_ref, wr_hi_ref, wr_lo_ref,
                    x1_ref, hn_ref, logit_ref):
    a = jnp.concatenate([attn_ref[0, h] for h in range(N_HEADS)], axis=1)
    r = lax.rsqrt(jnp.sum(a * a, axis=-1, keepdims=True) * (1.0 / ATTN_WIDTH) + EPS)
    an = (a * r * again_ref[...]).astype(BF16)
    x1 = x_ref[...] + _dot(an, wa_ref[...]) + _dot(ssm_ref[...].astype(BF16), ws_ref[...])
    x1_ref[...] = x1
    hn = x1 * lax.rsqrt(jnp.mean(x1 * x1, axis=-1, keepdims=True) + EPS) * fgain_ref[...]
    hn_ref[...] = hn
    h_hi = hn.astype(BF16)
    h_lo = (hn - h_hi.astype(F32)).astype(BF16)
    w_hi = wr_hi_ref[...]
    logit_ref[...] = _dot_nt(w_hi, h_hi) + _dot_nt(w_hi, h_lo) + _dot_nt(wr_lo_ref[...], h_hi)


def _outproj(attn, ssm, x, again, wa, ws, fgain, wr_hi, wr_lo, tm):
    b, _, l, _ = attn.shape
    t = b * l
    nt = l // tm
    const = lambda shape: pl.BlockSpec(shape, lambda i: (0,) * len(shape))
    return pl.pallas_call(
        _outproj_kernel,
        grid=(t // tm,),
        in_specs=[
            pl.BlockSpec((1, N_HEADS, tm, LANES), lambda i: (i // nt, 0, i % nt, 0)),
            pl.BlockSpec((tm, SSM_WIDTH), lambda i: (i, 0)),
            pl.BlockSpec((tm, D_MODEL), lambda i: (i, 0)),
            const((1, Q_PAD)), const(wa.shape), const(ws.shape), const((1, D_MODEL)),
            const(wr_hi.shape), const(wr_lo.shape),
        ],
        out_specs=[
            pl.BlockSpec((tm, D_MODEL), lambda i: (i, 0)),
            pl.BlockSpec((tm, D_MODEL), lambda i: (i, 0)),
            pl.BlockSpec((N_EXPERTS, tm), lambda i: (0, i)),
        ],
        out_shape=[
            jax.ShapeDtypeStruct((t, D_MODEL), F32),
            jax.ShapeDtypeStruct((t, D_MODEL), F32),
            jax.ShapeDtypeStruct((N_EXPERTS, t), F32),
        ],
        compiler_params=_cparams(("parallel",)),
        name="outproj",
    )(attn, ssm, x, again, wa, ws, fgain, wr_hi, wr_lo)


def _route_kernel(logit_ref, bias_ref, idx_ref, w_ref, rank_ref, count_ref, tile_ref, carry_ref):
    i = pl.program_id(0)
    tn = logit_ref.shape[1]

    @pl.when(i == 0)
    def _():
        carry_ref[...] = jnp.zeros(carry_ref.shape, F32)

    lg = logit_ref[...] + bias_ref[...]
    eid = lax.broadcasted_iota(jnp.int32, (N_EXPERTS, tn), 0).astype(F32)
    chosen = jnp.zeros((N_EXPERTS, tn), F32)
    vals, ids = [], []
    for _ in range(TOP_K):
        mx = jnp.max(lg, axis=0, keepdims=True)
        sel = jnp.min(jnp.where(lg == mx, eid, float(N_EXPERTS)), axis=0, keepdims=True)
        hit = eid == sel
        chosen = jnp.where(hit, 1.0, chosen)
        lg = jnp.where(hit, -jnp.inf, lg)
        vals.append(mx)
        ids.append(sel)
    ex = [jnp.exp(v - vals[0]) for v in vals]
    den = ex[0] + ex[1] + ex[2] + ex[3]
    ri = lax.broadcasted_iota(jnp.int32, (tn, tn), 0)
    ci = lax.broadcasted_iota(jnp.int32, (tn, tn), 1)
    upper = jnp.where(ri <= ci, 1.0, 0.0).astype(BF16)
    incl = _dot(chosen.astype(BF16), upper)
    rank_e = incl - chosen + carry_ref[...]
    carry_ref[...] = carry_ref[...] + incl[:, tn - 1:tn]
    for k in range(TOP_K):
        idx_ref[k:k + 1, :] = ids[k].astype(jnp.int32)
        w_ref[k:k + 1, :] = ex[k] / den
        rank_ref[k:k + 1, :] = jnp.sum(jnp.where(eid == ids[k], rank_e, 0.0), axis=0,
                                       keepdims=True).astype(jnp.int32)
    count_ref[...] = carry_ref[...].astype(jnp.int32)
    tile_ref[0] = incl[:, tn - 1:tn].astype(jnp.int32)


def _route(logits_t, bias_col, tn):
    t = logits_t.shape[1]
    tok = pl.BlockSpec((TOP_K, tn), lambda i: (0, i))
    return pl.pallas_call(
        _route_kernel,
        grid=(t // tn,),
        in_specs=[pl.BlockSpec((N_EXPERTS, tn), lambda i: (0, i)),
                  pl.BlockSpec((N_EXPERTS, 1), lambda i: (0, 0))],
        out_specs=[tok, tok, tok, pl.BlockSpec((N_EXPERTS, 1), lambda i: (0, 0)),
                   pl.BlockSpec((1, N_EXPERTS, 1), lambda i: (i, 0, 0))],
        out_shape=[
            jax.ShapeDtypeStruct((TOP_K, t), jnp.int32),
            jax.ShapeDtypeStruct((TOP_K, t), F32),
            jax.ShapeDtypeStruct((TOP_K, t), jnp.int32),
            jax.ShapeDtypeStruct((N_EXPERTS, 1), jnp.int32),
            jax.ShapeDtypeStruct((t // tn, N_EXPERTS, 1), jnp.int32),
        ],
        scratch_shapes=[pltpu.VMEM((N_EXPERTS, 1), F32)],
        compiler_params=_cparams(("arbitrary",)),
        name="route",
    )(logits_t, bias_col)


RUN_ALIGN = SUBLANES
RUN_BITS = (256, 128, 64, 32, 16, 8)


def _copy_run(src_ref, dst_ref, src_start, dst_start, count, sem, start=True, wait=True):
    for bit in RUN_BITS:
        take = count & bit

        @pl.when(take != 0)
        def _():
            cp = pltpu.make_async_copy(src_ref.at[pl.ds(pl.multiple_of(src_start, RUN_ALIGN), bit), :],
                                       dst_ref.at[pl.ds(pl.multiple_of(dst_start, RUN_ALIGN), bit), :], sem)
            if start:
                cp.start()
            if wait:
                cp.wait()

        src_start = src_start + take
        dst_start = dst_start + take


def _tile_positions_rows(idx_ref, off_col, tri_ref):
    tn = idx_ref.shape[1]
    eid = lax.broadcasted_iota(jnp.int32, (N_EXPERTS, tn), 0)
    hits = [eid == idx_ref[k:k + 1, :] for k in range(TOP_K)]
    chosen = jnp.where(hits[0] | hits[1] | hits[2] | hits[3], 1.0, 0.0)
    incl = _dot(chosen.astype(BF16), tri_ref[...])
    slot = off_col + incl - chosen
    return [jnp.sum(jnp.where(h, slot, 0.0), axis=0, keepdims=True).astype(jnp.int32) for h in hits]


def _dispatch_kernel(seg_lo_ref, seg_hi_ref, tab_ref, idx_ref, off_ref, tri_ref, hn_ref, xs_ref,
                     sorted_ref, zero_ref, sem, *, bm):
    i = pl.program_id(0)
    tn = idx_ref.shape[1]
    n_sorted = sorted_ref.shape[0]

    @pl.when(i == 0)
    def _():
        zero_ref[...] = jnp.zeros(zero_ref.shape, F32)

        def segment(e, carry):
            lo = seg_lo_ref[e]
            count = seg_hi_ref[e] - lo

            def whole(j, c):
                cp = pltpu.make_async_copy(zero_ref, xs_ref.at[pl.ds(pl.multiple_of(lo + j * bm, RUN_ALIGN), bm), :], sem)
                cp.start()
                cp.wait()
                return c

            lax.fori_loop(0, count // bm, whole, 0)
            _copy_run(zero_ref, xs_ref, 0, lo + count // bm * bm, count % bm, sem)
            return carry

        lax.fori_loop(0, N_EXPERTS + 1, segment, 0)

    lpos = _tile_positions_rows(idx_ref, off_ref[0].astype(F32), tri_ref)
    pos = lax.broadcasted_iota(jnp.int32, (n_sorted, tn), 0)
    onehot = (pos == lpos[0]) | (pos == lpos[1]) | (pos == lpos[2]) | (pos == lpos[3])
    sorted_ref[...] = _dot(jnp.where(onehot, 1.0, 0.0).astype(BF16), hn_ref[...].astype(BF16))

    def run(start):
        def body(e, carry):
            _copy_run(sorted_ref, xs_ref, tab_ref[0, 0, e], tab_ref[0, 0, 2 * N_EXPERTS + e],
                      tab_ref[0, 0, N_EXPERTS + e], sem, start=start, wait=not start)
            return carry

        return body

    lax.fori_loop(0, N_EXPERTS, run(True), 0)
    lax.fori_loop(0, N_EXPERTS, run(False), 0)


def _dispatch(seg_lo, seg_hi, tab, idx_t, off_col, tri, hn, n_rows, n_sorted, bm, tn):
    t, d = hn.shape
    return pl.pallas_call(
        functools.partial(_dispatch_kernel, bm=bm),
        grid_spec=pltpu.PrefetchScalarGridSpec(
            num_scalar_prefetch=2,
            grid=(t // tn,),
            in_specs=[
                pl.BlockSpec((1, 1, 3 * N_EXPERTS), lambda i, lo, hi: (i, 0, 0), memory_space=pltpu.SMEM),
                pl.BlockSpec((TOP_K, tn), lambda i, lo, hi: (0, i)),
                pl.BlockSpec((1, N_EXPERTS, 1), lambda i, lo, hi: (i, 0, 0)),
                pl.BlockSpec((tn, tn), lambda i, lo, hi: (0, 0)),
                pl.BlockSpec((tn, d), lambda i, lo, hi: (i, 0)),
            ],
            out_specs=pl.BlockSpec(memory_space=pl.ANY),
            scratch_shapes=[pltpu.VMEM((n_sorted, d), F32), pltpu.VMEM((bm, d), F32),
                            pltpu.SemaphoreType.DMA(())],
        ),
        out_shape=jax.ShapeDtypeStruct((n_rows, d), F32),
        compiler_params=_cparams(("arbitrary",)),
        name="dispatch",
    )(seg_lo, seg_hi, tab, idx_t, off_col, tri, hn)


def _expert_kernel(block_e_ref, nused_ref, xs_ref, wg_ref, bg_ref, wu_ref, bu_ref, wd_ref, bd_ref,
                   ys_ref, wg16, wu16, wd16):
    b = pl.program_id(0)
    used = b < nused_ref[0]
    prev_e = block_e_ref[jnp.maximum(b - 1, 0)]
    fresh = jnp.logical_or(b == 0, block_e_ref[b] != prev_e)

    @pl.when(jnp.logical_and(used, fresh))
    def _():
        wg16[...] = wg_ref[0].astype(BF16)
        wu16[...] = wu_ref[0].astype(BF16)
        wd16[...] = wd_ref[0].astype(BF16)

    @pl.when(used)
    def _():
        x = xs_ref[...].astype(BF16)
        g = _dot(x, wg16[...]) + bg_ref[0]
        u = _dot(x, wu16[...]) + bu_ref[0]
        g = jnp.minimum(g, SWIGLU_LIMIT)
        u = jnp.clip(u, -SWIGLU_LIMIT, SWIGLU_LIMIT)
        act = g * _sigmoid(SWIGLU_ALPHA * g) * (u + 1.0)
        ys_ref[...] = _dot(act.astype(BF16), wd16[...]) + bd_ref[0]

    @pl.when(jnp.logical_not(used))
    def _():
        ys_ref[...] = jnp.zeros(ys_ref.shape, F32)


def _experts(block_e, nused, xs, wg, bg, wu, bu, wd, bd, bm, layer):
    n_rows, d = xs.shape
    nb = n_rows // bm
    base = layer * N_EXPERTS
    wspec = pl.BlockSpec((1, d, d), lambda i, be, nu: (base + be[i], 0, 0))
    bspec = pl.BlockSpec((1, 1, d), lambda i, be, nu: (base + be[i], 0, 0))
    row = pl.BlockSpec((bm, d), lambda i, be, nu: (i, 0))
    return pl.pallas_call(
        _expert_kernel,
        grid_spec=pltpu.PrefetchScalarGridSpec(
            num_scalar_prefetch=2,
            grid=(nb,),
            in_specs=[row, wspec, bspec, wspec, bspec, wspec, bspec],
            out_specs=row,
            scratch_shapes=[pltpu.VMEM((d, d), BF16)] * 3,
        ),
        out_shape=jax.ShapeDtypeStruct((n_rows, d), F32),
        compiler_params=_cparams(("arbitrary",)),
        name="experts",
    )(block_e, nused, xs, wg, bg, wu, bu, wd, bd)


def _combine_kernel(tab_ref, idx_ref, w_ref, off_ref, tri_ref, x1_ref, ys_ref, out_ref, sorted_ref, sem):
    tn = x1_ref.shape[0]
    n_sorted = sorted_ref.shape[0]

    def run(start):
        def body(e, carry):
            _copy_run(ys_ref, sorted_ref, tab_ref[0, 0, 2 * N_EXPERTS + e], tab_ref[0, 0, e],
                      tab_ref[0, 0, N_EXPERTS + e], sem, start=start, wait=not start)
            return carry

        return body

    lax.fori_loop(0, N_EXPERTS, run(True), 0)

    idx = idx_ref[...]
    eid = lax.broadcasted_iota(jnp.int32, (tn, N_EXPERTS), 1)
    hits = [eid == idx[:, k:k + 1] for k in range(TOP_K)]
    chosen = jnp.where(hits[0] | hits[1] | hits[2] | hits[3], 1.0, 0.0)
    incl = _dot(tri_ref[...], chosen.astype(BF16))
    slot = off_ref[0].astype(F32) + incl - chosen
    w = w_ref[...]
    pos = lax.broadcasted_iota(jnp.int32, (tn, n_sorted), 1)
    weights = jnp.zeros((tn, n_sorted), F32)
    for k in range(TOP_K):
        lpos = jnp.sum(jnp.where(hits[k], slot, 0.0), axis=1, keepdims=True).astype(jnp.int32)
        weights = jnp.where(pos == lpos, w[:, k:k + 1], weights)
    w_hi = weights.astype(BF16)
    w_lo = (weights - w_hi.astype(F32)).astype(BF16)

    lax.fori_loop(0, N_EXPERTS, run(False), 0)
    row_id = lax.broadcasted_iota(jnp.int32, (n_sorted, 1), 0)
    y = jnp.where(row_id < tab_ref[0, 0, N_EXPERTS - 1] + tab_ref[0, 0, 2 * N_EXPERTS - 1], sorted_ref[...], 0.0)
    y_hi = y.astype(BF16)
    y_lo = (y - y_hi.astype(F32)).astype(BF16)
    out_ref[...] = x1_ref[...] + _dot(w_hi, y_hi) + _dot(w_hi, y_lo) + _dot(w_lo, y_hi)


def _combine(tab, idx_tok, w_tok, off_row, tri, x1, ys, n_sorted, tn):
    t, d = x1.shape
    return pl.pallas_call(
        _combine_kernel,
        grid=(t // tn,),
        in_specs=[
            pl.BlockSpec((1, 1, 3 * N_EXPERTS), lambda i: (i, 0, 0), memory_space=pltpu.SMEM),
            pl.BlockSpec((tn, TOP_K), lambda i: (i, 0)),
            pl.BlockSpec((tn, TOP_K), lambda i: (i, 0)),
            pl.BlockSpec((1, 1, N_EXPERTS), lambda i: (i, 0, 0)),
            pl.BlockSpec((tn, tn), lambda i: (0, 0)),
            pl.BlockSpec((tn, d), lambda i: (i, 0)),
            pl.BlockSpec(memory_space=pl.ANY),
        ],
        out_specs=pl.BlockSpec((tn, d), lambda i: (i, 0)),
        out_shape=jax.ShapeDtypeStruct((t, d), F32),
        scratch_shapes=[pltpu.VMEM((n_sorted, d), F32), pltpu.SemaphoreType.DMA(())],
        compiler_params=_cparams(("arbitrary",)),
        name="combine",
    )(tab, idx_tok, w_tok, off_row, tri, x1, ys)


def _pad_heads(w, n_heads):
    d = w.shape[0]
    w = w.reshape(d, n_heads, HEAD_DIM)
    return jnp.pad(w, ((0, 0), (0, 0), (0, LANES - HEAD_DIM))).reshape(d, n_heads * LANES)


def _rope_tables(seq_len):
    rows = seq_len // GRID_W
    inv_freq = ROPE_THETA ** (-jnp.arange(0, ROPE_AXIS_DIM, 2, dtype=F32) / ROPE_AXIS_DIM)
    ang_r = jnp.arange(rows, dtype=F32)[:, None] * inv_freq
    ang_c = jnp.arange(GRID_W, dtype=F32)[:, None] * inv_freq
    expand_r = lambda a: jnp.repeat(a, GRID_W, axis=0)
    expand_c = lambda a: jnp.tile(a, (rows, 1))
    cr, sr = expand_r(jnp.cos(ang_r)), expand_r(jnp.sin(ang_r))
    cc, sc = expand_c(jnp.cos(ang_c)), expand_c(jnp.sin(ang_c))
    zeros = jnp.zeros_like(cr)
    pad = jnp.zeros((seq_len, LANES - HEAD_DIM), F32)
    cos = jnp.concatenate([cr, cr, cc, cc, pad], axis=1)
    s_up = jnp.concatenate([-sr, zeros, -sc, zeros, pad], axis=1)
    s_dn = jnp.concatenate([zeros, sr, zeros, sc, pad], axis=1)
    cos_t = jnp.concatenate([cr, cr, cc, cc], axis=1).T
    sin_t = jnp.concatenate([-sr, sr, -sc, sc], axis=1).T
    return cos, s_up, s_dn, cos_t, sin_t


def _tiles(seq_len, n_tok):
    return dict(
        tm_in=min(512, seq_len),
        tq=min(256, seq_len),
        tk=min(512, seq_len // 2),
        tm_out=min(512, seq_len),
        tn_route=min(256, n_tok),
        bm=256,
    )


def _layer(x, p, i, tabs, tl):
    b, l, d = x.shape
    t = b * l
    w_in = p["w_in"][i]
    o_q, o_k, o_v, o_z, o_xbc = ATTN_WIDTH, ATTN_WIDTH + KV_WIDTH, ATTN_WIDTH + 2 * KV_WIDTH, \
        ATTN_WIDTH + 2 * KV_WIDTH + SSM_WIDTH, ATTN_WIDTH + 2 * KV_WIDTH + SSM_WIDTH + XBC_WIDTH
    wq_t = w_in[:, :o_q].T.astype(BF16)
    wk = _pad_heads(w_in[:, o_q:o_k], N_KV).astype(BF16)
    wv_t = w_in[:, o_k:o_v].T.astype(BF16)
    wz = w_in[:, o_v:o_z].astype(BF16)
    wxbc = w_in[:, o_z:o_xbc].astype(BF16)
    wdt = w_in[:, o_xbc:].astype(BF16)
    q4, k4, v4, z, xbc, dt, dt_t = _inproj(
        x, p["norm_mix"][i].reshape(1, d), wq_t, wk, wv_t, wz, wxbc, wdt, wdt.T,
        p["q_norm"][i].reshape(HEAD_DIM, 1), jnp.pad(p["k_norm"][i], (0, LANES - HEAD_DIM)).reshape(1, LANES),
        tabs, tl["tm_in"])

    attn = _attention(q4, k4, v4, tl["tq"], tl["tk"])

    dt_dir = dt.reshape(b, l, 2, SSM_HEADS).transpose(2, 0, 1, 3)
    dtt_dir = dt_t.reshape(b, 2, SSM_HEADS, l).transpose(1, 0, 2, 3)
    ssd_common = (xbc, dt_dir, dtt_dir, p["conv_w"][i], p["conv_b"][i].reshape(1, XBC_WIDTH),
                  p["dt_bias"][i].reshape(2, 1, SSM_HEADS), p["dt_bias"][i].reshape(2, SSM_HEADS, 1),
                  p["a_log"][i].reshape(2, 1, SSM_HEADS), p["a_log"][i].reshape(2, SSM_HEADS, 1))
    y_fwd = _ssd(*ssd_common, 0)
    ssm = _ssd(*ssd_common, 1, y_fwd=y_fwd, z=z,
               dskip=jnp.repeat(p["d_skip"][i], SSM_HEAD_DIM).reshape(1, SSM_WIDTH),
               gain=p["ssm_norm"][i].reshape(1, SSM_WIDTH))

    w_out = p["w_out"][i]
    wa = jnp.pad(w_out[:ATTN_WIDTH].reshape(N_HEADS, HEAD_DIM, d),
                 ((0, 0), (0, LANES - HEAD_DIM), (0, 0))).reshape(Q_PAD, d).astype(BF16)
    ws = w_out[ATTN_WIDTH:].astype(BF16)
    again = jnp.pad(p["attn_norm"][i].reshape(N_HEADS, HEAD_DIM),
                    ((0, 0), (0, LANES - HEAD_DIM))).reshape(1, Q_PAD)
    wr_t = p["w_router"][i].T
    wr_hi = wr_t.astype(BF16)
    wr_lo = (wr_t - wr_hi.astype(F32)).astype(BF16)
    x1, hn, logits_t = _outproj(attn, ssm.reshape(t, SSM_WIDTH), x.reshape(t, d), again, wa, ws,
                                p["norm_ffn"][i].reshape(1, d), wr_hi, wr_lo, tl["tm_out"])

    tn = tl["tn_route"]
    idx_t, w_t, rank_t, counts, tile_counts = _route(logits_t, p["b_router"][i].reshape(N_EXPERTS, 1), tn)
    del rank_t

    bm = tl["bm"]
    nt = t // tn
    del counts
    tile_counts = tile_counts.reshape(nt, N_EXPERTS)
    run_len = (tile_counts + RUN_ALIGN - 1) // RUN_ALIGN * RUN_ALIGN
    run_off = jnp.cumsum(run_len, axis=1) - run_len
    rows_e = jnp.sum(run_len, axis=0)
    padded = (rows_e + bm - 1) // bm * bm
    pad_end = jnp.cumsum(padded)
    pad_start = pad_end - padded
    run_row = pad_start[None, :] + jnp.cumsum(run_len, axis=0) - run_len
    n_sorted = TOP_K * tn + N_EXPERTS * RUN_ALIGN
    n_rows = t * TOP_K + nt * N_EXPERTS * RUN_ALIGN + N_EXPERTS * bm
    nb = n_rows // bm
    block_row = jnp.arange(nb, dtype=jnp.int32) * bm
    block_e = jnp.minimum(jnp.sum(pad_end[None, :] <= block_row[:, None], axis=1), N_EXPERTS - 1).astype(jnp.int32)
    nused = (pad_end[-1:] // bm).astype(jnp.int32)
    seg_lo = jnp.concatenate([pad_start + rows_e, pad_end[-1:]]).astype(jnp.int32)
    seg_hi = jnp.concatenate([pad_end, jnp.full((1,), n_rows, jnp.int32)]).astype(jnp.int32)
    tab = jnp.concatenate([run_off, run_len, run_row], axis=1).astype(jnp.int32).reshape(nt, 1, 3 * N_EXPERTS)
    ones = jnp.ones((tn, tn), BF16)

    xs = _dispatch(seg_lo, seg_hi, tab, idx_t, run_off.reshape(nt, N_EXPERTS, 1).astype(jnp.int32),
                   jnp.triu(ones), hn, n_rows, n_sorted, bm, tn)
    n_all = p["w_gate"].shape[0] * N_EXPERTS
    ys = _experts(block_e, nused, xs,
                  p["w_gate"].reshape(n_all, d, d), p["b_gate"].reshape(n_all, 1, d),
                  p["w_up"].reshape(n_all, d, d), p["b_up"].reshape(n_all, 1, d),
                  p["w_down"].reshape(n_all, d, d), p["b_down"].reshape(n_all, 1, d), bm, i)
    x2 = _combine(tab, idx_t.T, w_t.T, run_off.reshape(nt, 1, N_EXPERTS).astype(jnp.int32),
                  jnp.tril(ones), x1, ys, n_sorted, tn)
    return x2.reshape(b, l, d)


_PARAM_NAMES = ("norm_mix", "w_in", "q_norm", "k_norm", "conv_w", "conv_b", "dt_bias", "a_log", "d_skip",
                "ssm_norm", "attn_norm", "w_out", "norm_ffn", "w_router", "b_router", "w_gate", "b_gate",
                "w_up", "b_up", "w_down", "b_down")


def kernel(x, norm_mix, w_in, q_norm, k_norm, conv_w, conv_b, dt_bias, a_log, d_skip, ssm_norm, attn_norm,
           w_out, norm_ffn, w_router, b_router, w_gate, b_gate, w_up, b_up, w_down, b_down):
    params = dict(zip(_PARAM_NAMES, (norm_mix, w_in, q_norm, k_norm, conv_w, conv_b, dt_bias, a_log, d_skip,
                                     ssm_norm, attn_norm, w_out, norm_ffn, w_router, b_router, w_gate, b_gate,
                                     w_up, b_up, w_down, b_down)))
    b, l, _ = x.shape
    tabs = _rope_tables(l)
    tl = _tiles(l, b * l)
    for i in range(norm_mix.shape[0]):
        x = _layer(x, params, i, tabs, tl)
    return x
```

```python
import functools
import math

import jax
import jax.numpy as jnp
from jax import lax
from jax.experimental import pallas as pl
from jax.experimental.pallas import tpu as pltpu

F32 = jnp.float32
BF16 = jnp.bfloat16

D_MODEL = 1024
GRID_W = 64
HEAD_DIM = 64
N_HEADS = 8
N_KV = 2
HEADS_PER_KV = N_HEADS // N_KV
ATTN_WIDTH = N_HEADS * HEAD_DIM
KV_WIDTH = N_KV * HEAD_DIM
SSM_WIDTH = 512
SSM_HEADS = 8
SSM_HEAD_DIM = 64
SSM_GROUPS = 2
SSM_HEADS_PER_GROUP = SSM_HEADS // SSM_GROUPS
SSM_STATE = 128
CHUNK = 128
CONV_K = 5
XBC_WIDTH = SSM_WIDTH + 2 * SSM_GROUPS * SSM_STATE
ROPE_THETA = 10000.0
ROPE_AXIS_DIM = HEAD_DIM // 2
N_EXPERTS = 32
TOP_K = 4
SWIGLU_LIMIT = 7.0
SWIGLU_ALPHA = 1.702
EPS = 1e-6

LANES = 128
SUBLANES = 8
VMEM_LIMIT = 56 * 1024 * 1024

Q_PAD = N_HEADS * LANES
ATTN_SCORE_BUFFERS = 2


def _cparams(sem):
    return pltpu.CompilerParams(dimension_semantics=sem, vmem_limit_bytes=VMEM_LIMIT)


def _dot(a, b):
    return jnp.dot(a, b, preferred_element_type=F32)


def _dot_nt(a, b):
    return lax.dot_general(a, b, (((1,), (1,)), ((), ())), preferred_element_type=F32)


def _split3(x):
    hi = x.astype(BF16)
    r1 = x - hi.astype(F32)
    mid = r1.astype(BF16)
    lo = (r1 - mid.astype(F32)).astype(BF16)
    return hi, mid, lo


def _sigmoid(x):
    return 1.0 / (1.0 + jnp.exp(-x))


def _softplus(x):
    return jnp.maximum(x, 0.0) + jnp.log(1.0 + jnp.exp(-jnp.abs(x)))


def _inproj_kernel(x_ref, g_ref, wqT_ref, wk_ref, wvT_ref, wz_ref, wxbc_ref, wdt_ref, wdtT_ref,
                   qg_ref, kg_ref, cos_ref, s1_ref, s2_ref, cosT_ref, sinT_ref,
                   q_ref, k_ref, v_ref, z_ref, xbc_ref, dt_ref, dtT_ref):
    x = x_ref[0]
    tm = x.shape[0]
    hn = (x * lax.rsqrt(jnp.mean(x * x, axis=-1, keepdims=True) + EPS) * g_ref[...]).astype(BF16)

    cos_t = cosT_ref[...]
    sin_t = sinT_ref[...]
    q_scale = math.log2(math.e) / math.sqrt(HEAD_DIM)
    acc_qt = _dot_nt(wqT_ref[...], hn)
    half = ROPE_AXIS_DIM // 2
    for h in range(N_HEADS):
        a = acc_qt[HEAD_DIM * h:HEAD_DIM * (h + 1)]
        r = lax.rsqrt(jnp.sum(a * a, axis=0, keepdims=True) * (1.0 / HEAD_DIM) + EPS)
        y = a * r * qg_ref[...]
        swapped = jnp.concatenate([y[half:2 * half], y[0:half], y[3 * half:4 * half], y[2 * half:3 * half]], axis=0)
        y = (y * cos_t + swapped * sin_t) * q_scale
        q_ref[0, h, 0:HEAD_DIM, :] = y.astype(BF16)
        q_ref[0, h, HEAD_DIM:, :] = jnp.zeros((LANES - HEAD_DIM, tm), BF16)

    cos = cos_ref[...]
    s1 = s1_ref[...]
    s2 = s2_ref[...]
    acc_k = _dot(hn, wk_ref[...])
    for g in range(N_KV):
        a = acc_k[:, LANES * g:LANES * (g + 1)]
        r = lax.rsqrt(jnp.sum(a * a, axis=-1, keepdims=True) * (1.0 / HEAD_DIM) + EPS)
        y = a * r * kg_ref[...]
        y = y * cos + pltpu.roll(y, LANES - 16, 1) * s1 + pltpu.roll(y, 16, 1) * s2
        k_ref[0, g] = y.astype(BF16)

    acc_vt = _dot_nt(wvT_ref[...], hn)
    ones_row = jnp.where(lax.broadcasted_iota(jnp.int32, (LANES - HEAD_DIM, tm), 0) == 0, 1.0, 0.0).astype(BF16)
    for g in range(N_KV):
        v_ref[0, g, 0:HEAD_DIM, :] = acc_vt[HEAD_DIM * g:HEAD_DIM * (g + 1)].astype(BF16)
        v_ref[0, g, HEAD_DIM:, :] = ones_row

    z_ref[0] = _dot(hn, wz_ref[...])
    xbc_ref[0] = _dot(hn, wxbc_ref[...])
    dt_ref[0] = _dot(hn, wdt_ref[...])
    dtT_ref[0] = _dot_nt(wdtT_ref[...], hn)


def _inproj(x, gain, wqT, wk, wvT, wz, wxbc, wdt, wdtT, qg, kg, tabs, tm):
    b, l, d = x.shape
    nt = l // tm
    const = lambda shape: pl.BlockSpec(shape, lambda bi, i: (0,) * len(shape))
    tab = pl.BlockSpec((tm, LANES), lambda bi, i: (i, 0))
    tab_t = pl.BlockSpec((HEAD_DIM, tm), lambda bi, i: (0, i))
    return pl.pallas_call(
        _inproj_kernel,
        grid=(b, nt),
        in_specs=[
            pl.BlockSpec((1, tm, d), lambda bi, i: (bi, i, 0)),
            const((1, d)), const(wqT.shape), const(wk.shape), const(wvT.shape), const(wz.shape),
            const(wxbc.shape), const(wdt.shape), const(wdtT.shape), const((HEAD_DIM, 1)), const((1, LANES)),
            tab, tab, tab, tab_t, tab_t,
        ],
        out_specs=[
            pl.BlockSpec((1, N_HEADS, LANES, tm), lambda bi, i: (bi, 0, 0, i)),
            pl.BlockSpec((1, N_KV, tm, LANES), lambda bi, i: (bi, 0, i, 0)),
            pl.BlockSpec((1, N_KV, LANES, tm), lambda bi, i: (bi, 0, 0, i)),
            pl.BlockSpec((1, tm, SSM_WIDTH), lambda bi, i: (bi, i, 0)),
            pl.BlockSpec((1, tm, XBC_WIDTH), lambda bi, i: (bi, i, 0)),
            pl.BlockSpec((1, tm, 2 * SSM_HEADS), lambda bi, i: (bi, i, 0)),
            pl.BlockSpec((1, 2 * SSM_HEADS, tm), lambda bi, i: (bi, 0, i)),
        ],
        out_shape=[
            jax.ShapeDtypeStruct((b, N_HEADS, LANES, l), BF16),
            jax.ShapeDtypeStruct((b, N_KV, l, LANES), BF16),
            jax.ShapeDtypeStruct((b, N_KV, LANES, l), BF16),
            jax.ShapeDtypeStruct((b, l, SSM_WIDTH), F32),
            jax.ShapeDtypeStruct((b, l, XBC_WIDTH), F32),
            jax.ShapeDtypeStruct((b, l, 2 * SSM_HEADS), F32),
            jax.ShapeDtypeStruct((b, 2 * SSM_HEADS, l), F32),
        ],
        compiler_params=_cparams(("parallel", "parallel")),
        name="inproj",
    )(x, gain, wqT, wk, wvT, wz, wxbc, wdt, wdtT, qg, kg, *tabs)


def _attn_kernel(q_ref, k_ref, v_ref, o_ref, qcat_ref, m_ref, acc_ref, *s_refs, tk):
    tq = q_ref.shape[3]
    seq = k_ref.shape[2]
    cols = HEADS_PER_KV * tq
    n_chunks = seq // tk
    for hh in range(HEADS_PER_KV):
        qcat_ref[:, hh * tq:(hh + 1) * tq] = q_ref[0, hh]
    m_ref[...] = jnp.full((1, cols), -jnp.inf, F32)
    acc_ref[...] = jnp.zeros((LANES, cols), F32)

    def scores(j, s_ref):
        off = pl.multiple_of(j * tk, tk)
        s_ref[...] = _dot(k_ref[0, 0, pl.ds(off, tk), :], qcat_ref[...])

    def accumulate(j, s_ref):
        off = pl.multiple_of(j * tk, tk)
        vc = v_ref[0, 0, :, pl.ds(off, tk)]
        s = s_ref[...]
        m_old = m_ref[...]
        m_new = jnp.maximum(m_old, jnp.max(s, axis=0, keepdims=True))
        p = jnp.exp2(s - m_new).astype(BF16)
        acc_ref[...] = jnp.exp2(m_old - m_new) * acc_ref[...] + _dot(vc, p)
        m_ref[...] = m_new

    nbuf = len(s_refs)
    scores(0, s_refs[0])

    def body(i, carry):
        j = nbuf * i
        for u in range(nbuf):
            scores(jnp.minimum(j + u + 1, n_chunks - 1), s_refs[(u + 1) % nbuf])
            accumulate(j + u, s_refs[u])
        return carry

    lax.fori_loop(0, n_chunks // nbuf, body, 0)
    acc = acc_ref[...]
    out = acc[0:HEAD_DIM] / acc[HEAD_DIM:HEAD_DIM + 1]
    out = jnp.concatenate([out, jnp.zeros((LANES - HEAD_DIM, cols), F32)], axis=0)
    for hh in range(HEADS_PER_KV):
        o_ref[0, hh] = out[:, hh * tq:(hh + 1) * tq].T


def _attention(q, k, v, tq, tk):
    b, _, _, l = q.shape
    cols = HEADS_PER_KV * tq
    return pl.pallas_call(
        functools.partial(_attn_kernel, tk=tk),
        grid=(b, N_KV, l // tq),
        in_specs=[
            pl.BlockSpec((1, HEADS_PER_KV, LANES, tq), lambda bi, g, i: (bi, g, 0, i)),
            pl.BlockSpec((1, 1, l, LANES), lambda bi, g, i: (bi, g, 0, 0)),
            pl.BlockSpec((1, 1, LANES, l), lambda bi, g, i: (bi, g, 0, 0)),
        ],
        out_specs=pl.BlockSpec((1, HEADS_PER_KV, tq, LANES), lambda bi, g, i: (bi, g, i, 0)),
        out_shape=jax.ShapeDtypeStruct((b, N_HEADS, l, LANES), F32),
        scratch_shapes=[
            pltpu.VMEM((LANES, cols), BF16),
            pltpu.VMEM((1, cols), F32),
            pltpu.VMEM((LANES, cols), F32),
        ] + [pltpu.VMEM((tk, cols), F32)] * min(ATTN_SCORE_BUFFERS, l // tk),
        compiler_params=_cparams(("parallel", "parallel", "parallel")),
        name="attn",
    )(q, k, v)


def _ssd_kernel(*refs, reverse):
    if reverse:
        (cur_ref, prev_ref, next_ref, dt_ref, dtT_ref, convw_ref, convb_ref, dtb_row_ref, dtb_col_ref,
         alog_row_ref, alog_col_ref, yin_ref, z_ref, dskip_ref, gain_ref,
         y_ref, pad_ref, state_ref) = refs
    else:
        (cur_ref, prev_ref, next_ref, dt_ref, dtT_ref, convw_ref, convb_ref, dtb_row_ref, dtb_col_ref,
         alog_row_ref, alog_col_ref, y_ref, pad_ref, state_ref) = refs
    c = pl.program_id(1)
    nc = pl.num_programs(1)
    chunk = (nc - 1 - c) if reverse else c

    @pl.when(c == 0)
    def _():
        state_ref[...] = jnp.zeros(state_ref.shape, F32)

    halo = SUBLANES
    pad_ref[0:halo, :] = jnp.where(chunk > 0, prev_ref[0], 0.0)
    pad_ref[halo:halo + CHUNK, :] = cur_ref[0]
    pad_ref[halo + CHUNK:, :] = jnp.where(chunk < nc - 1, next_ref[0], 0.0)
    padded = pad_ref[...]
    rows_padded = CHUNK + 2 * halo
    conv = jnp.broadcast_to(convb_ref[...], (CHUNK, XBC_WIDTH))
    for j in range(CONV_K):
        shifted = pltpu.roll(padded, (CONV_K // 2 - j) % rows_padded, 0) if j != CONV_K // 2 else padded
        conv = conv + convw_ref[j:j + 1, :] * shifted[halo:halo + CHUNK]
    xc = conv * _sigmoid(conv)
    xs = xc[:, :SSM_WIDTH]

    dt_col = _softplus(dt_ref[0, 0] + dtb_row_ref[0])
    dt_row = _softplus(dtT_ref[0, 0] + dtb_col_ref[0])
    da_col = dt_col * -jnp.exp(alog_row_ref[0])
    da_row = dt_row * -jnp.exp(alog_col_ref[0])

    ri = lax.broadcasted_iota(jnp.int32, (CHUNK, CHUNK), 0)
    ci = lax.broadcasted_iota(jnp.int32, (CHUNK, CHUNK), 1)
    low = ci <= ri
    up = ci >= ri
    low_m = jnp.where(low, 1.0, 0.0).astype(BF16)
    up_m = jnp.where(up, 1.0, 0.0).astype(BF16)
    col_m, row_m, mask = (up_m, low_m, up) if reverse else (low_m, up_m, low)
    ch, cm, cl = _split3(da_col)
    acum_col = _dot(col_m, ch) + _dot(col_m, cm) + _dot(col_m, cl)
    rh, rm, rl = _split3(da_row)
    acum_row = _dot(rh, row_m) + _dot(rm, row_m) + _dot(rl, row_m)
    end = 0 if reverse else CHUNK - 1
    a_end_row = acum_col[end:end + 1, :]
    w_col = dt_col * jnp.exp(a_end_row - acum_col)
    e_col = jnp.exp(acum_col)
    chunk_decay = jnp.exp(a_end_row)

    pieces = []
    for g in range(SSM_GROUPS):
        bm = xc[:, SSM_WIDTH + g * SSM_STATE:SSM_WIDTH + (g + 1) * SSM_STATE]
        cmat = xc[:, SSM_WIDTH + (SSM_GROUPS + g) * SSM_STATE:SSM_WIDTH + (SSM_GROUPS + g + 1) * SSM_STATE]
        bm16 = bm.astype(BF16)
        cm16 = cmat.astype(BF16)
        cb = _dot_nt(cm16, bm16)
        bt16 = bm.T.astype(BF16)
        for hh in range(SSM_HEADS_PER_GROUP):
            h = g * SSM_HEADS_PER_GROUP + hh
            xh = xs[:, h * SSM_HEAD_DIM:(h + 1) * SSM_HEAD_DIM]
            seg = acum_col[:, h:h + 1] - acum_row[h:h + 1, :]
            decay = jnp.exp(jnp.where(mask, seg, -jnp.inf))
            mat = cb * decay * dt_row[h:h + 1, :]
            y_diag = _dot(mat.astype(BF16), xh.astype(BF16))
            prev_state = state_ref[h]
            y_off = _dot(cm16, prev_state.astype(BF16)) * e_col[:, h:h + 1]
            pieces.append(y_diag + y_off)
            xw = xh * w_col[:, h:h + 1]
            state_ref[h] = prev_state * chunk_decay[:, h:h + 1] + _dot(bt16, xw.astype(BF16))
    y = jnp.concatenate(pieces, axis=1)

    if not reverse:
        y_ref[0] = y
    else:
        y = y + yin_ref[0] + xs * dskip_ref[...]
        zz = z_ref[0]
        y = y * (zz * _sigmoid(zz))
        gw = SSM_WIDTH // SSM_GROUPS
        outs = []
        for g in range(SSM_GROUPS):
            yg = y[:, g * gw:(g + 1) * gw]
            outs.append(yg * lax.rsqrt(jnp.mean(yg * yg, axis=-1, keepdims=True) + EPS))
        y_ref[0] = jnp.concatenate(outs, axis=1) * gain_ref[...]


def _ssd(xbc, dt, dtT, convw, convb, dtb_row, dtb_col, alog_row, alog_col, direction,
         y_fwd=None, z=None, dskip=None, gain=None):
    b, l, _ = xbc.shape
    nc = l // CHUNK
    reverse = direction == 1
    blocks_per_chunk = CHUNK // SUBLANES
    nblk = l // SUBLANES

    def pos(c):
        return (nc - 1 - c) if reverse else c

    const = lambda shape: pl.BlockSpec(shape, lambda bi, c: (0,) * len(shape))
    dsel = lambda shape: pl.BlockSpec(shape, lambda bi, c: (direction,) + (0,) * (len(shape) - 1))
    in_specs = [
        pl.BlockSpec((1, CHUNK, XBC_WIDTH), lambda bi, c: (bi, pos(c), 0)),
        pl.BlockSpec((1, SUBLANES, XBC_WIDTH),
                     lambda bi, c: (bi, jnp.maximum(pos(c) * blocks_per_chunk - 1, 0), 0)),
        pl.BlockSpec((1, SUBLANES, XBC_WIDTH),
                     lambda bi, c: (bi, jnp.minimum((pos(c) + 1) * blocks_per_chunk, nblk - 1), 0)),
        pl.BlockSpec((1, 1, CHUNK, SSM_HEADS), lambda bi, c: (direction, bi, pos(c), 0)),
        pl.BlockSpec((1, 1, SSM_HEADS, CHUNK), lambda bi, c: (direction, bi, 0, pos(c))),
        const((CONV_K, XBC_WIDTH)), const((1, XBC_WIDTH)),
        dsel((1, 1, SSM_HEADS)), dsel((1, SSM_HEADS, 1)), dsel((1, 1, SSM_HEADS)), dsel((1, SSM_HEADS, 1)),
    ]
    args = [xbc, xbc, xbc, dt, dtT, convw, convb, dtb_row, dtb_col, alog_row, alog_col]
    if reverse:
        in_specs += [
            pl.BlockSpec((1, CHUNK, SSM_WIDTH), lambda bi, c: (bi, pos(c), 0)),
            pl.BlockSpec((1, CHUNK, SSM_WIDTH), lambda bi, c: (bi, pos(c), 0)),
            const((1, SSM_WIDTH)), const((1, SSM_WIDTH)),
        ]
        args += [y_fwd, z, dskip, gain]
    return pl.pallas_call(
        functools.partial(_ssd_kernel, reverse=reverse),
        grid=(b, nc),
        in_specs=in_specs,
        out_specs=pl.BlockSpec((1, CHUNK, SSM_WIDTH), lambda bi, c: (bi, pos(c), 0)),
        out_shape=jax.ShapeDtypeStruct((b, l, SSM_WIDTH), F32),
        scratch_shapes=[
            pltpu.VMEM((CHUNK + 2 * SUBLANES, XBC_WIDTH), F32),
            pltpu.VMEM((SSM_HEADS, SSM_STATE, SSM_HEAD_DIM), F32),
        ],
        compiler_params=_cparams(("parallel", "arbitrary")),
        name="ssd_bwd" if reverse else "ssd_fwd",
    )(*args)


def _outproj_kernel(attn_ref, ssm_ref, x_ref, again_ref, wa_ref, ws_ref, fgain_ref, wr_hi_ref, wr_lo_ref,
                    x1_ref, hn_ref, logit_ref):
    a = jnp.concatenate([attn_ref[0, h] for h in range(N_HEADS)], axis=1)
    r = lax.rsqrt(jnp.sum(a * a, axis=-1, keepdims=True) * (1.0 / ATTN_WIDTH) + EPS)
    an = (a * r * again_ref[...]).astype(BF16)
    x1 = x_ref[...] + _dot(an, wa_ref[...]) + _dot(ssm_ref[...].astype(BF16), ws_ref[...])
    x1_ref[...] = x1
    hn = x1 * lax.rsqrt(jnp.mean(x1 * x1, axis=-1, keepdims=True) + EPS) * fgain_ref[...]
    hn_ref[...] = hn
    h_hi = hn.astype(BF16)
    h_lo = (hn - h_hi.astype(F32)).astype(BF16)
    w_hi = wr_hi_ref[...]
    logit_ref[...] = _dot_nt(w_hi, h_hi) + _dot_nt(w_hi, h_lo) + _dot_nt(wr_lo_ref[...], h_hi)


def _outproj(attn, ssm, x, again, wa, ws, fgain, wr_hi, wr_lo, tm):
    b, _, l, _ = attn.shape
    t = b * l
    nt = l // tm
    const = lambda shape: pl.BlockSpec(shape, lambda i: (0,) * len(shape))
    return pl.pallas_call(
        _outproj_kernel,
        grid=(t // tm,),
        in_specs=[
            pl.BlockSpec((1, N_HEADS, tm, LANES), lambda i: (i // nt, 0, i % nt, 0)),
            pl.BlockSpec((tm, SSM_WIDTH), lambda i: (i, 0)),
            pl.BlockSpec((tm, D_MODEL), lambda i: (i, 0)),
            const((1, Q_PAD)), const(wa.shape), const(ws.shape), const((1, D_MODEL)),
            const(wr_hi.shape), const(wr_lo.shape),
        ],
        out_specs=[
            pl.BlockSpec((tm, D_MODEL), lambda i: (i, 0)),
            pl.BlockSpec((tm, D_MODEL), lambda i: (i, 0)),
            pl.BlockSpec((N_EXPERTS, tm), lambda i: (0, i)),
        ],
        out_shape=[
            jax.ShapeDtypeStruct((t, D_MODEL), F32),
            jax.ShapeDtypeStruct((t, D_MODEL), F32),
            jax.ShapeDtypeStruct((N_EXPERTS, t), F32),
        ],
        compiler_params=_cparams(("parallel",)),
        name="outproj",
    )(attn, ssm, x, again, wa, ws, fgain, wr_hi, wr_lo)


def _route_kernel(logit_ref, bias_ref, idx_ref, w_ref, rank_ref, count_ref, tile_ref, carry_ref):
    i = pl.program_id(0)
    tn = logit_ref.shape[1]

    @pl.when(i == 0)
    def _():
        carry_ref[...] = jnp.zeros(carry_ref.shape, F32)

    lg = logit_ref[...] + bias_ref[...]
    eid = lax.broadcasted_iota(jnp.int32, (N_EXPERTS, tn), 0).astype(F32)
    chosen = jnp.zeros((N_EXPERTS, tn), F32)
    vals, ids = [], []
    for _ in range(TOP_K):
        mx = jnp.max(lg, axis=0, keepdims=True)
        sel = jnp.min(jnp.where(lg == mx, eid, float(N_EXPERTS)), axis=0, keepdims=True)
        hit = eid == sel
        chosen = jnp.where(hit, 1.0, chosen)
        lg = jnp.where(hit, -jnp.inf, lg)
        vals.append(mx)
        ids.append(sel)
    ex = [jnp.exp(v - vals[0]) for v in vals]
    den = ex[0] + ex[1] + ex[2] + ex[3]
    ri = lax.broadcasted_iota(jnp.int32, (tn, tn), 0)
    ci = lax.broadcasted_iota(jnp.int32, (tn, tn), 1)
    upper = jnp.where(ri <= ci, 1.0, 0.0).astype(BF16)
    incl = _dot(chosen.astype(BF16), upper)
    rank_e = incl - chosen + carry_ref[...]
    carry_ref[...] = carry_ref[...] + incl[:, tn - 1:tn]
    for k in range(TOP_K):
        idx_ref[k:k + 1, :] = ids[k].astype(jnp.int32)
        w_ref[k:k + 1, :] = ex[k] / den
        rank_ref[k:k + 1, :] = jnp.sum(jnp.where(eid == ids[k], rank_e, 0.0), axis=0,
                                       keepdims=True).astype(jnp.int32)
    count_ref[...] = carry_ref[...].astype(jnp.int32)
    tile_ref[0] = incl[:, tn - 1:tn].astype(jnp.int32)


def _route(logits_t, bias_col, tn):
    t = logits_t.shape[1]
    tok = pl.BlockSpec((TOP_K, tn), lambda i: (0, i))
    return pl.pallas_call(
        _route_kernel,
        grid=(t // tn,),
        in_specs=[pl.BlockSpec((N_EXPERTS, tn), lambda i: (0, i)),
                  pl.BlockSpec((N_EXPERTS, 1), lambda i: (0, 0))],
        out_specs=[tok, tok, tok, pl.BlockSpec((N_EXPERTS, 1), lambda i: (0, 0)),
                   pl.BlockSpec((1, N_EXPERTS, 1), lambda i: (i, 0, 0))],
        out_shape=[
            jax.ShapeDtypeStruct((TOP_K, t), jnp.int32),
            jax.ShapeDtypeStruct((TOP_K, t), F32),
            jax.ShapeDtypeStruct((TOP_K, t), jnp.int32),
            jax.ShapeDtypeStruct((N_EXPERTS, 1), jnp.int32),
            jax.ShapeDtypeStruct((t // tn, N_EXPERTS, 1), jnp.int32),
        ],
        scratch_shapes=[pltpu.VMEM((N_EXPERTS, 1), F32)],
        compiler_params=_cparams(("arbitrary",)),
        name="route",
    )(logits_t, bias_col)


RUN_ALIGN = SUBLANES
RUN_BITS = (256, 128, 64, 32, 16, 8)


def _copy_run(src_ref, dst_ref, src_start, dst_start, count, sem, start=True, wait=True):
    for bit in RUN_BITS:
        take = count & bit

        @pl.when(take != 0)
        def _():
            cp = pltpu.make_async_copy(src_ref.at[pl.ds(pl.multiple_of(src_start, RUN_ALIGN), bit), :],
                                       dst_ref.at[pl.ds(pl.multiple_of(dst_start, RUN_ALIGN), bit), :], sem)
            if start:
                cp.start()
            if wait:
                cp.wait()

        src_start = src_start + take
        dst_start = dst_start + take


def _tile_positions_rows(idx_ref, off_col, tri_ref):
    tn = idx_ref.shape[1]
    eid = lax.broadcasted_iota(jnp.int32, (N_EXPERTS, tn), 0)
    hits = [eid == idx_ref[k:k + 1, :] for k in range(TOP_K)]
    chosen = jnp.where(hits[0] | hits[1] | hits[2] | hits[3], 1.0, 0.0)
    incl = _dot(chosen.astype(BF16), tri_ref[...])
    slot = off_col + incl - chosen
    return [jnp.sum(jnp.where(h, slot, 0.0), axis=0, keepdims=True).astype(jnp.int32) for h in hits]


def _dispatch_kernel(seg_lo_ref, seg_hi_ref, tab_ref, tab_prev_ref, idx_ref, off_ref, tri_ref, hn_ref, xs_ref,
                     sorted_ref, zero_ref, sems, sem, *, bm):
    i = pl.program_id(0)
    last = pl.num_programs(0) - 1
    slot = i % 2
    tn = idx_ref.shape[1]
    n_sorted = sorted_ref.shape[1]

    @pl.when(i == 0)
    def _():
        zero_ref[...] = jnp.zeros(zero_ref.shape, F32)

        def segment(e, carry):
            lo = seg_lo_ref[e]
            count = seg_hi_ref[e] - lo

            def whole(j, c):
                cp = pltpu.make_async_copy(zero_ref, xs_ref.at[pl.ds(pl.multiple_of(lo + j * bm, RUN_ALIGN), bm), :], sem)
                cp.start()
                cp.wait()
                return c

            lax.fori_loop(0, count // bm, whole, 0)
            _copy_run(zero_ref, xs_ref, 0, lo + count // bm * bm, count % bm, sem)
            return carry

        lax.fori_loop(0, N_EXPERTS + 1, segment, 0)

    lpos = _tile_positions_rows(idx_ref, off_ref[0].astype(F32), tri_ref)
    pos = lax.broadcasted_iota(jnp.int32, (n_sorted, tn), 0)
    onehot = (pos == lpos[0]) | (pos == lpos[1]) | (pos == lpos[2]) | (pos == lpos[3])
    sorted_ref[slot] = _dot(jnp.where(onehot, 1.0, 0.0).astype(BF16), hn_ref[...].astype(BF16))

    def runs(tab, buf, start):
        def body(e, carry):
            _copy_run(sorted_ref.at[buf], xs_ref, tab[0, 0, e], tab[0, 0, 2 * N_EXPERTS + e],
                      tab[0, 0, N_EXPERTS + e], sems.at[buf], start=start, wait=not start)
            return carry

        lax.fori_loop(0, N_EXPERTS, body, 0)

    runs(tab_ref, slot, True)

    @pl.when(i > 0)
    def _():
        runs(tab_prev_ref, 1 - slot, False)

    @pl.when(i == last)
    def _():
        runs(tab_ref, slot, False)


def _dispatch(seg_lo, seg_hi, tab, idx_t, off_col, tri, hn, n_rows, n_sorted, bm, tn):
    t, d = hn.shape
    return pl.pallas_call(
        functools.partial(_dispatch_kernel, bm=bm),
        grid_spec=pltpu.PrefetchScalarGridSpec(
            num_scalar_prefetch=2,
            grid=(t // tn,),
            in_specs=[
                pl.BlockSpec((1, 1, 3 * N_EXPERTS), lambda i, lo, hi: (i, 0, 0), memory_space=pltpu.SMEM),
                pl.BlockSpec((1, 1, 3 * N_EXPERTS), lambda i, lo, hi: (jnp.maximum(i - 1, 0), 0, 0),
                             memory_space=pltpu.SMEM),
                pl.BlockSpec((TOP_K, tn), lambda i, lo, hi: (0, i)),
                pl.BlockSpec((1, N_EXPERTS, 1), lambda i, lo, hi: (i, 0, 0)),
                pl.BlockSpec((tn, tn), lambda i, lo, hi: (0, 0)),
                pl.BlockSpec((tn, d), lambda i, lo, hi: (i, 0)),
            ],
            out_specs=pl.BlockSpec(memory_space=pl.ANY),
            scratch_shapes=[pltpu.VMEM((2, n_sorted, d), F32), pltpu.VMEM((bm, d), F32),
                            pltpu.SemaphoreType.DMA((2,)), pltpu.SemaphoreType.DMA(())],
        ),
        out_shape=jax.ShapeDtypeStruct((n_rows, d), F32),
        compiler_params=_cparams(("arbitrary",)),
        name="dispatch",
    )(seg_lo, seg_hi, tab, tab, idx_t, off_col, tri, hn)


def _expert_kernel(block_e_ref, nused_ref, xs_ref, wg_ref, bg_ref, wu_ref, bu_ref, wd_ref, bd_ref,
                   ys_ref, wg16, wu16, wd16):
    b = pl.program_id(0)
    used = b < nused_ref[0]
    prev_e = block_e_ref[jnp.maximum(b - 1, 0)]
    fresh = jnp.logical_or(b == 0, block_e_ref[b] != prev_e)

    @pl.when(jnp.logical_and(used, fresh))
    def _():
        wg16[...] = wg_ref[0].astype(BF16)
        wu16[...] = wu_ref[0].astype(BF16)
        wd16[...] = wd_ref[0].astype(BF16)

    @pl.when(used)
    def _():
        x = xs_ref[...].astype(BF16)
        g = _dot(x, wg16[...]) + bg_ref[0]
        u = _dot(x, wu16[...]) + bu_ref[0]
        g = jnp.minimum(g, SWIGLU_LIMIT)
        u = jnp.clip(u, -SWIGLU_LIMIT, SWIGLU_LIMIT)
        act = g * _sigmoid(SWIGLU_ALPHA * g) * (u + 1.0)
        ys_ref[...] = _dot(act.astype(BF16), wd16[...]) + bd_ref[0]

    @pl.when(jnp.logical_not(used))
    def _():
        ys_ref[...] = jnp.zeros(ys_ref.shape, F32)


def _experts(block_e, nused, xs, wg, bg, wu, bu, wd, bd, bm, layer):
    n_rows, d = xs.shape
    nb = n_rows // bm
    base = layer * N_EXPERTS
    wspec = pl.BlockSpec((1, d, d), lambda i, be, nu: (base + be[i], 0, 0))
    bspec = pl.BlockSpec((1, 1, d), lambda i, be, nu: (base + be[i], 0, 0))
    row = pl.BlockSpec((bm, d), lambda i, be, nu: (i, 0))
    return pl.pallas_call(
        _expert_kernel,
        grid_spec=pltpu.PrefetchScalarGridSpec(
            num_scalar_prefetch=2,
            grid=(nb,),
            in_specs=[row, wspec, bspec, wspec, bspec, wspec, bspec],
            out_specs=row,
            scratch_shapes=[pltpu.VMEM((d, d), BF16)] * 3,
        ),
        out_shape=jax.ShapeDtypeStruct((n_rows, d), F32),
        compiler_params=_cparams(("arbitrary",)),
        name="experts",
    )(block_e, nused, xs, wg, bg, wu, bu, wd, bd)


def _combine_kernel(tab_ref, tab_next_ref, idx_ref, w_ref, off_ref, tri_ref, x1_ref, ys_ref, out_ref,
                    sorted_ref, sems):
    i = pl.program_id(0)
    last = pl.num_programs(0) - 1
    cur = i % 2
    tn = x1_ref.shape[0]
    n_sorted = sorted_ref.shape[1]

    def runs(tab, buf, start):
        def body(e, carry):
            _copy_run(ys_ref, sorted_ref.at[buf], tab[0, 0, 2 * N_EXPERTS + e], tab[0, 0, e],
                      tab[0, 0, N_EXPERTS + e], sems.at[buf], start=start, wait=not start)
            return carry

        lax.fori_loop(0, N_EXPERTS, body, 0)

    @pl.when(i == 0)
    def _():
        runs(tab_ref, cur, True)

    @pl.when(i < last)
    def _():
        runs(tab_next_ref, 1 - cur, True)

    idx = idx_ref[...]
    eid = lax.broadcasted_iota(jnp.int32, (tn, N_EXPERTS), 1)
    hits = [eid == idx[:, k:k + 1] for k in range(TOP_K)]
    chosen = jnp.where(hits[0] | hits[1] | hits[2] | hits[3], 1.0, 0.0)
    incl = _dot(tri_ref[...], chosen.astype(BF16))
    slot = off_ref[0].astype(F32) + incl - chosen
    w = w_ref[...]
    pos = lax.broadcasted_iota(jnp.int32, (tn, n_sorted), 1)
    weights = jnp.zeros((tn, n_sorted), F32)
    for k in range(TOP_K):
        lpos = jnp.sum(jnp.where(hits[k], slot, 0.0), axis=1, keepdims=True).astype(jnp.int32)
        weights = jnp.where(pos == lpos, w[:, k:k + 1], weights)
    w_hi = weights.astype(BF16)
    w_lo = (weights - w_hi.astype(F32)).astype(BF16)

    runs(tab_ref, cur, False)
    row_id = lax.broadcasted_iota(jnp.int32, (n_sorted, 1), 0)
    y = jnp.where(row_id < tab_ref[0, 0, N_EXPERTS - 1] + tab_ref[0, 0, 2 * N_EXPERTS - 1], sorted_ref[cur], 0.0)
    y_hi = y.astype(BF16)
    y_lo = (y - y_hi.astype(F32)).astype(BF16)
    out_ref[...] = x1_ref[...] + _dot(w_hi, y_hi) + _dot(w_hi, y_lo) + _dot(w_lo, y_hi)


def _combine(tab, idx_tok, w_tok, off_row, tri, x1, ys, n_sorted, tn):
    t, d = x1.shape
    return pl.pallas_call(
        _combine_kernel,
        grid=(t // tn,),
        in_specs=[
            pl.BlockSpec((1, 1, 3 * N_EXPERTS), lambda i: (i, 0, 0), memory_space=pltpu.SMEM),
            pl.BlockSpec((1, 1, 3 * N_EXPERTS), lambda i: (jnp.minimum(i + 1, t // tn - 1), 0, 0),
                         memory_space=pltpu.SMEM),
            pl.BlockSpec((tn, TOP_K), lambda i: (i, 0)),
            pl.BlockSpec((tn, TOP_K), lambda i: (i, 0)),
            pl.BlockSpec((1, 1, N_EXPERTS), lambda i: (i, 0, 0)),
            pl.BlockSpec((tn, tn), lambda i: (0, 0)),
            pl.BlockSpec((tn, d), lambda i: (i, 0)),
            pl.BlockSpec(memory_space=pl.ANY),
        ],
        out_specs=pl.BlockSpec((tn, d), lambda i: (i, 0)),
        out_shape=jax.ShapeDtypeStruct((t, d), F32),
        scratch_shapes=[pltpu.VMEM((2, n_sorted, d), F32), pltpu.SemaphoreType.DMA((2,))],
        compiler_params=_cparams(("arbitrary",)),
        name="combine",
    )(tab, tab, idx_tok, w_tok, off_row, tri, x1, ys)


def _pad_heads(w, n_heads):
    d = w.shape[0]
    w = w.reshape(d, n_heads, HEAD_DIM)
    return jnp.pad(w, ((0, 0), (0, 0), (0, LANES - HEAD_DIM))).reshape(d, n_heads * LANES)


def _rope_tables(seq_len):
    rows = seq_len // GRID_W
    inv_freq = ROPE_THETA ** (-jnp.arange(0, ROPE_AXIS_DIM, 2, dtype=F32) / ROPE_AXIS_DIM)
    ang_r = jnp.arange(rows, dtype=F32)[:, None] * inv_freq
    ang_c = jnp.arange(GRID_W, dtype=F32)[:, None] * inv_freq
    expand_r = lambda a: jnp.repeat(a, GRID_W, axis=0)
    expand_c = lambda a: jnp.tile(a, (rows, 1))
    cr, sr = expand_r(jnp.cos(ang_r)), expand_r(jnp.sin(ang_r))
    cc, sc = expand_c(jnp.cos(ang_c)), expand_c(jnp.sin(ang_c))
    zeros = jnp.zeros_like(cr)
    pad = jnp.zeros((seq_len, LANES - HEAD_DIM), F32)
    cos = jnp.concatenate([cr, cr, cc, cc, pad], axis=1)
    s_up = jnp.concatenate([-sr, zeros, -sc, zeros, pad], axis=1)
    s_dn = jnp.concatenate([zeros, sr, zeros, sc, pad], axis=1)
    cos_t = jnp.concatenate([cr, cr, cc, cc], axis=1).T
    sin_t = jnp.concatenate([-sr, sr, -sc, sc], axis=1).T
    return cos, s_up, s_dn, cos_t, sin_t


def _tiles(seq_len, n_tok):
    return dict(
        tm_in=min(512, seq_len),
        tq=min(256, seq_len),
        tk=min(512, seq_len // 2),
        tm_out=min(512, seq_len),
        tn_route=min(256, n_tok),
        bm=256,
    )


def _layer(x, p, i, tabs, tl):
    b, l, d = x.shape
    t = b * l
    w_in = p["w_in"][i]
    o_q, o_k, o_v, o_z, o_xbc = ATTN_WIDTH, ATTN_WIDTH + KV_WIDTH, ATTN_WIDTH + 2 * KV_WIDTH, \
        ATTN_WIDTH + 2 * KV_WIDTH + SSM_WIDTH, ATTN_WIDTH + 2 * KV_WIDTH + SSM_WIDTH + XBC_WIDTH
    wq_t = w_in[:, :o_q].T.astype(BF16)
    wk = _pad_heads(w_in[:, o_q:o_k], N_KV).astype(BF16)
    wv_t = w_in[:, o_k:o_v].T.astype(BF16)
    wz = w_in[:, o_v:o_z].astype(BF16)
    wxbc = w_in[:, o_z:o_xbc].astype(BF16)
    wdt = w_in[:, o_xbc:].astype(BF16)
    q4, k4, v4, z, xbc, dt, dt_t = _inproj(
        x, p["norm_mix"][i].reshape(1, d), wq_t, wk, wv_t, wz, wxbc, wdt, wdt.T,
        p["q_norm"][i].reshape(HEAD_DIM, 1), jnp.pad(p["k_norm"][i], (0, LANES - HEAD_DIM)).reshape(1, LANES),
        tabs, tl["tm_in"])

    attn = _attention(q4, k4, v4, tl["tq"], tl["tk"])

    dt_dir = dt.reshape(b, l, 2, SSM_HEADS).transpose(2, 0, 1, 3)
    dtt_dir = dt_t.reshape(b, 2, SSM_HEADS, l).transpose(1, 0, 2, 3)
    ssd_common = (xbc, dt_dir, dtt_dir, p["conv_w"][i], p["conv_b"][i].reshape(1, XBC_WIDTH),
                  p["dt_bias"][i].reshape(2, 1, SSM_HEADS), p["dt_bias"][i].reshape(2, SSM_HEADS, 1),
                  p["a_log"][i].reshape(2, 1, SSM_HEADS), p["a_log"][i].reshape(2, SSM_HEADS, 1))
    y_fwd = _ssd(*ssd_common, 0)
    ssm = _ssd(*ssd_common, 1, y_fwd=y_fwd, z=z,
               dskip=jnp.repeat(p["d_skip"][i], SSM_HEAD_DIM).reshape(1, SSM_WIDTH),
               gain=p["ssm_norm"][i].reshape(1, SSM_WIDTH))

    w_out = p["w_out"][i]
    wa = jnp.pad(w_out[:ATTN_WIDTH].reshape(N_HEADS, HEAD_DIM, d),
                 ((0, 0), (0, LANES - HEAD_DIM), (0, 0))).reshape(Q_PAD, d).astype(BF16)
    ws = w_out[ATTN_WIDTH:].astype(BF16)
    again = jnp.pad(p["attn_norm"][i].reshape(N_HEADS, HEAD_DIM),
                    ((0, 0), (0, LANES - HEAD_DIM))).reshape(1, Q_PAD)
    wr_t = p["w_router"][i].T
    wr_hi = wr_t.astype(BF16)
    wr_lo = (wr_t - wr_hi.astype(F32)).astype(BF16)
    x1, hn, logits_t = _outproj(attn, ssm.reshape(t, SSM_WIDTH), x.reshape(t, d), again, wa, ws,
                                p["norm_ffn"][i].reshape(1, d), wr_hi, wr_lo, tl["tm_out"])

    tn = tl["tn_route"]
    idx_t, w_t, rank_t, counts, tile_counts = _route(logits_t, p["b_router"][i].reshape(N_EXPERTS, 1), tn)
    del rank_t

    bm = tl["bm"]
    nt = t // tn
    del counts
    tile_counts = tile_counts.reshape(nt, N_EXPERTS)
    run_len = (tile_counts + RUN_ALIGN - 1) // RUN_ALIGN * RUN_ALIGN
    run_off = jnp.cumsum(run_len, axis=1) - run_len
    rows_e = jnp.sum(run_len, axis=0)
    padded = (rows_e + bm - 1) // bm * bm
    pad_end = jnp.cumsum(padded)
    pad_start = pad_end - padded
    run_row = pad_start[None, :] + jnp.cumsum(run_len, axis=0) - run_len
    n_sorted = TOP_K * tn + N_EXPERTS * RUN_ALIGN
    n_rows = t * TOP_K + nt * N_EXPERTS * RUN_ALIGN + N_EXPERTS * bm
    nb = n_rows // bm
    block_row = jnp.arange(nb, dtype=jnp.int32) * bm
    block_e = jnp.minimum(jnp.sum(pad_end[None, :] <= block_row[:, None], axis=1), N_EXPERTS - 1).astype(jnp.int32)
    nused = (pad_end[-1:] // bm).astype(jnp.int32)
    seg_lo = jnp.concatenate([pad_start + rows_e, pad_end[-1:]]).astype(jnp.int32)
    seg_hi = jnp.concatenate([pad_end, jnp.full((1,), n_rows, jnp.int32)]).astype(jnp.int32)
    tab = jnp.concatenate([run_off, run_len, run_row], axis=1).astype(jnp.int32).reshape(nt, 1, 3 * N_EXPERTS)
    ones = jnp.ones((tn, tn), BF16)

    xs = _dispatch(seg_lo, seg_hi, tab, idx_t, run_off.reshape(nt, N_EXPERTS, 1).astype(jnp.int32),
                   jnp.triu(ones), hn, n_rows, n_sorted, bm, tn)
    n_all = p["w_gate"].shape[0] * N_EXPERTS
    ys = _experts(block_e, nused, xs,
                  p["w_gate"].reshape(n_all, d, d), p["b_gate"].reshape(n_all, 1, d),
                  p["w_up"].reshape(n_all, d, d), p["b_up"].reshape(n_all, 1, d),
                  p["w_down"].reshape(n_all, d, d), p["b_down"].reshape(n_all, 1, d), bm, i)
    x2 = _combine(tab, idx_t.T, w_t.T, run_off.reshape(nt, 1, N_EXPERTS).astype(jnp.int32),
                  jnp.tril(ones), x1, ys, n_sorted, tn)
    return x2.reshape(b, l, d)


_PARAM_NAMES = ("norm_mix", "w_in", "q_norm", "k_norm", "conv_w", "conv_b", "dt_bias", "a_log", "d_skip",
                "ssm_norm", "attn_norm", "w_out", "norm_ffn", "w_router", "b_router", "w_gate", "b_gate",
                "w_up", "b_up", "w_down", "b_down")


def kernel(x, norm_mix, w_in, q_norm, k_norm, conv_w, conv_b, dt_bias, a_log, d_skip, ssm_norm, attn_norm,
           w_out, norm_ffn, w_router, b_router, w_gate, b_gate, w_up, b_up, w_down, b_down):
    params = dict(zip(_PARAM_NAMES, (norm_mix, w_in, q_norm, k_norm, conv_w, conv_b, dt_bias, a_log, d_skip,
                                     ssm_norm, attn_norm, w_out, norm_ffn, w_router, b_router, w_gate, b_gate,
                                     w_up, b_up, w_down, b_down)))
    b, l, _ = x.shape
    tabs = _rope_tables(l)
    tl = _tiles(l, b * l)
    for i in range(norm_mix.shape[0]):
        x = _layer(x, params, i, tabs, tl)
    return x
```

```python
import functools
import math

import jax
import jax.numpy as jnp
from jax import lax
from jax.experimental import pallas as pl
from jax.experimental.pallas import tpu as pltpu

F32 = jnp.float32
BF16 = jnp.bfloat16

D_MODEL = 1024
GRID_W = 64
HEAD_DIM = 64
N_HEADS = 8
N_KV = 2
HEADS_PER_KV = N_HEADS // N_KV
ATTN_WIDTH = N_HEADS * HEAD_DIM
KV_WIDTH = N_KV * HEAD_DIM
SSM_WIDTH = 512
SSM_HEADS = 8
SSM_HEAD_DIM = 64
SSM_GROUPS = 2
SSM_HEADS_PER_GROUP = SSM_HEADS // SSM_GROUPS
SSM_STATE = 128
CHUNK = 128
CONV_K = 5
XBC_WIDTH = SSM_WIDTH + 2 * SSM_GROUPS * SSM_STATE
ROPE_THETA = 10000.0
ROPE_AXIS_DIM = HEAD_DIM // 2
N_EXPERTS = 32
TOP_K = 4
SWIGLU_LIMIT = 7.0
SWIGLU_ALPHA = 1.702
EPS = 1e-6

LANES = 128
SUBLANES = 8
VMEM_LIMIT = 56 * 1024 * 1024

Q_PAD = N_HEADS * LANES
ATTN_SCORE_BUFFERS = 2


def _cparams(sem):
    return pltpu.CompilerParams(dimension_semantics=sem, vmem_limit_bytes=VMEM_LIMIT)


def _dot(a, b):
    return jnp.dot(a, b, preferred_element_type=F32)


def _dot_nt(a, b):
    return lax.dot_general(a, b, (((1,), (1,)), ((), ())), preferred_element_type=F32)


def _split3(x):
    hi = x.astype(BF16)
    r1 = x - hi.astype(F32)
    mid = r1.astype(BF16)
    lo = (r1 - mid.astype(F32)).astype(BF16)
    return hi, mid, lo


def _sigmoid(x):
    return 1.0 / (1.0 + jnp.exp(-x))


def _softplus(x):
    return jnp.maximum(x, 0.0) + jnp.log(1.0 + jnp.exp(-jnp.abs(x)))


def _inproj_kernel(x_ref, g_ref, wqT_ref, wk_ref, wvT_ref, wz_ref, wxbc_ref, wdt_ref, wdtT_ref,
                   qg_ref, kg_ref, cos_ref, s1_ref, s2_ref, cosT_ref, sinT_ref,
                   q_ref, k_ref, v_ref, z_ref, xbc_ref, dt_ref, dtT_ref):
    x = x_ref[0]
    tm = x.shape[0]
    hn = (x * lax.rsqrt(jnp.mean(x * x, axis=-1, keepdims=True) + EPS) * g_ref[...]).astype(BF16)

    cos_t = cosT_ref[...]
    sin_t = sinT_ref[...]
    q_scale = math.log2(math.e) / math.sqrt(HEAD_DIM)
    acc_qt = _dot_nt(wqT_ref[...], hn)
    half = ROPE_AXIS_DIM // 2
    for h in range(N_HEADS):
        a = acc_qt[HEAD_DIM * h:HEAD_DIM * (h + 1)]
        r = lax.rsqrt(jnp.sum(a * a, axis=0, keepdims=True) * (1.0 / HEAD_DIM) + EPS)
        y = a * r * qg_ref[...]
        swapped = jnp.concatenate([y[half:2 * half], y[0:half], y[3 * half:4 * half], y[2 * half:3 * half]], axis=0)
        y = (y * cos_t + swapped * sin_t) * q_scale
        q_ref[0, h, 0:HEAD_DIM, :] = y.astype(BF16)
        q_ref[0, h, HEAD_DIM:, :] = jnp.zeros((LANES - HEAD_DIM, tm), BF16)

    cos = cos_ref[...]
    s1 = s1_ref[...]
    s2 = s2_ref[...]
    acc_k = _dot(hn, wk_ref[...])
    for g in range(N_KV):
        a = acc_k[:, LANES * g:LANES * (g + 1)]
        r = lax.rsqrt(jnp.sum(a * a, axis=-1, keepdims=True) * (1.0 / HEAD_DIM) + EPS)
        y = a * r * kg_ref[...]
        y = y * cos + pltpu.roll(y, LANES - 16, 1) * s1 + pltpu.roll(y, 16, 1) * s2
        k_ref[0, g] = y.astype(BF16)

    acc_vt = _dot_nt(wvT_ref[...], hn)
    ones_row = jnp.where(lax.broadcasted_iota(jnp.int32, (LANES - HEAD_DIM, tm), 0) == 0, 1.0, 0.0).astype(BF16)
    for g in range(N_KV):
        v_ref[0, g, 0:HEAD_DIM, :] = acc_vt[HEAD_DIM * g:HEAD_DIM * (g + 1)].astype(BF16)
        v_ref[0, g, HEAD_DIM:, :] = ones_row

    z_ref[0] = _dot(hn, wz_ref[...])
    xbc_ref[0] = _dot(hn, wxbc_ref[...])
    dt_ref[0] = _dot(hn, wdt_ref[...])
    dtT_ref[0] = _dot_nt(wdtT_ref[...], hn)


def _inproj(x, gain, wqT, wk, wvT, wz, wxbc, wdt, wdtT, qg, kg, tabs, tm):
    b, l, d = x.shape
    nt = l // tm
    const = lambda shape: pl.BlockSpec(shape, lambda bi, i: (0,) * len(shape))
    tab = pl.BlockSpec((tm, LANES), lambda bi, i: (i, 0))
    tab_t = pl.BlockSpec((HEAD_DIM, tm), lambda bi, i: (0, i))
    return pl.pallas_call(
        _inproj_kernel,
        grid=(b, nt),
        in_specs=[
            pl.BlockSpec((1, tm, d), lambda bi, i: (bi, i, 0)),
            const((1, d)), const(wqT.shape), const(wk.shape), const(wvT.shape), const(wz.shape),
            const(wxbc.shape), const(wdt.shape), const(wdtT.shape), const((HEAD_DIM, 1)), const((1, LANES)),
            tab, tab, tab, tab_t, tab_t,
        ],
        out_specs=[
            pl.BlockSpec((1, N_HEADS, LANES, tm), lambda bi, i: (bi, 0, 0, i)),
            pl.BlockSpec((1, N_KV, tm, LANES), lambda bi, i: (bi, 0, i, 0)),
            pl.BlockSpec((1, N_KV, LANES, tm), lambda bi, i: (bi, 0, 0, i)),
            pl.BlockSpec((1, tm, SSM_WIDTH), lambda bi, i: (bi, i, 0)),
            pl.BlockSpec((1, tm, XBC_WIDTH), lambda bi, i: (bi, i, 0)),
            pl.BlockSpec((1, tm, 2 * SSM_HEADS), lambda bi, i: (bi, i, 0)),
            pl.BlockSpec((1, 2 * SSM_HEADS, tm), lambda bi, i: (bi, 0, i)),
        ],
        out_shape=[
            jax.ShapeDtypeStruct((b, N_HEADS, LANES, l), BF16),
            jax.ShapeDtypeStruct((b, N_KV, l, LANES), BF16),
            jax.ShapeDtypeStruct((b, N_KV, LANES, l), BF16),
            jax.ShapeDtypeStruct((b, l, SSM_WIDTH), F32),
            jax.ShapeDtypeStruct((b, l, XBC_WIDTH), F32),
            jax.ShapeDtypeStruct((b, l, 2 * SSM_HEADS), F32),
            jax.ShapeDtypeStruct((b, 2 * SSM_HEADS, l), F32),
        ],
        compiler_params=_cparams(("parallel", "parallel")),
        name="inproj",
    )(x, gain, wqT, wk, wvT, wz, wxbc, wdt, wdtT, qg, kg, *tabs)


def _attn_kernel(q_ref, k_ref, v_ref, o_ref, qcat_ref, m_ref, acc_ref, *bufs, tk):
    tq = q_ref.shape[3]
    seq = k_ref.shape[2]
    cols = HEADS_PER_KV * tq
    n_chunks = seq // tk
    for hh in range(HEADS_PER_KV):
        qcat_ref[:, hh * tq:(hh + 1) * tq] = q_ref[0, hh]
    m_ref[...] = jnp.full((1, cols), -jnp.inf, F32)
    acc_ref[...] = jnp.zeros((LANES, cols), F32)

    s_refs, top_refs = bufs[:len(bufs) // 2], bufs[len(bufs) // 2:]

    def scores(j, u):
        off = pl.multiple_of(j * tk, tk)
        s = _dot(k_ref[0, 0, pl.ds(off, tk), :], qcat_ref[...])
        s_refs[u][...] = s
        top_refs[u][...] = jnp.max(s, axis=0, keepdims=True)

    def accumulate(j, u):
        off = pl.multiple_of(j * tk, tk)
        vc = v_ref[0, 0, :, pl.ds(off, tk)]
        s = s_refs[u][...]
        m_old = m_ref[...]
        m_new = jnp.maximum(m_old, top_refs[u][...])
        p = jnp.exp2(s - m_new).astype(BF16)
        acc_ref[...] = jnp.exp2(m_old - m_new) * acc_ref[...] + _dot(vc, p)
        m_ref[...] = m_new

    nbuf = len(s_refs)
    scores(0, 0)

    def body(i, carry):
        j = nbuf * i
        for u in range(nbuf):
            scores(jnp.minimum(j + u + 1, n_chunks - 1), (u + 1) % nbuf)
            accumulate(j + u, u)
        return carry

    lax.fori_loop(0, n_chunks // nbuf, body, 0)
    acc = acc_ref[...]
    out = acc[0:HEAD_DIM] / acc[HEAD_DIM:HEAD_DIM + 1]
    out = jnp.concatenate([out, jnp.zeros((LANES - HEAD_DIM, cols), F32)], axis=0)
    for hh in range(HEADS_PER_KV):
        o_ref[0, hh] = out[:, hh * tq:(hh + 1) * tq].T


def _attention(q, k, v, tq, tk):
    b, _, _, l = q.shape
    cols = HEADS_PER_KV * tq
    nbuf = min(ATTN_SCORE_BUFFERS, l // tk)
    return pl.pallas_call(
        functools.partial(_attn_kernel, tk=tk),
        grid=(b, N_KV, l // tq),
        in_specs=[
            pl.BlockSpec((1, HEADS_PER_KV, LANES, tq), lambda bi, g, i: (bi, g, 0, i)),
            pl.BlockSpec((1, 1, l, LANES), lambda bi, g, i: (bi, g, 0, 0)),
            pl.BlockSpec((1, 1, LANES, l), lambda bi, g, i: (bi, g, 0, 0)),
        ],
        out_specs=pl.BlockSpec((1, HEADS_PER_KV, tq, LANES), lambda bi, g, i: (bi, g, i, 0)),
        out_shape=jax.ShapeDtypeStruct((b, N_HEADS, l, LANES), F32),
        scratch_shapes=[
            pltpu.VMEM((LANES, cols), BF16),
            pltpu.VMEM((1, cols), F32),
            pltpu.VMEM((LANES, cols), F32),
        ] + [pltpu.VMEM((tk, cols), F32)] * nbuf + [pltpu.VMEM((1, cols), F32)] * nbuf,
        compiler_params=_cparams(("parallel", "parallel", "parallel")),
        name="attn",
    )(q, k, v)


def _ssd_kernel(*refs, reverse):
    if reverse:
        (cur_ref, prev_ref, next_ref, dt_ref, dtT_ref, convw_ref, convb_ref, dtb_row_ref, dtb_col_ref,
         alog_row_ref, alog_col_ref, yin_ref, z_ref, dskip_ref, gain_ref,
         y_ref, pad_ref, state_ref) = refs
    else:
        (cur_ref, prev_ref, next_ref, dt_ref, dtT_ref, convw_ref, convb_ref, dtb_row_ref, dtb_col_ref,
         alog_row_ref, alog_col_ref, y_ref, pad_ref, state_ref) = refs
    c = pl.program_id(1)
    nc = pl.num_programs(1)
    chunk = (nc - 1 - c) if reverse else c

    @pl.when(c == 0)
    def _():
        state_ref[...] = jnp.zeros(state_ref.shape, F32)

    halo = SUBLANES
    pad_ref[0:halo, :] = jnp.where(chunk > 0, prev_ref[0], 0.0)
    pad_ref[halo:halo + CHUNK, :] = cur_ref[0]
    pad_ref[halo + CHUNK:, :] = jnp.where(chunk < nc - 1, next_ref[0], 0.0)
    padded = pad_ref[...]
    rows_padded = CHUNK + 2 * halo
    conv = jnp.broadcast_to(convb_ref[...], (CHUNK, XBC_WIDTH))
    for j in range(CONV_K):
        shifted = pltpu.roll(padded, (CONV_K // 2 - j) % rows_padded, 0) if j != CONV_K // 2 else padded
        conv = conv + convw_ref[j:j + 1, :] * shifted[halo:halo + CHUNK]
    xc = conv * _sigmoid(conv)
    xs = xc[:, :SSM_WIDTH]

    dt_col = _softplus(dt_ref[0, 0] + dtb_row_ref[0])
    dt_row = _softplus(dtT_ref[0, 0] + dtb_col_ref[0])
    da_col = dt_col * -jnp.exp(alog_row_ref[0])
    da_row = dt_row * -jnp.exp(alog_col_ref[0])

    ri = lax.broadcasted_iota(jnp.int32, (CHUNK, CHUNK), 0)
    ci = lax.broadcasted_iota(jnp.int32, (CHUNK, CHUNK), 1)
    low = ci <= ri
    up = ci >= ri
    low_m = jnp.where(low, 1.0, 0.0).astype(BF16)
    up_m = jnp.where(up, 1.0, 0.0).astype(BF16)
    col_m, row_m, mask = (up_m, low_m, up) if reverse else (low_m, up_m, low)
    ch, cm, cl = _split3(da_col)
    acum_col = _dot(col_m, ch) + _dot(col_m, cm) + _dot(col_m, cl)
    rh, rm, rl = _split3(da_row)
    acum_row = _dot(rh, row_m) + _dot(rm, row_m) + _dot(rl, row_m)
    end = 0 if reverse else CHUNK - 1
    a_end_row = acum_col[end:end + 1, :]
    w_col = dt_col * jnp.exp(a_end_row - acum_col)
    e_col = jnp.exp(acum_col)
    chunk_decay = jnp.exp(a_end_row)

    pieces = []
    for g in range(SSM_GROUPS):
        bm = xc[:, SSM_WIDTH + g * SSM_STATE:SSM_WIDTH + (g + 1) * SSM_STATE]
        cmat = xc[:, SSM_WIDTH + (SSM_GROUPS + g) * SSM_STATE:SSM_WIDTH + (SSM_GROUPS + g + 1) * SSM_STATE]
        bm16 = bm.astype(BF16)
        cm16 = cmat.astype(BF16)
        cb = _dot_nt(cm16, bm16)
        bt16 = bm.T.astype(BF16)
        for hh in range(SSM_HEADS_PER_GROUP):
            h = g * SSM_HEADS_PER_GROUP + hh
            xh = xs[:, h * SSM_HEAD_DIM:(h + 1) * SSM_HEAD_DIM]
            seg = acum_col[:, h:h + 1] - acum_row[h:h + 1, :]
            decay = jnp.exp(jnp.where(mask, seg, -jnp.inf))
            mat = cb * decay * dt_row[h:h + 1, :]
            y_diag = _dot(mat.astype(BF16), xh.astype(BF16))
            prev_state = state_ref[h]
            y_off = _dot(cm16, prev_state.astype(BF16)) * e_col[:, h:h + 1]
            pieces.append(y_diag + y_off)
            xw = xh * w_col[:, h:h + 1]
            state_ref[h] = prev_state * chunk_decay[:, h:h + 1] + _dot(bt16, xw.astype(BF16))
    y = jnp.concatenate(pieces, axis=1)

    if not reverse:
        y_ref[0] = y
    else:
        y = y + yin_ref[0] + xs * dskip_ref[...]
        zz = z_ref[0]
        y = y * (zz * _sigmoid(zz))
        gw = SSM_WIDTH // SSM_GROUPS
        outs = []
        for g in range(SSM_GROUPS):
            yg = y[:, g * gw:(g + 1) * gw]
            outs.append(yg * lax.rsqrt(jnp.mean(yg * yg, axis=-1, keepdims=True) + EPS))
        y_ref[0] = jnp.concatenate(outs, axis=1) * gain_ref[...]


def _ssd(xbc, dt, dtT, convw, convb, dtb_row, dtb_col, alog_row, alog_col, direction,
         y_fwd=None, z=None, dskip=None, gain=None):
    b, l, _ = xbc.shape
    nc = l // CHUNK
    reverse = direction == 1
    blocks_per_chunk = CHUNK // SUBLANES
    nblk = l // SUBLANES

    def pos(c):
        return (nc - 1 - c) if reverse else c

    const = lambda shape: pl.BlockSpec(shape, lambda bi, c: (0,) * len(shape))
    dsel = lambda shape: pl.BlockSpec(shape, lambda bi, c: (direction,) + (0,) * (len(shape) - 1))
    in_specs = [
        pl.BlockSpec((1, CHUNK, XBC_WIDTH), lambda bi, c: (bi, pos(c), 0)),
        pl.BlockSpec((1, SUBLANES, XBC_WIDTH),
                     lambda bi, c: (bi, jnp.maximum(pos(c) * blocks_per_chunk - 1, 0), 0)),
        pl.BlockSpec((1, SUBLANES, XBC_WIDTH),
                     lambda bi, c: (bi, jnp.minimum((pos(c) + 1) * blocks_per_chunk, nblk - 1), 0)),
        pl.BlockSpec((1, 1, CHUNK, SSM_HEADS), lambda bi, c: (direction, bi, pos(c), 0)),
        pl.BlockSpec((1, 1, SSM_HEADS, CHUNK), lambda bi, c: (direction, bi, 0, pos(c))),
        const((CONV_K, XBC_WIDTH)), const((1, XBC_WIDTH)),
        dsel((1, 1, SSM_HEADS)), dsel((1, SSM_HEADS, 1)), dsel((1, 1, SSM_HEADS)), dsel((1, SSM_HEADS, 1)),
    ]
    args = [xbc, xbc, xbc, dt, dtT, convw, convb, dtb_row, dtb_col, alog_row, alog_col]
    if reverse:
        in_specs += [
            pl.BlockSpec((1, CHUNK, SSM_WIDTH), lambda bi, c: (bi, pos(c), 0)),
            pl.BlockSpec((1, CHUNK, SSM_WIDTH), lambda bi, c: (bi, pos(c), 0)),
            const((1, SSM_WIDTH)), const((1, SSM_WIDTH)),
        ]
        args += [y_fwd, z, dskip, gain]
    return pl.pallas_call(
        functools.partial(_ssd_kernel, reverse=reverse),
        grid=(b, nc),
        in_specs=in_specs,
        out_specs=pl.BlockSpec((1, CHUNK, SSM_WIDTH), lambda bi, c: (bi, pos(c), 0)),
        out_shape=jax.ShapeDtypeStruct((b, l, SSM_WIDTH), F32),
        scratch_shapes=[
            pltpu.VMEM((CHUNK + 2 * SUBLANES, XBC_WIDTH), F32),
            pltpu.VMEM((SSM_HEADS, SSM_STATE, SSM_HEAD_DIM), F32),
        ],
        compiler_params=_cparams(("parallel", "arbitrary")),
        name="ssd_bwd" if reverse else "ssd_fwd",
    )(*args)


def _outproj_kernel(attn_ref, ssm_ref, x_ref, again_ref, wa_ref, ws_ref, fgain_ref, wr_hi_ref, wr_lo_ref,
                    x1_ref, hn_ref, logit_ref):
    a = jnp.concatenate([attn_ref[0, h] for h in range(N_HEADS)], axis=1)
    r = lax.rsqrt(jnp.sum(a * a, axis=-1, keepdims=True) * (1.0 / ATTN_WIDTH) + EPS)
    an = (a * r * again_ref[...]).astype(BF16)
    x1 = x_ref[...] + _dot(an, wa_ref[...]) + _dot(ssm_ref[...].astype(BF16), ws_ref[...])
    x1_ref[...] = x1
    hn = x1 * lax.rsqrt(jnp.mean(x1 * x1, axis=-1, keepdims=True) + EPS) * fgain_ref[...]
    hn_ref[...] = hn
    h_hi = hn.astype(BF16)
    h_lo = (hn - h_hi.astype(F32)).astype(BF16)
    w_hi = wr_hi_ref[...]
    logit_ref[...] = _dot_nt(w_hi, h_hi) + _dot_nt(w_hi, h_lo) + _dot_nt(wr_lo_ref[...], h_hi)


def _outproj(attn, ssm, x, again, wa, ws, fgain, wr_hi, wr_lo, tm):
    b, _, l, _ = attn.shape
    t = b * l
    nt = l // tm
    const = lambda shape: pl.BlockSpec(shape, lambda i: (0,) * len(shape))
    return pl.pallas_call(
        _outproj_kernel,
        grid=(t // tm,),
        in_specs=[
            pl.BlockSpec((1, N_HEADS, tm, LANES), lambda i: (i // nt, 0, i % nt, 0)),
            pl.BlockSpec((tm, SSM_WIDTH), lambda i: (i, 0)),
            pl.BlockSpec((tm, D_MODEL), lambda i: (i, 0)),
            const((1, Q_PAD)), const(wa.shape), const(ws.shape), const((1, D_MODEL)),
            const(wr_hi.shape), const(wr_lo.shape),
        ],
        out_specs=[
            pl.BlockSpec((tm, D_MODEL), lambda i: (i, 0)),
            pl.BlockSpec((tm, D_MODEL), lambda i: (i, 0)),
            pl.BlockSpec((N_EXPERTS, tm), lambda i: (0, i)),
        ],
        out_shape=[
            jax.ShapeDtypeStruct((t, D_MODEL), F32),
            jax.ShapeDtypeStruct((t, D_MODEL), F32),
            jax.ShapeDtypeStruct((N_EXPERTS, t), F32),
        ],
        compiler_params=_cparams(("parallel",)),
        name="outproj",
    )(attn, ssm, x, again, wa, ws, fgain, wr_hi, wr_lo)


def _route_kernel(logit_ref, bias_ref, idx_ref, w_ref, rank_ref, count_ref, tile_ref, carry_ref):
    i = pl.program_id(0)
    tn = logit_ref.shape[1]

    @pl.when(i == 0)
    def _():
        carry_ref[...] = jnp.zeros(carry_ref.shape, F32)

    lg = logit_ref[...] + bias_ref[...]
    eid = lax.broadcasted_iota(jnp.int32, (N_EXPERTS, tn), 0).astype(F32)
    chosen = jnp.zeros((N_EXPERTS, tn), F32)
    vals, ids = [], []
    for _ in range(TOP_K):
        mx = jnp.max(lg, axis=0, keepdims=True)
        sel = jnp.min(jnp.where(lg == mx, eid, float(N_EXPERTS)), axis=0, keepdims=True)
        hit = eid == sel
        chosen = jnp.where(hit, 1.0, chosen)
        lg = jnp.where(hit, -jnp.inf, lg)
        vals.append(mx)
        ids.append(sel)
    ex = [jnp.exp(v - vals[0]) for v in vals]
    den = ex[0] + ex[1] + ex[2] + ex[3]
    ri = lax.broadcasted_iota(jnp.int32, (tn, tn), 0)
    ci = lax.broadcasted_iota(jnp.int32, (tn, tn), 1)
    upper = jnp.where(ri <= ci, 1.0, 0.0).astype(BF16)
    incl = _dot(chosen.astype(BF16), upper)
    rank_e = incl - chosen + carry_ref[...]
    carry_ref[...] = carry_ref[...] + incl[:, tn - 1:tn]
    for k in range(TOP_K):
        idx_ref[k:k + 1, :] = ids[k].astype(jnp.int32)
        w_ref[k:k + 1, :] = ex[k] / den
        rank_ref[k:k + 1, :] = jnp.sum(jnp.where(eid == ids[k], rank_e, 0.0), axis=0,
                                       keepdims=True).astype(jnp.int32)
    count_ref[...] = carry_ref[...].astype(jnp.int32)
    tile_ref[0] = incl[:, tn - 1:tn].astype(jnp.int32)


def _route(logits_t, bias_col, tn):
    t = logits_t.shape[1]
    tok = pl.BlockSpec((TOP_K, tn), lambda i: (0, i))
    return pl.pallas_call(
        _route_kernel,
        grid=(t // tn,),
        in_specs=[pl.BlockSpec((N_EXPERTS, tn), lambda i: (0, i)),
                  pl.BlockSpec((N_EXPERTS, 1), lambda i: (0, 0))],
        out_specs=[tok, tok, tok, pl.BlockSpec((N_EXPERTS, 1), lambda i: (0, 0)),
                   pl.BlockSpec((1, N_EXPERTS, 1), lambda i: (i, 0, 0))],
        out_shape=[
            jax.ShapeDtypeStruct((TOP_K, t), jnp.int32),
            jax.ShapeDtypeStruct((TOP_K, t), F32),
            jax.ShapeDtypeStruct((TOP_K, t), jnp.int32),
            jax.ShapeDtypeStruct((N_EXPERTS, 1), jnp.int32),
            jax.ShapeDtypeStruct((t // tn, N_EXPERTS, 1), jnp.int32),
        ],
        scratch_shapes=[pltpu.VMEM((N_EXPERTS, 1), F32)],
        compiler_params=_cparams(("arbitrary",)),
        name="route",
    )(logits_t, bias_col)


RUN_ALIGN = SUBLANES
RUN_BITS = (256, 128, 64, 32, 16, 8)


def _copy_run(src_ref, dst_ref, src_start, dst_start, count, sem, start=True, wait=True):
    for bit in RUN_BITS:
        take = count & bit

        @pl.when(take != 0)
        def _():
            cp = pltpu.make_async_copy(src_ref.at[pl.ds(pl.multiple_of(src_start, RUN_ALIGN), bit), :],
                                       dst_ref.at[pl.ds(pl.multiple_of(dst_start, RUN_ALIGN), bit), :], sem)
            if start:
                cp.start()
            if wait:
                cp.wait()

        src_start = src_start + take
        dst_start = dst_start + take


def _tile_positions_rows(idx_ref, off_col, tri_ref):
    tn = idx_ref.shape[1]
    eid = lax.broadcasted_iota(jnp.int32, (N_EXPERTS, tn), 0)
    hits = [eid == idx_ref[k:k + 1, :] for k in range(TOP_K)]
    chosen = jnp.where(hits[0] | hits[1] | hits[2] | hits[3], 1.0, 0.0)
    incl = _dot(chosen.astype(BF16), tri_ref[...])
    slot = off_col + incl - chosen
    return [jnp.sum(jnp.where(h, slot, 0.0), axis=0, keepdims=True).astype(jnp.int32) for h in hits]


def _dispatch_kernel(seg_lo_ref, seg_hi_ref, tab_ref, tab_prev_ref, idx_ref, off_ref, tri_ref, hn_ref, xs_ref,
                     sorted_ref, zero_ref, sems, sem, *, bm):
    i = pl.program_id(0)
    last = pl.num_programs(0) - 1
    slot = i % 2
    tn = idx_ref.shape[1]
    n_sorted = sorted_ref.shape[1]

    @pl.when(i == 0)
    def _():
        zero_ref[...] = jnp.zeros(zero_ref.shape, F32)

        def segment(e, carry):
            lo = seg_lo_ref[e]
            count = seg_hi_ref[e] - lo

            def whole(j, c):
                cp = pltpu.make_async_copy(zero_ref, xs_ref.at[pl.ds(pl.multiple_of(lo + j * bm, RUN_ALIGN), bm), :], sem)
                cp.start()
                cp.wait()
                return c

            lax.fori_loop(0, count // bm, whole, 0)
            _copy_run(zero_ref, xs_ref, 0, lo + count // bm * bm, count % bm, sem)
            return carry

        lax.fori_loop(0, N_EXPERTS + 1, segment, 0)

    lpos = _tile_positions_rows(idx_ref, off_ref[0].astype(F32), tri_ref)
    pos = lax.broadcasted_iota(jnp.int32, (n_sorted, tn), 0)
    onehot = (pos == lpos[0]) | (pos == lpos[1]) | (pos == lpos[2]) | (pos == lpos[3])
    sorted_ref[slot] = _dot(jnp.where(onehot, 1.0, 0.0).astype(BF16), hn_ref[...].astype(BF16))

    def runs(tab, buf, start):
        def body(e, carry):
            _copy_run(sorted_ref.at[buf], xs_ref, tab[0, 0, e], tab[0, 0, 2 * N_EXPERTS + e],
                      tab[0, 0, N_EXPERTS + e], sems.at[buf], start=start, wait=not start)
            return carry

        lax.fori_loop(0, N_EXPERTS, body, 0)

    runs(tab_ref, slot, True)

    @pl.when(i > 0)
    def _():
        runs(tab_prev_ref, 1 - slot, False)

    @pl.when(i == last)
    def _():
        runs(tab_ref, slot, False)


def _dispatch(seg_lo, seg_hi, tab, idx_t, off_col, tri, hn, n_rows, n_sorted, bm, tn):
    t, d = hn.shape
    return pl.pallas_call(
        functools.partial(_dispatch_kernel, bm=bm),
        grid_spec=pltpu.PrefetchScalarGridSpec(
            num_scalar_prefetch=2,
            grid=(t // tn,),
            in_specs=[
                pl.BlockSpec((1, 1, 3 * N_EXPERTS), lambda i, lo, hi: (i, 0, 0), memory_space=pltpu.SMEM),
                pl.BlockSpec((1, 1, 3 * N_EXPERTS), lambda i, lo, hi: (jnp.maximum(i - 1, 0), 0, 0),
                             memory_space=pltpu.SMEM),
                pl.BlockSpec((TOP_K, tn), lambda i, lo, hi: (0, i)),
                pl.BlockSpec((1, N_EXPERTS, 1), lambda i, lo, hi: (i, 0, 0)),
                pl.BlockSpec((tn, tn), lambda i, lo, hi: (0, 0)),
                pl.BlockSpec((tn, d), lambda i, lo, hi: (i, 0)),
            ],
            out_specs=pl.BlockSpec(memory_space=pl.ANY),
            scratch_shapes=[pltpu.VMEM((2, n_sorted, d), F32), pltpu.VMEM((bm, d), F32),
                            pltpu.SemaphoreType.DMA((2,)), pltpu.SemaphoreType.DMA(())],
        ),
        out_shape=jax.ShapeDtypeStruct((n_rows, d), F32),
        compiler_params=_cparams(("arbitrary",)),
        name="dispatch",
    )(seg_lo, seg_hi, tab, tab, idx_t, off_col, tri, hn)


def _expert_kernel(block_e_ref, slot_ref, next_e_ref, nused_ref, xs_ref, wg_ref, bg_ref, wu_ref, bu_ref,
                   wd_ref, bd_ref, ys_ref, wbuf, w16, sems, *, base):
    b = pl.program_id(0)
    used = b < nused_ref[0]
    e = block_e_ref[b]
    slot = slot_ref[b]
    prev_e = block_e_ref[jnp.maximum(b - 1, 0)]
    fresh = jnp.logical_and(used, jnp.logical_or(b == 0, e != prev_e))

    def fetch(expert, buf):
        return [pltpu.make_async_copy(w_ref.at[base + expert], wbuf.at[buf, j], sems.at[buf])
                for j, w_ref in enumerate((wg_ref, wu_ref, wd_ref))]

    @pl.when(b == 0)
    def _():
        for cp in fetch(e, slot):
            cp.start()

    @pl.when(fresh)
    def _():
        for cp in fetch(e, slot):
            cp.wait()
        for j in range(3):
            w16[j] = wbuf[slot, j].astype(BF16)
        nxt = next_e_ref[b]

        @pl.when(nxt >= 0)
        def _():
            for cp in fetch(nxt, 1 - slot):
                cp.start()

    @pl.when(used)
    def _():
        x = xs_ref[...].astype(BF16)
        g = _dot(x, w16[0]) + bg_ref[0]
        u = _dot(x, w16[1]) + bu_ref[0]
        g = jnp.minimum(g, SWIGLU_LIMIT)
        u = jnp.clip(u, -SWIGLU_LIMIT, SWIGLU_LIMIT)
        act = g * _sigmoid(SWIGLU_ALPHA * g) * (u + 1.0)
        ys_ref[...] = _dot(act.astype(BF16), w16[2]) + bd_ref[0]

    @pl.when(jnp.logical_not(used))
    def _():
        ys_ref[...] = jnp.zeros(ys_ref.shape, F32)


def _experts(block_e, slot, next_e, nused, xs, wg, bg, wu, bu, wd, bd, bm, layer):
    n_rows, d = xs.shape
    nb = n_rows // bm
    base = layer * N_EXPERTS
    hbm = pl.BlockSpec(memory_space=pl.ANY)
    bspec = pl.BlockSpec((1, 1, d), lambda i, be, sl, ne, nu: (base + be[i], 0, 0))
    row = pl.BlockSpec((bm, d), lambda i, be, sl, ne, nu: (i, 0))
    return pl.pallas_call(
        functools.partial(_expert_kernel, base=base),
        grid_spec=pltpu.PrefetchScalarGridSpec(
            num_scalar_prefetch=4,
            grid=(nb,),
            in_specs=[row, hbm, bspec, hbm, bspec, hbm, bspec],
            out_specs=row,
            scratch_shapes=[pltpu.VMEM((2, 3, d, d), F32), pltpu.VMEM((3, d, d), BF16),
                            pltpu.SemaphoreType.DMA((2,))],
        ),
        out_shape=jax.ShapeDtypeStruct((n_rows, d), F32),
        compiler_params=_cparams(("arbitrary",)),
        name="experts",
    )(block_e, slot, next_e, nused, xs, wg, bg, wu, bu, wd, bd)


def _combine_kernel(tab_ref, tab_next_ref, idx_ref, w_ref, off_ref, tri_ref, x1_ref, ys_ref, out_ref,
                    sorted_ref, sems):
    i = pl.program_id(0)
    last = pl.num_programs(0) - 1
    cur = i % 2
    tn = x1_ref.shape[0]
    n_sorted = sorted_ref.shape[1]

    def runs(tab, buf, start):
        def body(e, carry):
            _copy_run(ys_ref, sorted_ref.at[buf], tab[0, 0, 2 * N_EXPERTS + e], tab[0, 0, e],
                      tab[0, 0, N_EXPERTS + e], sems.at[buf], start=start, wait=not start)
            return carry

        lax.fori_loop(0, N_EXPERTS, body, 0)

    @pl.when(i == 0)
    def _():
        runs(tab_ref, cur, True)

    @pl.when(i < last)
    def _():
        runs(tab_next_ref, 1 - cur, True)

    idx = idx_ref[...]
    eid = lax.broadcasted_iota(jnp.int32, (tn, N_EXPERTS), 1)
    hits = [eid == idx[:, k:k + 1] for k in range(TOP_K)]
    chosen = jnp.where(hits[0] | hits[1] | hits[2] | hits[3], 1.0, 0.0)
    incl = _dot(tri_ref[...], chosen.astype(BF16))
    slot = off_ref[0].astype(F32) + incl - chosen
    w = w_ref[...]
    pos = lax.broadcasted_iota(jnp.int32, (tn, n_sorted), 1)
    weights = jnp.zeros((tn, n_sorted), F32)
    for k in range(TOP_K):
        lpos = jnp.sum(jnp.where(hits[k], slot, 0.0), axis=1, keepdims=True).astype(jnp.int32)
        weights = jnp.where(pos == lpos, w[:, k:k + 1], weights)
    w_hi = weights.astype(BF16)
    w_lo = (weights - w_hi.astype(F32)).astype(BF16)

    runs(tab_ref, cur, False)
    row_id = lax.broadcasted_iota(jnp.int32, (n_sorted, 1), 0)
    y = jnp.where(row_id < tab_ref[0, 0, N_EXPERTS - 1] + tab_ref[0, 0, 2 * N_EXPERTS - 1], sorted_ref[cur], 0.0)
    y_hi = y.astype(BF16)
    y_lo = (y - y_hi.astype(F32)).astype(BF16)
    out_ref[...] = x1_ref[...] + _dot(w_hi, y_hi) + _dot(w_hi, y_lo) + _dot(w_lo, y_hi)


def _combine(tab, idx_tok, w_tok, off_row, tri, x1, ys, n_sorted, tn):
    t, d = x1.shape
    return pl.pallas_call(
        _combine_kernel,
        grid=(t // tn,),
        in_specs=[
            pl.BlockSpec((1, 1, 3 * N_EXPERTS), lambda i: (i, 0, 0), memory_space=pltpu.SMEM),
            pl.BlockSpec((1, 1, 3 * N_EXPERTS), lambda i: (jnp.minimum(i + 1, t // tn - 1), 0, 0),
                         memory_space=pltpu.SMEM),
            pl.BlockSpec((tn, TOP_K), lambda i: (i, 0)),
            pl.BlockSpec((tn, TOP_K), lambda i: (i, 0)),
            pl.BlockSpec((1, 1, N_EXPERTS), lambda i: (i, 0, 0)),
            pl.BlockSpec((tn, tn), lambda i: (0, 0)),
            pl.BlockSpec((tn, d), lambda i: (i, 0)),
            pl.BlockSpec(memory_space=pl.ANY),
        ],
        out_specs=pl.BlockSpec((tn, d), lambda i: (i, 0)),
        out_shape=jax.ShapeDtypeStruct((t, d), F32),
        scratch_shapes=[pltpu.VMEM((2, n_sorted, d), F32), pltpu.SemaphoreType.DMA((2,))],
        compiler_params=_cparams(("arbitrary",)),
        name="combine",
    )(tab, tab, idx_tok, w_tok, off_row, tri, x1, ys)


def _pad_heads(w, n_heads):
    d = w.shape[0]
    w = w.reshape(d, n_heads, HEAD_DIM)
    return jnp.pad(w, ((0, 0), (0, 0), (0, LANES - HEAD_DIM))).reshape(d, n_heads * LANES)


def _rope_tables(seq_len):
    rows = seq_len // GRID_W
    inv_freq = ROPE_THETA ** (-jnp.arange(0, ROPE_AXIS_DIM, 2, dtype=F32) / ROPE_AXIS_DIM)
    ang_r = jnp.arange(rows, dtype=F32)[:, None] * inv_freq
    ang_c = jnp.arange(GRID_W, dtype=F32)[:, None] * inv_freq
    expand_r = lambda a: jnp.repeat(a, GRID_W, axis=0)
    expand_c = lambda a: jnp.tile(a, (rows, 1))
    cr, sr = expand_r(jnp.cos(ang_r)), expand_r(jnp.sin(ang_r))
    cc, sc = expand_c(jnp.cos(ang_c)), expand_c(jnp.sin(ang_c))
    zeros = jnp.zeros_like(cr)
    pad = jnp.zeros((seq_len, LANES - HEAD_DIM), F32)
    cos = jnp.concatenate([cr, cr, cc, cc, pad], axis=1)
    s_up = jnp.concatenate([-sr, zeros, -sc, zeros, pad], axis=1)
    s_dn = jnp.concatenate([zeros, sr, zeros, sc, pad], axis=1)
    cos_t = jnp.concatenate([cr, cr, cc, cc], axis=1).T
    sin_t = jnp.concatenate([-sr, sr, -sc, sc], axis=1).T
    return cos, s_up, s_dn, cos_t, sin_t


def _tiles(seq_len, n_tok):
    return dict(
        tm_in=min(512, seq_len),
        tq=min(256, seq_len),
        tk=min(512, seq_len // 2),
        tm_out=min(512, seq_len),
        tn_route=min(256, n_tok),
        bm=256,
    )


def _layer(x, p, i, tabs, tl):
    b, l, d = x.shape
    t = b * l
    w_in = p["w_in"][i]
    o_q, o_k, o_v, o_z, o_xbc = ATTN_WIDTH, ATTN_WIDTH + KV_WIDTH, ATTN_WIDTH + 2 * KV_WIDTH, \
        ATTN_WIDTH + 2 * KV_WIDTH + SSM_WIDTH, ATTN_WIDTH + 2 * KV_WIDTH + SSM_WIDTH + XBC_WIDTH
    wq_t = w_in[:, :o_q].T.astype(BF16)
    wk = _pad_heads(w_in[:, o_q:o_k], N_KV).astype(BF16)
    wv_t = w_in[:, o_k:o_v].T.astype(BF16)
    wz = w_in[:, o_v:o_z].astype(BF16)
    wxbc = w_in[:, o_z:o_xbc].astype(BF16)
    wdt = w_in[:, o_xbc:].astype(BF16)
    q4, k4, v4, z, xbc, dt, dt_t = _inproj(
        x, p["norm_mix"][i].reshape(1, d), wq_t, wk, wv_t, wz, wxbc, wdt, wdt.T,
        p["q_norm"][i].reshape(HEAD_DIM, 1), jnp.pad(p["k_norm"][i], (0, LANES - HEAD_DIM)).reshape(1, LANES),
        tabs, tl["tm_in"])

    attn = _attention(q4, k4, v4, tl["tq"], tl["tk"])

    dt_dir = dt.reshape(b, l, 2, SSM_HEADS).transpose(2, 0, 1, 3)
    dtt_dir = dt_t.reshape(b, 2, SSM_HEADS, l).transpose(1, 0, 2, 3)
    ssd_common = (xbc, dt_dir, dtt_dir, p["conv_w"][i], p["conv_b"][i].reshape(1, XBC_WIDTH),
                  p["dt_bias"][i].reshape(2, 1, SSM_HEADS), p["dt_bias"][i].reshape(2, SSM_HEADS, 1),
                  p["a_log"][i].reshape(2, 1, SSM_HEADS), p["a_log"][i].reshape(2, SSM_HEADS, 1))
    y_fwd = _ssd(*ssd_common, 0)
    ssm = _ssd(*ssd_common, 1, y_fwd=y_fwd, z=z,
               dskip=jnp.repeat(p["d_skip"][i], SSM_HEAD_DIM).reshape(1, SSM_WIDTH),
               gain=p["ssm_norm"][i].reshape(1, SSM_WIDTH))

    w_out = p["w_out"][i]
    wa = jnp.pad(w_out[:ATTN_WIDTH].reshape(N_HEADS, HEAD_DIM, d),
                 ((0, 0), (0, LANES - HEAD_DIM), (0, 0))).reshape(Q_PAD, d).astype(BF16)
    ws = w_out[ATTN_WIDTH:].astype(BF16)
    again = jnp.pad(p["attn_norm"][i].reshape(N_HEADS, HEAD_DIM),
                    ((0, 0), (0, LANES - HEAD_DIM))).reshape(1, Q_PAD)
    wr_t = p["w_router"][i].T
    wr_hi = wr_t.astype(BF16)
    wr_lo = (wr_t - wr_hi.astype(F32)).astype(BF16)
    x1, hn, logits_t = _outproj(attn, ssm.reshape(t, SSM_WIDTH), x.reshape(t, d), again, wa, ws,
                                p["norm_ffn"][i].reshape(1, d), wr_hi, wr_lo, tl["tm_out"])

    tn = tl["tn_route"]
    idx_t, w_t, rank_t, counts, tile_counts = _route(logits_t, p["b_router"][i].reshape(N_EXPERTS, 1), tn)
    del rank_t

    bm = tl["bm"]
    nt = t // tn
    del counts
    tile_counts = tile_counts.reshape(nt, N_EXPERTS)
    run_len = (tile_counts + RUN_ALIGN - 1) // RUN_ALIGN * RUN_ALIGN
    run_off = jnp.cumsum(run_len, axis=1) - run_len
    rows_e = jnp.sum(run_len, axis=0)
    padded = (rows_e + bm - 1) // bm * bm
    pad_end = jnp.cumsum(padded)
    pad_start = pad_end - padded
    run_row = pad_start[None, :] + jnp.cumsum(run_len, axis=0) - run_len
    n_sorted = TOP_K * tn + N_EXPERTS * RUN_ALIGN
    n_rows = t * TOP_K + nt * N_EXPERTS * RUN_ALIGN + N_EXPERTS * bm
    nb = n_rows // bm
    block_row = jnp.arange(nb, dtype=jnp.int32) * bm
    block_e = jnp.minimum(jnp.sum(pad_end[None, :] <= block_row[:, None], axis=1), N_EXPERTS - 1).astype(jnp.int32)
    nused = (pad_end[-1:] // bm).astype(jnp.int32)
    has_rows = padded > 0
    order = jnp.cumsum(has_rows.astype(jnp.int32)) - 1
    expert_ids = jnp.arange(N_EXPERTS, dtype=jnp.int32)
    later = jnp.where(has_rows[None, :] & (expert_ids[None, :] > expert_ids[:, None]), expert_ids[None, :], N_EXPERTS)
    next_rows = jnp.min(later, axis=1)
    next_rows = jnp.where(next_rows < N_EXPERTS, next_rows, -1).astype(jnp.int32)
    block_slot = (order[block_e] % 2).astype(jnp.int32)
    block_next = next_rows[block_e]
    seg_lo = jnp.concatenate([pad_start + rows_e, pad_end[-1:]]).astype(jnp.int32)
    seg_hi = jnp.concatenate([pad_end, jnp.full((1,), n_rows, jnp.int32)]).astype(jnp.int32)
    tab = jnp.concatenate([run_off, run_len, run_row], axis=1).astype(jnp.int32).reshape(nt, 1, 3 * N_EXPERTS)
    ones = jnp.ones((tn, tn), BF16)

    xs = _dispatch(seg_lo, seg_hi, tab, idx_t, run_off.reshape(nt, N_EXPERTS, 1).astype(jnp.int32),
                   jnp.triu(ones), hn, n_rows, n_sorted, bm, tn)
    n_all = p["w_gate"].shape[0] * N_EXPERTS
    ys = _experts(block_e, block_slot, block_next, nused, xs,
                  p["w_gate"].reshape(n_all, d, d), p["b_gate"].reshape(n_all, 1, d),
                  p["w_up"].reshape(n_all, d, d), p["b_up"].reshape(n_all, 1, d),
                  p["w_down"].reshape(n_all, d, d), p["b_down"].reshape(n_all, 1, d), bm, i)
    x2 = _combine(tab, idx_t.T, w_t.T, run_off.reshape(nt, 1, N_EXPERTS).astype(jnp.int32),
                  jnp.tril(ones), x1, ys, n_sorted, tn)
    return x2.reshape(b, l, d)


_PARAM_NAMES = ("norm_mix", "w_in", "q_norm", "k_norm", "conv_w", "conv_b", "dt_bias", "a_log", "d_skip",
                "ssm_norm", "attn_norm", "w_out", "norm_ffn", "w_router", "b_router", "w_gate", "b_gate",
                "w_up", "b_up", "w_down", "b_down")


def kernel(x, norm_mix, w_in, q_norm, k_norm, conv_w, conv_b, dt_bias, a_log, d_skip, ssm_norm, attn_norm,
           w_out, norm_ffn, w_router, b_router, w_gate, b_gate, w_up, b_up, w_down, b_down):
    params = dict(zip(_PARAM_NAMES, (norm_mix, w_in, q_norm, k_norm, conv_w, conv_b, dt_bias, a_log, d_skip,
                                     ssm_norm, attn_norm, w_out, norm_ffn, w_router, b_router, w_gate, b_gate,
                                     w_up, b_up, w_down, b_down)))
    b, l, _ = x.shape
    tabs = _rope_tables(l)
    tl = _tiles(l, b * l)
    for i in range(norm_mix.shape[0]):
        x = _layer(x, params, i, tabs, tl)
    return x
```

```python
import functools
import math

import jax
import jax.numpy as jnp
from jax import lax
from jax.experimental import pallas as pl
from jax.experimental.pallas import tpu as pltpu

F32 = jnp.float32
BF16 = jnp.bfloat16

D_MODEL = 1024
GRID_W = 64
HEAD_DIM = 64
N_HEADS = 8
N_KV = 2
HEADS_PER_KV = N_HEADS // N_KV
ATTN_WIDTH = N_HEADS * HEAD_DIM
KV_WIDTH = N_KV * HEAD_DIM
SSM_WIDTH = 512
SSM_HEADS = 8
SSM_HEAD_DIM = 64
SSM_GROUPS = 2
SSM_HEADS_PER_GROUP = SSM_HEADS // SSM_GROUPS
SSM_STATE = 128
CHUNK = 128
CONV_K = 5
XBC_WIDTH = SSM_WIDTH + 2 * SSM_GROUPS * SSM_STATE
ROPE_THETA = 10000.0
ROPE_AXIS_DIM = HEAD_DIM // 2
N_EXPERTS = 32
TOP_K = 4
SWIGLU_LIMIT = 7.0
SWIGLU_ALPHA = 1.702
EPS = 1e-6

LANES = 128
SUBLANES = 8
VMEM_LIMIT = 56 * 1024 * 1024

Q_PAD = N_HEADS * LANES
ATTN_SCORE_BUFFERS = 2
ATTN_ACC_ROWS = HEAD_DIM + 2 * SUBLANES


def _cparams(sem):
    return pltpu.CompilerParams(dimension_semantics=sem, vmem_limit_bytes=VMEM_LIMIT)


def _dot(a, b):
    return jnp.dot(a, b, preferred_element_type=F32)


def _dot_nt(a, b):
    return lax.dot_general(a, b, (((1,), (1,)), ((), ())), preferred_element_type=F32)


def _split3(x):
    hi = x.astype(BF16)
    r1 = x - hi.astype(F32)
    mid = r1.astype(BF16)
    lo = (r1 - mid.astype(F32)).astype(BF16)
    return hi, mid, lo


def _sigmoid(x):
    return 1.0 / (1.0 + jnp.exp(-x))


def _softplus(x):
    return jnp.maximum(x, 0.0) + jnp.log(1.0 + jnp.exp(-jnp.abs(x)))


def _inproj_kernel(x_ref, g_ref, wqT_ref, wk_ref, wvT_ref, wz_ref, wxbc_ref, wdt_ref, wdtT_ref,
                   qg_ref, kg_ref, cos_ref, s1_ref, s2_ref, cosT_ref, sinT_ref,
                   q_ref, k_ref, v_ref, z_ref, xbc_ref, dt_ref, dtT_ref):
    x = x_ref[0]
    tm = x.shape[0]
    hn = (x * lax.rsqrt(jnp.mean(x * x, axis=-1, keepdims=True) + EPS) * g_ref[...]).astype(BF16)

    cos_t = cosT_ref[...]
    sin_t = sinT_ref[...]
    q_scale = math.log2(math.e) / math.sqrt(HEAD_DIM)
    acc_qt = _dot_nt(wqT_ref[...], hn)
    half = ROPE_AXIS_DIM // 2
    for h in range(N_HEADS):
        a = acc_qt[HEAD_DIM * h:HEAD_DIM * (h + 1)]
        r = lax.rsqrt(jnp.sum(a * a, axis=0, keepdims=True) * (1.0 / HEAD_DIM) + EPS)
        y = a * r * qg_ref[...]
        swapped = jnp.concatenate([y[half:2 * half], y[0:half], y[3 * half:4 * half], y[2 * half:3 * half]], axis=0)
        y = (y * cos_t + swapped * sin_t) * q_scale
        q_ref[0, h, 0:HEAD_DIM, :] = y.astype(BF16)
        q_ref[0, h, HEAD_DIM:, :] = jnp.zeros((LANES - HEAD_DIM, tm), BF16)

    cos = cos_ref[...]
    s1 = s1_ref[...]
    s2 = s2_ref[...]
    acc_k = _dot(hn, wk_ref[...])
    for g in range(N_KV):
        a = acc_k[:, LANES * g:LANES * (g + 1)]
        r = lax.rsqrt(jnp.sum(a * a, axis=-1, keepdims=True) * (1.0 / HEAD_DIM) + EPS)
        y = a * r * kg_ref[...]
        y = y * cos + pltpu.roll(y, LANES - 16, 1) * s1 + pltpu.roll(y, 16, 1) * s2
        k_ref[0, g] = y.astype(BF16)

    acc_vt = _dot_nt(wvT_ref[...], hn)
    ones_row = jnp.where(lax.broadcasted_iota(jnp.int32, (LANES - HEAD_DIM, tm), 0) == 0, 1.0, 0.0).astype(BF16)
    for g in range(N_KV):
        v_ref[0, g, 0:HEAD_DIM, :] = acc_vt[HEAD_DIM * g:HEAD_DIM * (g + 1)].astype(BF16)
        v_ref[0, g, HEAD_DIM:, :] = ones_row

    z_ref[0] = _dot(hn, wz_ref[...])
    xbc_ref[0] = _dot(hn, wxbc_ref[...])
    dt_ref[0] = _dot(hn, wdt_ref[...])
    dtT_ref[0] = _dot_nt(wdtT_ref[...], hn)


def _inproj(x, gain, wqT, wk, wvT, wz, wxbc, wdt, wdtT, qg, kg, tabs, tm):
    b, l, d = x.shape
    nt = l // tm
    const = lambda shape: pl.BlockSpec(shape, lambda bi, i: (0,) * len(shape))
    tab = pl.BlockSpec((tm, LANES), lambda bi, i: (i, 0))
    tab_t = pl.BlockSpec((HEAD_DIM, tm), lambda bi, i: (0, i))
    return pl.pallas_call(
        _inproj_kernel,
        grid=(b, nt),
        in_specs=[
            pl.BlockSpec((1, tm, d), lambda bi, i: (bi, i, 0)),
            const((1, d)), const(wqT.shape), const(wk.shape), const(wvT.shape), const(wz.shape),
            const(wxbc.shape), const(wdt.shape), const(wdtT.shape), const((HEAD_DIM, 1)), const((1, LANES)),
            tab, tab, tab, tab_t, tab_t,
        ],
        out_specs=[
            pl.BlockSpec((1, N_HEADS, LANES, tm), lambda bi, i: (bi, 0, 0, i)),
            pl.BlockSpec((1, N_KV, tm, LANES), lambda bi, i: (bi, 0, i, 0)),
            pl.BlockSpec((1, N_KV, LANES, tm), lambda bi, i: (bi, 0, 0, i)),
            pl.BlockSpec((1, tm, SSM_WIDTH), lambda bi, i: (bi, i, 0)),
            pl.BlockSpec((1, tm, XBC_WIDTH), lambda bi, i: (bi, i, 0)),
            pl.BlockSpec((1, tm, 2 * SSM_HEADS), lambda bi, i: (bi, i, 0)),
            pl.BlockSpec((1, 2 * SSM_HEADS, tm), lambda bi, i: (bi, 0, i)),
        ],
        out_shape=[
            jax.ShapeDtypeStruct((b, N_HEADS, LANES, l), BF16),
            jax.ShapeDtypeStruct((b, N_KV, l, LANES), BF16),
            jax.ShapeDtypeStruct((b, N_KV, LANES, l), BF16),
            jax.ShapeDtypeStruct((b, l, SSM_WIDTH), F32),
            jax.ShapeDtypeStruct((b, l, XBC_WIDTH), F32),
            jax.ShapeDtypeStruct((b, l, 2 * SSM_HEADS), F32),
            jax.ShapeDtypeStruct((b, 2 * SSM_HEADS, l), F32),
        ],
        compiler_params=_cparams(("parallel", "parallel")),
        name="inproj",
    )(x, gain, wqT, wk, wvT, wz, wxbc, wdt, wdtT, qg, kg, *tabs)


def _attn_kernel(q_ref, k_ref, v_ref, o_ref, qcat_ref, m_ref, acc_ref, *bufs, tk):
    tq = q_ref.shape[3]
    seq = k_ref.shape[2]
    cols = HEADS_PER_KV * tq
    n_chunks = seq // tk
    for hh in range(HEADS_PER_KV):
        qcat_ref[:, hh * tq:(hh + 1) * tq] = q_ref[0, hh]
    m_ref[...] = jnp.full((1, cols), -jnp.inf, F32)
    acc_ref[...] = jnp.zeros(acc_ref.shape, F32)

    s_refs, top_refs = bufs[:len(bufs) // 2], bufs[len(bufs) // 2:]

    def scores(j, u):
        off = pl.multiple_of(j * tk, tk)
        s = _dot(k_ref[0, 0, pl.ds(off, tk), :], qcat_ref[...])
        s_refs[u][...] = s
        top_refs[u][...] = jnp.max(s, axis=0, keepdims=True)

    def accumulate(j, u):
        off = pl.multiple_of(j * tk, tk)
        vc = v_ref[0, 0, 0:ATTN_ACC_ROWS, pl.ds(off, tk)]
        s = s_refs[u][...]
        m_old = m_ref[...]
        m_new = jnp.maximum(m_old, top_refs[u][...])
        p = jnp.exp2(s - m_new).astype(BF16)
        acc_ref[...] = jnp.exp2(m_old - m_new) * acc_ref[...] + _dot(vc, p)
        m_ref[...] = m_new

    nbuf = len(s_refs)
    scores(0, 0)

    def body(i, carry):
        j = nbuf * i
        for u in range(nbuf):
            scores(jnp.minimum(j + u + 1, n_chunks - 1), (u + 1) % nbuf)
            accumulate(j + u, u)
        return carry

    lax.fori_loop(0, n_chunks // nbuf, body, 0)
    acc = acc_ref[...]
    out = acc[0:HEAD_DIM] / acc[HEAD_DIM:HEAD_DIM + 1]
    out = jnp.concatenate([out, jnp.zeros((LANES - HEAD_DIM, cols), F32)], axis=0)
    for hh in range(HEADS_PER_KV):
        o_ref[0, hh] = out[:, hh * tq:(hh + 1) * tq].T


def _attention(q, k, v, tq, tk):
    b, _, _, l = q.shape
    cols = HEADS_PER_KV * tq
    nbuf = min(ATTN_SCORE_BUFFERS, l // tk)
    return pl.pallas_call(
        functools.partial(_attn_kernel, tk=tk),
        grid=(b, N_KV, l // tq),
        in_specs=[
            pl.BlockSpec((1, HEADS_PER_KV, LANES, tq), lambda bi, g, i: (bi, g, 0, i)),
            pl.BlockSpec((1, 1, l, LANES), lambda bi, g, i: (bi, g, 0, 0)),
            pl.BlockSpec((1, 1, LANES, l), lambda bi, g, i: (bi, g, 0, 0)),
        ],
        out_specs=pl.BlockSpec((1, HEADS_PER_KV, tq, LANES), lambda bi, g, i: (bi, g, i, 0)),
        out_shape=jax.ShapeDtypeStruct((b, N_HEADS, l, LANES), F32),
        scratch_shapes=[
            pltpu.VMEM((LANES, cols), BF16),
            pltpu.VMEM((1, cols), F32),
            pltpu.VMEM((ATTN_ACC_ROWS, cols), F32),
        ] + [pltpu.VMEM((tk, cols), F32)] * nbuf + [pltpu.VMEM((1, cols), F32)] * nbuf,
        compiler_params=_cparams(("parallel", "parallel", "parallel")),
        name="attn",
    )(q, k, v)


def _ssd_kernel(*refs, reverse):
    if reverse:
        (cur_ref, prev_ref, next_ref, dt_ref, dtT_ref, convw_ref, convb_ref, dtb_row_ref, dtb_col_ref,
         alog_row_ref, alog_col_ref, yin_ref, z_ref, dskip_ref, gain_ref,
         y_ref, pad_ref, state_ref) = refs
    else:
        (cur_ref, prev_ref, next_ref, dt_ref, dtT_ref, convw_ref, convb_ref, dtb_row_ref, dtb_col_ref,
         alog_row_ref, alog_col_ref, y_ref, pad_ref, state_ref) = refs
    c = pl.program_id(1)
    nc = pl.num_programs(1)
    chunk = (nc - 1 - c) if reverse else c

    @pl.when(c == 0)
    def _():
        state_ref[...] = jnp.zeros(state_ref.shape, F32)

    halo = SUBLANES
    pad_ref[0:halo, :] = jnp.where(chunk > 0, prev_ref[0], 0.0)
    pad_ref[halo:halo + CHUNK, :] = cur_ref[0]
    pad_ref[halo + CHUNK:, :] = jnp.where(chunk < nc - 1, next_ref[0], 0.0)
    padded = pad_ref[...]
    rows_padded = CHUNK + 2 * halo
    conv = jnp.broadcast_to(convb_ref[...], (CHUNK, XBC_WIDTH))
    for j in range(CONV_K):
        shifted = pltpu.roll(padded, (CONV_K // 2 - j) % rows_padded, 0) if j != CONV_K // 2 else padded
        conv = conv + convw_ref[j:j + 1, :] * shifted[halo:halo + CHUNK]
    xc = conv * _sigmoid(conv)
    xs = xc[:, :SSM_WIDTH]

    dt_col = _softplus(dt_ref[0, 0] + dtb_row_ref[0])
    dt_row = _softplus(dtT_ref[0, 0] + dtb_col_ref[0])
    da_col = dt_col * -jnp.exp(alog_row_ref[0])
    da_row = dt_row * -jnp.exp(alog_col_ref[0])

    ri = lax.broadcasted_iota(jnp.int32, (CHUNK, CHUNK), 0)
    ci = lax.broadcasted_iota(jnp.int32, (CHUNK, CHUNK), 1)
    low = ci <= ri
    up = ci >= ri
    low_m = jnp.where(low, 1.0, 0.0).astype(BF16)
    up_m = jnp.where(up, 1.0, 0.0).astype(BF16)
    col_m, row_m, mask = (up_m, low_m, up) if reverse else (low_m, up_m, low)
    ch, cm, cl = _split3(da_col)
    acum_col = _dot(col_m, ch) + _dot(col_m, cm) + _dot(col_m, cl)
    rh, rm, rl = _split3(da_row)
    acum_row = _dot(rh, row_m) + _dot(rm, row_m) + _dot(rl, row_m)
    end = 0 if reverse else CHUNK - 1
    a_end_row = acum_col[end:end + 1, :]
    w_col = dt_col * jnp.exp(a_end_row - acum_col)
    e_col = jnp.exp(acum_col)
    chunk_decay = jnp.exp(a_end_row)

    pieces = []
    for g in range(SSM_GROUPS):
        bm = xc[:, SSM_WIDTH + g * SSM_STATE:SSM_WIDTH + (g + 1) * SSM_STATE]
        cmat = xc[:, SSM_WIDTH + (SSM_GROUPS + g) * SSM_STATE:SSM_WIDTH + (SSM_GROUPS + g + 1) * SSM_STATE]
        bm16 = bm.astype(BF16)
        cm16 = cmat.astype(BF16)
        cb = _dot_nt(cm16, bm16)
        bt16 = bm.T.astype(BF16)
        for hh in range(SSM_HEADS_PER_GROUP):
            h = g * SSM_HEADS_PER_GROUP + hh
            xh = xs[:, h * SSM_HEAD_DIM:(h + 1) * SSM_HEAD_DIM]
            seg = acum_col[:, h:h + 1] - acum_row[h:h + 1, :]
            decay = jnp.exp(jnp.where(mask, seg, -jnp.inf))
            mat = cb * decay * dt_row[h:h + 1, :]
            y_diag = _dot(mat.astype(BF16), xh.astype(BF16))
            prev_state = state_ref[h]
            y_off = _dot(cm16, prev_state.astype(BF16)) * e_col[:, h:h + 1]
            pieces.append(y_diag + y_off)
            xw = xh * w_col[:, h:h + 1]
            state_ref[h] = prev_state * chunk_decay[:, h:h + 1] + _dot(bt16, xw.astype(BF16))
    y = jnp.concatenate(pieces, axis=1)

    if not reverse:
        y_ref[0] = y
    else:
        y = y + yin_ref[0] + xs * dskip_ref[...]
        zz = z_ref[0]
        y = y * (zz * _sigmoid(zz))
        gw = SSM_WIDTH // SSM_GROUPS
        outs = []
        for g in range(SSM_GROUPS):
            yg = y[:, g * gw:(g + 1) * gw]
            outs.append(yg * lax.rsqrt(jnp.mean(yg * yg, axis=-1, keepdims=True) + EPS))
        y_ref[0] = jnp.concatenate(outs, axis=1) * gain_ref[...]


def _ssd(xbc, dt, dtT, convw, convb, dtb_row, dtb_col, alog_row, alog_col, direction,
         y_fwd=None, z=None, dskip=None, gain=None):
    b, l, _ = xbc.shape
    nc = l // CHUNK
    reverse = direction == 1
    blocks_per_chunk = CHUNK // SUBLANES
    nblk = l // SUBLANES

    def pos(c):
        return (nc - 1 - c) if reverse else c

    const = lambda shape: pl.BlockSpec(shape, lambda bi, c: (0,) * len(shape))
    dsel = lambda shape: pl.BlockSpec(shape, lambda bi, c: (direction,) + (0,) * (len(shape) - 1))
    in_specs = [
        pl.BlockSpec((1, CHUNK, XBC_WIDTH), lambda bi, c: (bi, pos(c), 0)),
        pl.BlockSpec((1, SUBLANES, XBC_WIDTH),
                     lambda bi, c: (bi, jnp.maximum(pos(c) * blocks_per_chunk - 1, 0), 0)),
        pl.BlockSpec((1, SUBLANES, XBC_WIDTH),
                     lambda bi, c: (bi, jnp.minimum((pos(c) + 1) * blocks_per_chunk, nblk - 1), 0)),
        pl.BlockSpec((1, 1, CHUNK, SSM_HEADS), lambda bi, c: (direction, bi, pos(c), 0)),
        pl.BlockSpec((1, 1, SSM_HEADS, CHUNK), lambda bi, c: (direction, bi, 0, pos(c))),
        const((CONV_K, XBC_WIDTH)), const((1, XBC_WIDTH)),
        dsel((1, 1, SSM_HEADS)), dsel((1, SSM_HEADS, 1)), dsel((1, 1, SSM_HEADS)), dsel((1, SSM_HEADS, 1)),
    ]
    args = [xbc, xbc, xbc, dt, dtT, convw, convb, dtb_row, dtb_col, alog_row, alog_col]
    if reverse:
        in_specs += [
            pl.BlockSpec((1, CHUNK, SSM_WIDTH), lambda bi, c: (bi, pos(c), 0)),
            pl.BlockSpec((1, CHUNK, SSM_WIDTH), lambda bi, c: (bi, pos(c), 0)),
            const((1, SSM_WIDTH)), const((1, SSM_WIDTH)),
        ]
        args += [y_fwd, z, dskip, gain]
    return pl.pallas_call(
        functools.partial(_ssd_kernel, reverse=reverse),
        grid=(b, nc),
        in_specs=in_specs,
        out_specs=pl.BlockSpec((1, CHUNK, SSM_WIDTH), lambda bi, c: (bi, pos(c), 0)),
        out_shape=jax.ShapeDtypeStruct((b, l, SSM_WIDTH), F32),
        scratch_shapes=[
            pltpu.VMEM((CHUNK + 2 * SUBLANES, XBC_WIDTH), F32),
            pltpu.VMEM((SSM_HEADS, SSM_STATE, SSM_HEAD_DIM), F32),
        ],
        compiler_params=_cparams(("parallel", "arbitrary")),
        name="ssd_bwd" if reverse else "ssd_fwd",
    )(*args)


def _outproj_kernel(attn_ref, ssm_ref, x_ref, again_ref, wa_ref, ws_ref, fgain_ref, wr_hi_ref, wr_lo_ref,
                    x1_ref, hn_ref, logit_ref):
    a = jnp.concatenate([attn_ref[0, h] for h in range(N_HEADS)], axis=1)
    r = lax.rsqrt(jnp.sum(a * a, axis=-1, keepdims=True) * (1.0 / ATTN_WIDTH) + EPS)
    an = (a * r * again_ref[...]).astype(BF16)
    x1 = x_ref[...] + _dot(an, wa_ref[...]) + _dot(ssm_ref[...].astype(BF16), ws_ref[...])
    x1_ref[...] = x1
    hn = x1 * lax.rsqrt(jnp.mean(x1 * x1, axis=-1, keepdims=True) + EPS) * fgain_ref[...]
    hn_ref[...] = hn
    h_hi = hn.astype(BF16)
    h_lo = (hn - h_hi.astype(F32)).astype(BF16)
    w_hi = wr_hi_ref[...]
    logit_ref[...] = _dot_nt(w_hi, h_hi) + _dot_nt(w_hi, h_lo) + _dot_nt(wr_lo_ref[...], h_hi)


def _outproj(attn, ssm, x, again, wa, ws, fgain, wr_hi, wr_lo, tm):
    b, _, l, _ = attn.shape
    t = b * l
    nt = l // tm
    const = lambda shape: pl.BlockSpec(shape, lambda i: (0,) * len(shape))
    return pl.pallas_call(
        _outproj_kernel,
        grid=(t // tm,),
        in_specs=[
            pl.BlockSpec((1, N_HEADS, tm, LANES), lambda i: (i // nt, 0, i % nt, 0)),
            pl.BlockSpec((tm, SSM_WIDTH), lambda i: (i, 0)),
            pl.BlockSpec((tm, D_MODEL), lambda i: (i, 0)),
            const((1, Q_PAD)), const(wa.shape), const(ws.shape), const((1, D_MODEL)),
            const(wr_hi.shape), const(wr_lo.shape),
        ],
        out_specs=[
            pl.BlockSpec((tm, D_MODEL), lambda i: (i, 0)),
            pl.BlockSpec((tm, D_MODEL), lambda i: (i, 0)),
            pl.BlockSpec((N_EXPERTS, tm), lambda i: (0, i)),
        ],
        out_shape=[
            jax.ShapeDtypeStruct((t, D_MODEL), F32),
            jax.ShapeDtypeStruct((t, D_MODEL), F32),
            jax.ShapeDtypeStruct((N_EXPERTS, t), F32),
        ],
        compiler_params=_cparams(("parallel",)),
        name="outproj",
    )(attn, ssm, x, again, wa, ws, fgain, wr_hi, wr_lo)


def _route_kernel(logit_ref, bias_ref, idx_ref, w_ref, rank_ref, count_ref, tile_ref, carry_ref):
    i = pl.program_id(0)
    tn = logit_ref.shape[1]

    @pl.when(i == 0)
    def _():
        carry_ref[...] = jnp.zeros(carry_ref.shape, F32)

    lg = logit_ref[...] + bias_ref[...]
    eid = lax.broadcasted_iota(jnp.int32, (N_EXPERTS, tn), 0).astype(F32)
    chosen = jnp.zeros((N_EXPERTS, tn), F32)
    vals, ids = [], []
    for _ in range(TOP_K):
        mx = jnp.max(lg, axis=0, keepdims=True)
        sel = jnp.min(jnp.where(lg == mx, eid, float(N_EXPERTS)), axis=0, keepdims=True)
        hit = eid == sel
        chosen = jnp.where(hit, 1.0, chosen)
        lg = jnp.where(hit, -jnp.inf, lg)
        vals.append(mx)
        ids.append(sel)
    ex = [jnp.exp(v - vals[0]) for v in vals]
    den = ex[0] + ex[1] + ex[2] + ex[3]
    ri = lax.broadcasted_iota(jnp.int32, (tn, tn), 0)
    ci = lax.broadcasted_iota(jnp.int32, (tn, tn), 1)
    upper = jnp.where(ri <= ci, 1.0, 0.0).astype(BF16)
    incl = _dot(chosen.astype(BF16), upper)
    rank_e = incl - chosen + carry_ref[...]
    carry_ref[...] = carry_ref[...] + incl[:, tn - 1:tn]
    for k in range(TOP_K):
        idx_ref[k:k + 1, :] = ids[k].astype(jnp.int32)
        w_ref[k:k + 1, :] = ex[k] / den
        rank_ref[k:k + 1, :] = jnp.sum(jnp.where(eid == ids[k], rank_e, 0.0), axis=0,
                                       keepdims=True).astype(jnp.int32)
    count_ref[...] = carry_ref[...].astype(jnp.int32)
    tile_ref[0] = incl[:, tn - 1:tn].astype(jnp.int32)


def _route(logits_t, bias_col, tn):
    t = logits_t.shape[1]
    tok = pl.BlockSpec((TOP_K, tn), lambda i: (0, i))
    return pl.pallas_call(
        _route_kernel,
        grid=(t // tn,),
        in_specs=[pl.BlockSpec((N_EXPERTS, tn), lambda i: (0, i)),
                  pl.BlockSpec((N_EXPERTS, 1), lambda i: (0, 0))],
        out_specs=[tok, tok, tok, pl.BlockSpec((N_EXPERTS, 1), lambda i: (0, 0)),
                   pl.BlockSpec((1, N_EXPERTS, 1), lambda i: (i, 0, 0))],
        out_shape=[
            jax.ShapeDtypeStruct((TOP_K, t), jnp.int32),
            jax.ShapeDtypeStruct((TOP_K, t), F32),
            jax.ShapeDtypeStruct((TOP_K, t), jnp.int32),
            jax.ShapeDtypeStruct((N_EXPERTS, 1), jnp.int32),
            jax.ShapeDtypeStruct((t // tn, N_EXPERTS, 1), jnp.int32),
        ],
        scratch_shapes=[pltpu.VMEM((N_EXPERTS, 1), F32)],
        compiler_params=_cparams(("arbitrary",)),
        name="route",
    )(logits_t, bias_col)


RUN_ALIGN = SUBLANES
RUN_BITS = (256, 128, 64, 32, 16, 8)


def _copy_run(src_ref, dst_ref, src_start, dst_start, count, sem, start=True, wait=True):
    for bit in RUN_BITS:
        take = count & bit

        @pl.when(take != 0)
        def _():
            cp = pltpu.make_async_copy(src_ref.at[pl.ds(pl.multiple_of(src_start, RUN_ALIGN), bit), :],
                                       dst_ref.at[pl.ds(pl.multiple_of(dst_start, RUN_ALIGN), bit), :], sem)
            if start:
                cp.start()
            if wait:
                cp.wait()

        src_start = src_start + take
        dst_start = dst_start + take


def _tile_positions_rows(idx_ref, off_col, tri_ref):
    tn = idx_ref.shape[1]
    eid = lax.broadcasted_iota(jnp.int32, (N_EXPERTS, tn), 0)
    hits = [eid == idx_ref[k:k + 1, :] for k in range(TOP_K)]
    chosen = jnp.where(hits[0] | hits[1] | hits[2] | hits[3], 1.0, 0.0)
    incl = _dot(chosen.astype(BF16), tri_ref[...])
    slot = off_col + incl - chosen
    return [jnp.sum(jnp.where(h, slot, 0.0), axis=0, keepdims=True).astype(jnp.int32) for h in hits]


def _dispatch_kernel(seg_lo_ref, seg_hi_ref, tab_ref, tab_prev_ref, idx_ref, off_ref, tri_ref, hn_ref, xs_ref,
                     sorted_ref, zero_ref, sems, sem, *, bm):
    i = pl.program_id(0)
    last = pl.num_programs(0) - 1
    slot = i % 2
    tn = idx_ref.shape[1]
    n_sorted = sorted_ref.shape[1]

    @pl.when(i == 0)
    def _():
        zero_ref[...] = jnp.zeros(zero_ref.shape, F32)

        def segment(e, carry):
            lo = seg_lo_ref[e]
            count = seg_hi_ref[e] - lo

            def whole(j, c):
                cp = pltpu.make_async_copy(zero_ref, xs_ref.at[pl.ds(pl.multiple_of(lo + j * bm, RUN_ALIGN), bm), :], sem)
                cp.start()
                cp.wait()
                return c

            lax.fori_loop(0, count // bm, whole, 0)
            _copy_run(zero_ref, xs_ref, 0, lo + count // bm * bm, count % bm, sem)
            return carry

        lax.fori_loop(0, N_EXPERTS + 1, segment, 0)

    lpos = _tile_positions_rows(idx_ref, off_ref[0].astype(F32), tri_ref)
    pos = lax.broadcasted_iota(jnp.int32, (n_sorted, tn), 0)
    onehot = (pos == lpos[0]) | (pos == lpos[1]) | (pos == lpos[2]) | (pos == lpos[3])
    sorted_ref[slot] = _dot(jnp.where(onehot, 1.0, 0.0).astype(BF16), hn_ref[...].astype(BF16))

    def runs(tab, buf, start):
        def body(e, carry):
            _copy_run(sorted_ref.at[buf], xs_ref, tab[0, 0, e], tab[0, 0, 2 * N_EXPERTS + e],
                      tab[0, 0, N_EXPERTS + e], sems.at[buf], start=start, wait=not start)
            return carry

        lax.fori_loop(0, N_EXPERTS, body, 0)

    runs(tab_ref, slot, True)

    @pl.when(i > 0)
    def _():
        runs(tab_prev_ref, 1 - slot, False)

    @pl.when(i == last)
    def _():
        runs(tab_ref, slot, False)


def _dispatch(seg_lo, seg_hi, tab, idx_t, off_col, tri, hn, n_rows, n_sorted, bm, tn):
    t, d = hn.shape
    return pl.pallas_call(
        functools.partial(_dispatch_kernel, bm=bm),
        grid_spec=pltpu.PrefetchScalarGridSpec(
            num_scalar_prefetch=2,
            grid=(t // tn,),
            in_specs=[
                pl.BlockSpec((1, 1, 3 * N_EXPERTS), lambda i, lo, hi: (i, 0, 0), memory_space=pltpu.SMEM),
                pl.BlockSpec((1, 1, 3 * N_EXPERTS), lambda i, lo, hi: (jnp.maximum(i - 1, 0), 0, 0),
                             memory_space=pltpu.SMEM),
                pl.BlockSpec((TOP_K, tn), lambda i, lo, hi: (0, i)),
                pl.BlockSpec((1, N_EXPERTS, 1), lambda i, lo, hi: (i, 0, 0)),
                pl.BlockSpec((tn, tn), lambda i, lo, hi: (0, 0)),
                pl.BlockSpec((tn, d), lambda i, lo, hi: (i, 0)),
            ],
            out_specs=pl.BlockSpec(memory_space=pl.ANY),
            scratch_shapes=[pltpu.VMEM((2, n_sorted, d), F32), pltpu.VMEM((bm, d), F32),
                            pltpu.SemaphoreType.DMA((2,)), pltpu.SemaphoreType.DMA(())],
        ),
        out_shape=jax.ShapeDtypeStruct((n_rows, d), F32),
        compiler_params=_cparams(("arbitrary",)),
        name="dispatch",
    )(seg_lo, seg_hi, tab, tab, idx_t, off_col, tri, hn)


def _expert_kernel(block_e_ref, slot_ref, next_e_ref, nused_ref, xs_ref, wg_ref, bg_ref, wu_ref, bu_ref,
                   wd_ref, bd_ref, ys_ref, wbuf, w16, sems, *, base):
    b = pl.program_id(0)
    used = b < nused_ref[0]
    e = block_e_ref[b]
    slot = slot_ref[b]
    prev_e = block_e_ref[jnp.maximum(b - 1, 0)]
    fresh = jnp.logical_and(used, jnp.logical_or(b == 0, e != prev_e))

    def fetch(expert, buf):
        return [pltpu.make_async_copy(w_ref.at[base + expert], wbuf.at[buf, j], sems.at[buf])
                for j, w_ref in enumerate((wg_ref, wu_ref, wd_ref))]

    @pl.when(b == 0)
    def _():
        for cp in fetch(e, slot):
            cp.start()

    @pl.when(fresh)
    def _():
        for cp in fetch(e, slot):
            cp.wait()
        for j in range(3):
            w16[j] = wbuf[slot, j].astype(BF16)
        nxt = next_e_ref[b]

        @pl.when(nxt >= 0)
        def _():
            for cp in fetch(nxt, 1 - slot):
                cp.start()

    @pl.when(used)
    def _():
        x = xs_ref[...].astype(BF16)
        g = _dot(x, w16[0]) + bg_ref[0]
        u = _dot(x, w16[1]) + bu_ref[0]
        g = jnp.minimum(g, SWIGLU_LIMIT)
        u = jnp.clip(u, -SWIGLU_LIMIT, SWIGLU_LIMIT)
        act = g * _sigmoid(SWIGLU_ALPHA * g) * (u + 1.0)
        ys_ref[...] = _dot(act.astype(BF16), w16[2]) + bd_ref[0]

    @pl.when(jnp.logical_not(used))
    def _():
        ys_ref[...] = jnp.zeros(ys_ref.shape, F32)


def _experts(block_e, slot, next_e, nused, xs, wg, bg, wu, bu, wd, bd, bm, layer):
    n_rows, d = xs.shape
    nb = n_rows // bm
    base = layer * N_EXPERTS
    hbm = pl.BlockSpec(memory_space=pl.ANY)
    bspec = pl.BlockSpec((1, 1, d), lambda i, be, sl, ne, nu: (base + be[i], 0, 0))
    row = pl.BlockSpec((bm, d), lambda i, be, sl, ne, nu: (i, 0))
    return pl.pallas_call(
        functools.partial(_expert_kernel, base=base),
        grid_spec=pltpu.PrefetchScalarGridSpec(
            num_scalar_prefetch=4,
            grid=(nb,),
            in_specs=[row, hbm, bspec, hbm, bspec, hbm, bspec],
            out_specs=row,
            scratch_shapes=[pltpu.VMEM((2, 3, d, d), F32), pltpu.VMEM((3, d, d), BF16),
                            pltpu.SemaphoreType.DMA((2,))],
        ),
        out_shape=jax.ShapeDtypeStruct((n_rows, d), F32),
        compiler_params=_cparams(("arbitrary",)),
        name="experts",
    )(block_e, slot, next_e, nused, xs, wg, bg, wu, bu, wd, bd)


def _combine_kernel(tab_ref, tab_next_ref, idx_ref, w_ref, off_ref, tri_ref, x1_ref, ys_ref, out_ref,
                    sorted_ref, sems):
    i = pl.program_id(0)
    last = pl.num_programs(0) - 1
    cur = i % 2
    tn = x1_ref.shape[0]
    n_sorted = sorted_ref.shape[1]

    def runs(tab, buf, start):
        def body(e, carry):
            _copy_run(ys_ref, sorted_ref.at[buf], tab[0, 0, 2 * N_EXPERTS + e], tab[0, 0, e],
                      tab[0, 0, N_EXPERTS + e], sems.at[buf], start=start, wait=not start)
            return carry

        lax.fori_loop(0, N_EXPERTS, body, 0)

    @pl.when(i == 0)
    def _():
        runs(tab_ref, cur, True)

    @pl.when(i < last)
    def _():
        runs(tab_next_ref, 1 - cur, True)

    idx = idx_ref[...]
    eid = lax.broadcasted_iota(jnp.int32, (tn, N_EXPERTS), 1)
    hits = [eid == idx[:, k:k + 1] for k in range(TOP_K)]
    chosen = jnp.where(hits[0] | hits[1] | hits[2] | hits[3], 1.0, 0.0)
    incl = _dot(tri_ref[...], chosen.astype(BF16))
    slot = off_ref[0].astype(F32) + incl - chosen
    w = w_ref[...]
    pos = lax.broadcasted_iota(jnp.int32, (tn, n_sorted), 1)
    weights = jnp.zeros((tn, n_sorted), F32)
    for k in range(TOP_K):
        lpos = jnp.sum(jnp.where(hits[k], slot, 0.0), axis=1, keepdims=True).astype(jnp.int32)
        weights = jnp.where(pos == lpos, w[:, k:k + 1], weights)
    w_hi = weights.astype(BF16)
    w_lo = (weights - w_hi.astype(F32)).astype(BF16)

    runs(tab_ref, cur, False)
    row_id = lax.broadcasted_iota(jnp.int32, (n_sorted, 1), 0)
    y = jnp.where(row_id < tab_ref[0, 0, N_EXPERTS - 1] + tab_ref[0, 0, 2 * N_EXPERTS - 1], sorted_ref[cur], 0.0)
    y_hi = y.astype(BF16)
    y_lo = (y - y_hi.astype(F32)).astype(BF16)
    out_ref[...] = x1_ref[...] + _dot(w_hi, y_hi) + _dot(w_hi, y_lo) + _dot(w_lo, y_hi)


def _combine(tab, idx_tok, w_tok, off_row, tri, x1, ys, n_sorted, tn):
    t, d = x1.shape
    return pl.pallas_call(
        _combine_kernel,
        grid=(t // tn,),
        in_specs=[
            pl.BlockSpec((1, 1, 3 * N_EXPERTS), lambda i: (i, 0, 0), memory_space=pltpu.SMEM),
            pl.BlockSpec((1, 1, 3 * N_EXPERTS), lambda i: (jnp.minimum(i + 1, t // tn - 1), 0, 0),
                         memory_space=pltpu.SMEM),
            pl.BlockSpec((tn, TOP_K), lambda i: (i, 0)),
            pl.BlockSpec((tn, TOP_K), lambda i: (i, 0)),
            pl.BlockSpec((1, 1, N_EXPERTS), lambda i: (i, 0, 0)),
            pl.BlockSpec((tn, tn), lambda i: (0, 0)),
            pl.BlockSpec((tn, d), lambda i: (i, 0)),
            pl.BlockSpec(memory_space=pl.ANY),
        ],
        out_specs=pl.BlockSpec((tn, d), lambda i: (i, 0)),
        out_shape=jax.ShapeDtypeStruct((t, d), F32),
        scratch_shapes=[pltpu.VMEM((2, n_sorted, d), F32), pltpu.SemaphoreType.DMA((2,))],
        compiler_params=_cparams(("arbitrary",)),
        name="combine",
    )(tab, tab, idx_tok, w_tok, off_row, tri, x1, ys)


def _pad_heads(w, n_heads):
    d = w.shape[0]
    w = w.reshape(d, n_heads, HEAD_DIM)
    return jnp.pad(w, ((0, 0), (0, 0), (0, LANES - HEAD_DIM))).reshape(d, n_heads * LANES)


def _rope_tables(seq_len):
    rows = seq_len // GRID_W
    inv_freq = ROPE_THETA ** (-jnp.arange(0, ROPE_AXIS_DIM, 2, dtype=F32) / ROPE_AXIS_DIM)
    ang_r = jnp.arange(rows, dtype=F32)[:, None] * inv_freq
    ang_c = jnp.arange(GRID_W, dtype=F32)[:, None] * inv_freq
    expand_r = lambda a: jnp.repeat(a, GRID_W, axis=0)
    expand_c = lambda a: jnp.tile(a, (rows, 1))
    cr, sr = expand_r(jnp.cos(ang_r)), expand_r(jnp.sin(ang_r))
    cc, sc = expand_c(jnp.cos(ang_c)), expand_c(jnp.sin(ang_c))
    zeros = jnp.zeros_like(cr)
    pad = jnp.zeros((seq_len, LANES - HEAD_DIM), F32)
    cos = jnp.concatenate([cr, cr, cc, cc, pad], axis=1)
    s_up = jnp.concatenate([-sr, zeros, -sc, zeros, pad], axis=1)
    s_dn = jnp.concatenate([zeros, sr, zeros, sc, pad], axis=1)
    cos_t = jnp.concatenate([cr, cr, cc, cc], axis=1).T
    sin_t = jnp.concatenate([-sr, sr, -sc, sc], axis=1).T
    return cos, s_up, s_dn, cos_t, sin_t


def _tiles(seq_len, n_tok):
    return dict(
        tm_in=min(512, seq_len),
        tq=min(256, seq_len),
        tk=min(512, seq_len // 2),
        tm_out=min(512, seq_len),
        tn_route=min(256, n_tok),
        bm=256,
    )


def _layer(x, p, i, tabs, tl):
    b, l, d = x.shape
    t = b * l
    w_in = p["w_in"][i]
    o_q, o_k, o_v, o_z, o_xbc = ATTN_WIDTH, ATTN_WIDTH + KV_WIDTH, ATTN_WIDTH + 2 * KV_WIDTH, \
        ATTN_WIDTH + 2 * KV_WIDTH + SSM_WIDTH, ATTN_WIDTH + 2 * KV_WIDTH + SSM_WIDTH + XBC_WIDTH
    wq_t = w_in[:, :o_q].T.astype(BF16)
    wk = _pad_heads(w_in[:, o_q:o_k], N_KV).astype(BF16)
    wv_t = w_in[:, o_k:o_v].T.astype(BF16)
    wz = w_in[:, o_v:o_z].astype(BF16)
    wxbc = w_in[:, o_z:o_xbc].astype(BF16)
    wdt = w_in[:, o_xbc:].astype(BF16)
    q4, k4, v4, z, xbc, dt, dt_t = _inproj(
        x, p["norm_mix"][i].reshape(1, d), wq_t, wk, wv_t, wz, wxbc, wdt, wdt.T,
        p["q_norm"][i].reshape(HEAD_DIM, 1), jnp.pad(p["k_norm"][i], (0, LANES - HEAD_DIM)).reshape(1, LANES),
        tabs, tl["tm_in"])

    attn = _attention(q4, k4, v4, tl["tq"], tl["tk"])

    dt_dir = dt.reshape(b, l, 2, SSM_HEADS).transpose(2, 0, 1, 3)
    dtt_dir = dt_t.reshape(b, 2, SSM_HEADS, l).transpose(1, 0, 2, 3)
    ssd_common = (xbc, dt_dir, dtt_dir, p["conv_w"][i], p["conv_b"][i].reshape(1, XBC_WIDTH),
                  p["dt_bias"][i].reshape(2, 1, SSM_HEADS), p["dt_bias"][i].reshape(2, SSM_HEADS, 1),
                  p["a_log"][i].reshape(2, 1, SSM_HEADS), p["a_log"][i].reshape(2, SSM_HEADS, 1))
    y_fwd = _ssd(*ssd_common, 0)
    ssm = _ssd(*ssd_common, 1, y_fwd=y_fwd, z=z,
               dskip=jnp.repeat(p["d_skip"][i], SSM_HEAD_DIM).reshape(1, SSM_WIDTH),
               gain=p["ssm_norm"][i].reshape(1, SSM_WIDTH))

    w_out = p["w_out"][i]
    wa = jnp.pad(w_out[:ATTN_WIDTH].reshape(N_HEADS, HEAD_DIM, d),
                 ((0, 0), (0, LANES - HEAD_DIM), (0, 0))).reshape(Q_PAD, d).astype(BF16)
    ws = w_out[ATTN_WIDTH:].astype(BF16)
    again = jnp.pad(p["attn_norm"][i].reshape(N_HEADS, HEAD_DIM),
                    ((0, 0), (0, LANES - HEAD_DIM))).reshape(1, Q_PAD)
    wr_t = p["w_router"][i].T
    wr_hi = wr_t.astype(BF16)
    wr_lo = (wr_t - wr_hi.astype(F32)).astype(BF16)
    x1, hn, logits_t = _outproj(attn, ssm.reshape(t, SSM_WIDTH), x.reshape(t, d), again, wa, ws,
                                p["norm_ffn"][i].reshape(1, d), wr_hi, wr_lo, tl["tm_out"])

    tn = tl["tn_route"]
    idx_t, w_t, rank_t, counts, tile_counts = _route(logits_t, p["b_router"][i].reshape(N_EXPERTS, 1), tn)
    del rank_t

    bm = tl["bm"]
    nt = t // tn
    del counts
    tile_counts = tile_counts.reshape(nt, N_EXPERTS)
    run_len = (tile_counts + RUN_ALIGN - 1) // RUN_ALIGN * RUN_ALIGN
    run_off = jnp.cumsum(run_len, axis=1) - run_len
    rows_e = jnp.sum(run_len, axis=0)
    padded = (rows_e + bm - 1) // bm * bm
    pad_end = jnp.cumsum(padded)
    pad_start = pad_end - padded
    run_row = pad_start[None, :] + jnp.cumsum(run_len, axis=0) - run_len
    n_sorted = TOP_K * tn + N_EXPERTS * RUN_ALIGN
    n_rows = t * TOP_K + nt * N_EXPERTS * RUN_ALIGN + N_EXPERTS * bm
    nb = n_rows // bm
    block_row = jnp.arange(nb, dtype=jnp.int32) * bm
    block_e = jnp.minimum(jnp.sum(pad_end[None, :] <= block_row[:, None], axis=1), N_EXPERTS - 1).astype(jnp.int32)
    nused = (pad_end[-1:] // bm).astype(jnp.int32)
    has_rows = padded > 0
    order = jnp.cumsum(has_rows.astype(jnp.int32)) - 1
    expert_ids = jnp.arange(N_EXPERTS, dtype=jnp.int32)
    later = jnp.where(has_rows[None, :] & (expert_ids[None, :] > expert_ids[:, None]), expert_ids[None, :], N_EXPERTS)
    next_rows = jnp.min(later, axis=1)
    next_rows = jnp.where(next_rows < N_EXPERTS, next_rows, -1).astype(jnp.int32)
    block_slot = (order[block_e] % 2).astype(jnp.int32)
    block_next = next_rows[block_e]
    seg_lo = jnp.concatenate([pad_start + rows_e, pad_end[-1:]]).astype(jnp.int32)
    seg_hi = jnp.concatenate([pad_end, jnp.full((1,), n_rows, jnp.int32)]).astype(jnp.int32)
    tab = jnp.concatenate([run_off, run_len, run_row], axis=1).astype(jnp.int32).reshape(nt, 1, 3 * N_EXPERTS)
    ones = jnp.ones((tn, tn), BF16)

    xs = _dispatch(seg_lo, seg_hi, tab, idx_t, run_off.reshape(nt, N_EXPERTS, 1).astype(jnp.int32),
                   jnp.triu(ones), hn, n_rows, n_sorted, bm, tn)
    n_all = p["w_gate"].shape[0] * N_EXPERTS
    ys = _experts(block_e, block_slot, block_next, nused, xs,
                  p["w_gate"].reshape(n_all, d, d), p["b_gate"].reshape(n_all, 1, d),
                  p["w_up"].reshape(n_all, d, d), p["b_up"].reshape(n_all, 1, d),
                  p["w_down"].reshape(n_all, d, d), p["b_down"].reshape(n_all, 1, d), bm, i)
    x2 = _combine(tab, idx_t.T, w_t.T, run_off.reshape(nt, 1, N_EXPERTS).astype(jnp.int32),
                  jnp.tril(ones), x1, ys, n_sorted, tn)
    return x2.reshape(b, l, d)


_PARAM_NAMES = ("norm_mix", "w_in", "q_norm", "k_norm", "conv_w", "conv_b", "dt_bias", "a_log", "d_skip",
                "ssm_norm", "attn_norm", "w_out", "norm_ffn", "w_router", "b_router", "w_gate", "b_gate",
                "w_up", "b_up", "w_down", "b_down")


def kernel(x, norm_mix, w_in, q_norm, k_norm, conv_w, conv_b, dt_bias, a_log, d_skip, ssm_norm, attn_norm,
           w_out, norm_ffn, w_router, b_router, w_gate, b_gate, w_up, b_up, w_down, b_down):
    params = dict(zip(_PARAM_NAMES, (norm_mix, w_in, q_norm, k_norm, conv_w, conv_b, dt_bias, a_log, d_skip,
                                     ssm_norm, attn_norm, w_out, norm_ffn, w_router, b_router, w_gate, b_gate,
                                     w_up, b_up, w_down, b_down)))
    b, l, _ = x.shape
    tabs = _rope_tables(l)
    tl = _tiles(l, b * l)
    for i in range(norm_mix.shape[0]):
        x = _layer(x, params, i, tabs, tl)
    return x
```

```python
import functools
import math

import jax
import jax.numpy as jnp
from jax import lax
from jax.experimental import pallas as pl
from jax.experimental.pallas import tpu as pltpu

F32 = jnp.float32
BF16 = jnp.bfloat16

D_MODEL = 1024
GRID_W = 64
HEAD_DIM = 64
N_HEADS = 8
N_KV = 2
HEADS_PER_KV = N_HEADS // N_KV
ATTN_WIDTH = N_HEADS * HEAD_DIM
KV_WIDTH = N_KV * HEAD_DIM
SSM_WIDTH = 512
SSM_HEADS = 8
SSM_HEAD_DIM = 64
SSM_GROUPS = 2
SSM_HEADS_PER_GROUP = SSM_HEADS // SSM_GROUPS
SSM_STATE = 128
CHUNK = 128
CONV_K = 5
XBC_WIDTH = SSM_WIDTH + 2 * SSM_GROUPS * SSM_STATE
ROPE_THETA = 10000.0
ROPE_AXIS_DIM = HEAD_DIM // 2
N_EXPERTS = 32
TOP_K = 4
SWIGLU_LIMIT = 7.0
SWIGLU_ALPHA = 1.702
EPS = 1e-6

LANES = 128
SUBLANES = 8
VMEM_LIMIT = 56 * 1024 * 1024

Q_PAD = N_HEADS * LANES
ATTN_SCORE_BUFFERS = 2
ATTN_ACC_ROWS = HEAD_DIM + 2 * SUBLANES


def _cparams(sem):
    return pltpu.CompilerParams(dimension_semantics=sem, vmem_limit_bytes=VMEM_LIMIT)


def _dot(a, b):
    return jnp.dot(a, b, preferred_element_type=F32)


def _dot_nt(a, b):
    return lax.dot_general(a, b, (((1,), (1,)), ((), ())), preferred_element_type=F32)


def _split3(x):
    hi = x.astype(BF16)
    r1 = x - hi.astype(F32)
    mid = r1.astype(BF16)
    lo = (r1 - mid.astype(F32)).astype(BF16)
    return hi, mid, lo


def _sigmoid(x):
    return 1.0 / (1.0 + jnp.exp(-x))


def _softplus(x):
    return jnp.maximum(x, 0.0) + jnp.log(1.0 + jnp.exp(-jnp.abs(x)))


def _inproj_kernel(x_ref, g_ref, wqT_ref, wk_ref, wvT_ref, wz_ref, wxbc_ref, wdt_ref, wdtT_ref,
                   qg_ref, kg_ref, cos_ref, s1_ref, s2_ref, cosT_ref, sinT_ref,
                   q_ref, k_ref, v_ref, z_ref, xbc_ref, dt_ref, dtT_ref):
    x = x_ref[0]
    tm = x.shape[0]
    hn = (x * lax.rsqrt(jnp.mean(x * x, axis=-1, keepdims=True) + EPS) * g_ref[...]).astype(BF16)

    cos_t = cosT_ref[...]
    sin_t = sinT_ref[...]
    q_scale = math.log2(math.e) / math.sqrt(HEAD_DIM)
    acc_qt = _dot_nt(wqT_ref[...], hn)
    half = ROPE_AXIS_DIM // 2
    for h in range(N_HEADS):
        a = acc_qt[HEAD_DIM * h:HEAD_DIM * (h + 1)]
        r = lax.rsqrt(jnp.sum(a * a, axis=0, keepdims=True) * (1.0 / HEAD_DIM) + EPS)
        y = a * r * qg_ref[...]
        swapped = jnp.concatenate([y[half:2 * half], y[0:half], y[3 * half:4 * half], y[2 * half:3 * half]], axis=0)
        y = (y * cos_t + swapped * sin_t) * q_scale
        q_ref[0, h, 0:HEAD_DIM, :] = y.astype(BF16)
        q_ref[0, h, HEAD_DIM:, :] = jnp.zeros((LANES - HEAD_DIM, tm), BF16)

    cos = cos_ref[...]
    s1 = s1_ref[...]
    s2 = s2_ref[...]
    acc_k = _dot(hn, wk_ref[...])
    for g in range(N_KV):
        a = acc_k[:, LANES * g:LANES * (g + 1)]
        r = lax.rsqrt(jnp.sum(a * a, axis=-1, keepdims=True) * (1.0 / HEAD_DIM) + EPS)
        y = a * r * kg_ref[...]
        y = y * cos + pltpu.roll(y, LANES - 16, 1) * s1 + pltpu.roll(y, 16, 1) * s2
        k_ref[0, g] = y.astype(BF16)

    acc_vt = _dot_nt(wvT_ref[...], hn)
    ones_row = jnp.where(lax.broadcasted_iota(jnp.int32, (LANES - HEAD_DIM, tm), 0) == 0, 1.0, 0.0).astype(BF16)
    for g in range(N_KV):
        v_ref[0, g, 0:HEAD_DIM, :] = acc_vt[HEAD_DIM * g:HEAD_DIM * (g + 1)].astype(BF16)
        v_ref[0, g, HEAD_DIM:, :] = ones_row

    z_ref[0] = _dot(hn, wz_ref[...])
    xbc_ref[0] = _dot(hn, wxbc_ref[...])
    dt_ref[0] = _dot(hn, wdt_ref[...])
    dtT_ref[0] = _dot_nt(wdtT_ref[...], hn)


def _inproj(x, gain, wqT, wk, wvT, wz, wxbc, wdt, wdtT, qg, kg, tabs, tm):
    b, l, d = x.shape
    nt = l // tm
    const = lambda shape: pl.BlockSpec(shape, lambda bi, i: (0,) * len(shape))
    tab = pl.BlockSpec((tm, LANES), lambda bi, i: (i, 0))
    tab_t = pl.BlockSpec((HEAD_DIM, tm), lambda bi, i: (0, i))
    return pl.pallas_call(
        _inproj_kernel,
        grid=(b, nt),
        in_specs=[
            pl.BlockSpec((1, tm, d), lambda bi, i: (bi, i, 0)),
            const((1, d)), const(wqT.shape), const(wk.shape), const(wvT.shape), const(wz.shape),
            const(wxbc.shape), const(wdt.shape), const(wdtT.shape), const((HEAD_DIM, 1)), const((1, LANES)),
            tab, tab, tab, tab_t, tab_t,
        ],
        out_specs=[
            pl.BlockSpec((1, N_HEADS, LANES, tm), lambda bi, i: (bi, 0, 0, i)),
            pl.BlockSpec((1, N_KV, tm, LANES), lambda bi, i: (bi, 0, i, 0)),
            pl.BlockSpec((1, N_KV, LANES, tm), lambda bi, i: (bi, 0, 0, i)),
            pl.BlockSpec((1, tm, SSM_WIDTH), lambda bi, i: (bi, i, 0)),
            pl.BlockSpec((1, tm, XBC_WIDTH), lambda bi, i: (bi, i, 0)),
            pl.BlockSpec((1, tm, 2 * SSM_HEADS), lambda bi, i: (bi, i, 0)),
            pl.BlockSpec((1, 2 * SSM_HEADS, tm), lambda bi, i: (bi, 0, i)),
        ],
        out_shape=[
            jax.ShapeDtypeStruct((b, N_HEADS, LANES, l), BF16),
            jax.ShapeDtypeStruct((b, N_KV, l, LANES), BF16),
            jax.ShapeDtypeStruct((b, N_KV, LANES, l), BF16),
            jax.ShapeDtypeStruct((b, l, SSM_WIDTH), F32),
            jax.ShapeDtypeStruct((b, l, XBC_WIDTH), F32),
            jax.ShapeDtypeStruct((b, l, 2 * SSM_HEADS), F32),
            jax.ShapeDtypeStruct((b, 2 * SSM_HEADS, l), F32),
        ],
        compiler_params=_cparams(("parallel", "parallel")),
        name="inproj",
    )(x, gain, wqT, wk, wvT, wz, wxbc, wdt, wdtT, qg, kg, *tabs)


def _attn_kernel(q_ref, k_ref, v_ref, o_ref, qcat_ref, m_ref, acc_ref, *bufs, tk):
    tq = q_ref.shape[3]
    seq = k_ref.shape[2]
    cols = HEADS_PER_KV * tq
    n_chunks = seq // tk
    for hh in range(HEADS_PER_KV):
        qcat_ref[:, hh * tq:(hh + 1) * tq] = q_ref[0, hh]
    m_ref[...] = jnp.full((1, cols), -jnp.inf, F32)
    acc_ref[...] = jnp.zeros(acc_ref.shape, F32)

    s_refs, top_refs = bufs[:len(bufs) // 2], bufs[len(bufs) // 2:]

    def scores(j, u):
        off = pl.multiple_of(j * tk, tk)
        s = _dot(k_ref[0, 0, pl.ds(off, tk), :], qcat_ref[...])
        s_refs[u][...] = s
        top_refs[u][...] = jnp.max(s, axis=0, keepdims=True)

    def accumulate(j, u):
        off = pl.multiple_of(j * tk, tk)
        vc = v_ref[0, 0, 0:ATTN_ACC_ROWS, pl.ds(off, tk)]
        s = s_refs[u][...]
        m_old = m_ref[...]
        m_new = jnp.maximum(m_old, top_refs[u][...])
        p = jnp.exp2((s - m_new).astype(BF16))
        acc_ref[...] = jnp.exp2(m_old - m_new) * acc_ref[...] + _dot(vc, p)
        m_ref[...] = m_new

    nbuf = len(s_refs)
    scores(0, 0)

    def body(i, carry):
        j = nbuf * i
        for u in range(nbuf):
            scores(jnp.minimum(j + u + 1, n_chunks - 1), (u + 1) % nbuf)
            accumulate(j + u, u)
        return carry

    lax.fori_loop(0, n_chunks // nbuf, body, 0)
    acc = acc_ref[...]
    out = acc[0:HEAD_DIM] / acc[HEAD_DIM:HEAD_DIM + 1]
    out = jnp.concatenate([out, jnp.zeros((LANES - HEAD_DIM, cols), F32)], axis=0)
    for hh in range(HEADS_PER_KV):
        o_ref[0, hh] = out[:, hh * tq:(hh + 1) * tq].T


def _attention(q, k, v, tq, tk):
    b, _, _, l = q.shape
    cols = HEADS_PER_KV * tq
    nbuf = min(ATTN_SCORE_BUFFERS, l // tk)
    return pl.pallas_call(
        functools.partial(_attn_kernel, tk=tk),
        grid=(b, N_KV, l // tq),
        in_specs=[
            pl.BlockSpec((1, HEADS_PER_KV, LANES, tq), lambda bi, g, i: (bi, g, 0, i)),
            pl.BlockSpec((1, 1, l, LANES), lambda bi, g, i: (bi, g, 0, 0)),
            pl.BlockSpec((1, 1, LANES, l), lambda bi, g, i: (bi, g, 0, 0)),
        ],
        out_specs=pl.BlockSpec((1, HEADS_PER_KV, tq, LANES), lambda bi, g, i: (bi, g, i, 0)),
        out_shape=jax.ShapeDtypeStruct((b, N_HEADS, l, LANES), F32),
        scratch_shapes=[
            pltpu.VMEM((LANES, cols), BF16),
            pltpu.VMEM((1, cols), F32),
            pltpu.VMEM((ATTN_ACC_ROWS, cols), F32),
        ] + [pltpu.VMEM((tk, cols), F32)] * nbuf + [pltpu.VMEM((1, cols), F32)] * nbuf,
        compiler_params=_cparams(("parallel", "parallel", "parallel")),
        name="attn",
    )(q, k, v)


def _ssd_kernel(*refs, reverse):
    if reverse:
        (cur_ref, prev_ref, next_ref, dt_ref, dtT_ref, convw_ref, convb_ref, dtb_row_ref, dtb_col_ref,
         alog_row_ref, alog_col_ref, yin_ref, z_ref, dskip_ref, gain_ref,
         y_ref, pad_ref, state_ref) = refs
    else:
        (cur_ref, prev_ref, next_ref, dt_ref, dtT_ref, convw_ref, convb_ref, dtb_row_ref, dtb_col_ref,
         alog_row_ref, alog_col_ref, y_ref, pad_ref, state_ref) = refs
    c = pl.program_id(1)
    nc = pl.num_programs(1)
    chunk = (nc - 1 - c) if reverse else c

    @pl.when(c == 0)
    def _():
        state_ref[...] = jnp.zeros(state_ref.shape, F32)

    halo = SUBLANES
    pad_ref[0:halo, :] = jnp.where(chunk > 0, prev_ref[0], 0.0)
    pad_ref[halo:halo + CHUNK, :] = cur_ref[0]
    pad_ref[halo + CHUNK:, :] = jnp.where(chunk < nc - 1, next_ref[0], 0.0)
    padded = pad_ref[...]
    rows_padded = CHUNK + 2 * halo
    conv = jnp.broadcast_to(convb_ref[...], (CHUNK, XBC_WIDTH))
    for j in range(CONV_K):
        shifted = pltpu.roll(padded, (CONV_K // 2 - j) % rows_padded, 0) if j != CONV_K // 2 else padded
        conv = conv + convw_ref[j:j + 1, :] * shifted[halo:halo + CHUNK]
    xc = conv * _sigmoid(conv)
    xs = xc[:, :SSM_WIDTH]

    dt_col = _softplus(dt_ref[0, 0] + dtb_row_ref[0])
    dt_row = _softplus(dtT_ref[0, 0] + dtb_col_ref[0])
    da_col = dt_col * -jnp.exp(alog_row_ref[0])
    da_row = dt_row * -jnp.exp(alog_col_ref[0])

    ri = lax.broadcasted_iota(jnp.int32, (CHUNK, CHUNK), 0)
    ci = lax.broadcasted_iota(jnp.int32, (CHUNK, CHUNK), 1)
    low = ci <= ri
    up = ci >= ri
    low_m = jnp.where(low, 1.0, 0.0).astype(BF16)
    up_m = jnp.where(up, 1.0, 0.0).astype(BF16)
    col_m, row_m, mask = (up_m, low_m, up) if reverse else (low_m, up_m, low)
    ch, cm, cl = _split3(da_col)
    acum_col = _dot(col_m, ch) + _dot(col_m, cm) + _dot(col_m, cl)
    rh, rm, rl = _split3(da_row)
    acum_row = _dot(rh, row_m) + _dot(rm, row_m) + _dot(rl, row_m)
    end = 0 if reverse else CHUNK - 1
    a_end_row = acum_col[end:end + 1, :]
    w_col = dt_col * jnp.exp(a_end_row - acum_col)
    e_col = jnp.exp(acum_col)
    chunk_decay = jnp.exp(a_end_row)

    pieces = []
    for g in range(SSM_GROUPS):
        bm = xc[:, SSM_WIDTH + g * SSM_STATE:SSM_WIDTH + (g + 1) * SSM_STATE]
        cmat = xc[:, SSM_WIDTH + (SSM_GROUPS + g) * SSM_STATE:SSM_WIDTH + (SSM_GROUPS + g + 1) * SSM_STATE]
        bm16 = bm.astype(BF16)
        cm16 = cmat.astype(BF16)
        cb = _dot_nt(cm16, bm16)
        bt16 = bm.T.astype(BF16)
        for hh in range(SSM_HEADS_PER_GROUP):
            h = g * SSM_HEADS_PER_GROUP + hh
            xh = xs[:, h * SSM_HEAD_DIM:(h + 1) * SSM_HEAD_DIM]
            seg = acum_col[:, h:h + 1] - acum_row[h:h + 1, :]
            decay = jnp.exp(jnp.where(mask, seg, -jnp.inf))
            mat = cb * decay * dt_row[h:h + 1, :]
            y_diag = _dot(mat.astype(BF16), xh.astype(BF16))
            prev_state = state_ref[h]
            y_off = _dot(cm16, prev_state.astype(BF16)) * e_col[:, h:h + 1]
            pieces.append(y_diag + y_off)
            xw = xh * w_col[:, h:h + 1]
            state_ref[h] = prev_state * chunk_decay[:, h:h + 1] + _dot(bt16, xw.astype(BF16))
    y = jnp.concatenate(pieces, axis=1)

    if not reverse:
        y_ref[0] = y
    else:
        y = y + yin_ref[0] + xs * dskip_ref[...]
        zz = z_ref[0]
        y = y * (zz * _sigmoid(zz))
        gw = SSM_WIDTH // SSM_GROUPS
        outs = []
        for g in range(SSM_GROUPS):
            yg = y[:, g * gw:(g + 1) * gw]
            outs.append(yg * lax.rsqrt(jnp.mean(yg * yg, axis=-1, keepdims=True) + EPS))
        y_ref[0] = jnp.concatenate(outs, axis=1) * gain_ref[...]


def _ssd(xbc, dt, dtT, convw, convb, dtb_row, dtb_col, alog_row, alog_col, direction,
         y_fwd=None, z=None, dskip=None, gain=None):
    b, l, _ = xbc.shape
    nc = l // CHUNK
    reverse = direction == 1
    blocks_per_chunk = CHUNK // SUBLANES
    nblk = l // SUBLANES

    def pos(c):
        return (nc - 1 - c) if reverse else c

    const = lambda shape: pl.BlockSpec(shape, lambda bi, c: (0,) * len(shape))
    dsel = lambda shape: pl.BlockSpec(shape, lambda bi, c: (direction,) + (0,) * (len(shape) - 1))
    in_specs = [
        pl.BlockSpec((1, CHUNK, XBC_WIDTH), lambda bi, c: (bi, pos(c), 0)),
        pl.BlockSpec((1, SUBLANES, XBC_WIDTH),
                     lambda bi, c: (bi, jnp.maximum(pos(c) * blocks_per_chunk - 1, 0), 0)),
        pl.BlockSpec((1, SUBLANES, XBC_WIDTH),
                     lambda bi, c: (bi, jnp.minimum((pos(c) + 1) * blocks_per_chunk, nblk - 1), 0)),
        pl.BlockSpec((1, 1, CHUNK, SSM_HEADS), lambda bi, c: (direction, bi, pos(c), 0)),
        pl.BlockSpec((1, 1, SSM_HEADS, CHUNK), lambda bi, c: (direction, bi, 0, pos(c))),
        const((CONV_K, XBC_WIDTH)), const((1, XBC_WIDTH)),
        dsel((1, 1, SSM_HEADS)), dsel((1, SSM_HEADS, 1)), dsel((1, 1, SSM_HEADS)), dsel((1, SSM_HEADS, 1)),
    ]
    args = [xbc, xbc, xbc, dt, dtT, convw, convb, dtb_row, dtb_col, alog_row, alog_col]
    if reverse:
        in_specs += [
            pl.BlockSpec((1, CHUNK, SSM_WIDTH), lambda bi, c: (bi, pos(c), 0)),
            pl.BlockSpec((1, CHUNK, SSM_WIDTH), lambda bi, c: (bi, pos(c), 0)),
            const((1, SSM_WIDTH)), const((1, SSM_WIDTH)),
        ]
        args += [y_fwd, z, dskip, gain]
    return pl.pallas_call(
        functools.partial(_ssd_kernel, reverse=reverse),
        grid=(b, nc),
        in_specs=in_specs,
        out_specs=pl.BlockSpec((1, CHUNK, SSM_WIDTH), lambda bi, c: (bi, pos(c), 0)),
        out_shape=jax.ShapeDtypeStruct((b, l, SSM_WIDTH), F32),
        scratch_shapes=[
            pltpu.VMEM((CHUNK + 2 * SUBLANES, XBC_WIDTH), F32),
            pltpu.VMEM((SSM_HEADS, SSM_STATE, SSM_HEAD_DIM), F32),
        ],
        compiler_params=_cparams(("parallel", "arbitrary")),
        name="ssd_bwd" if reverse else "ssd_fwd",
    )(*args)


def _outproj_kernel(attn_ref, ssm_ref, x_ref, again_ref, wa_ref, ws_ref, fgain_ref, wr_hi_ref, wr_lo_ref,
                    x1_ref, hn_ref, logit_ref):
    a = jnp.concatenate([attn_ref[0, h] for h in range(N_HEADS)], axis=1)
    r = lax.rsqrt(jnp.sum(a * a, axis=-1, keepdims=True) * (1.0 / ATTN_WIDTH) + EPS)
    an = (a * r * again_ref[...]).astype(BF16)
    x1 = x_ref[...] + _dot(an, wa_ref[...]) + _dot(ssm_ref[...].astype(BF16), ws_ref[...])
    x1_ref[...] = x1
    hn = x1 * lax.rsqrt(jnp.mean(x1 * x1, axis=-1, keepdims=True) + EPS) * fgain_ref[...]
    hn_ref[...] = hn
    h_hi = hn.astype(BF16)
    h_lo = (hn - h_hi.astype(F32)).astype(BF16)
    w_hi = wr_hi_ref[...]
    logit_ref[...] = _dot_nt(w_hi, h_hi) + _dot_nt(w_hi, h_lo) + _dot_nt(wr_lo_ref[...], h_hi)


def _outproj(attn, ssm, x, again, wa, ws, fgain, wr_hi, wr_lo, tm):
    b, _, l, _ = attn.shape
    t = b * l
    nt = l // tm
    const = lambda shape: pl.BlockSpec(shape, lambda i: (0,) * len(shape))
    return pl.pallas_call(
        _outproj_kernel,
        grid=(t // tm,),
        in_specs=[
            pl.BlockSpec((1, N_HEADS, tm, LANES), lambda i: (i // nt, 0, i % nt, 0)),
            pl.BlockSpec((tm, SSM_WIDTH), lambda i: (i, 0)),
            pl.BlockSpec((tm, D_MODEL), lambda i: (i, 0)),
            const((1, Q_PAD)), const(wa.shape), const(ws.shape), const((1, D_MODEL)),
            const(wr_hi.shape), const(wr_lo.shape),
        ],
        out_specs=[
            pl.BlockSpec((tm, D_MODEL), lambda i: (i, 0)),
            pl.BlockSpec((tm, D_MODEL), lambda i: (i, 0)),
            pl.BlockSpec((N_EXPERTS, tm), lambda i: (0, i)),
        ],
        out_shape=[
            jax.ShapeDtypeStruct((t, D_MODEL), F32),
            jax.ShapeDtypeStruct((t, D_MODEL), F32),
            jax.ShapeDtypeStruct((N_EXPERTS, t), F32),
        ],
        compiler_params=_cparams(("parallel",)),
        name="outproj",
    )(attn, ssm, x, again, wa, ws, fgain, wr_hi, wr_lo)


def _route_kernel(logit_ref, bias_ref, idx_ref, w_ref, rank_ref, count_ref, tile_ref, carry_ref):
    i = pl.program_id(0)
    tn = logit_ref.shape[1]

    @pl.when(i == 0)
    def _():
        carry_ref[...] = jnp.zeros(carry_ref.shape, F32)

    lg = logit_ref[...] + bias_ref[...]
    eid = lax.broadcasted_iota(jnp.int32, (N_EXPERTS, tn), 0).astype(F32)
    chosen = jnp.zeros((N_EXPERTS, tn), F32)
    vals, ids = [], []
    for _ in range(TOP_K):
        mx = jnp.max(lg, axis=0, keepdims=True)
        sel = jnp.min(jnp.where(lg == mx, eid, float(N_EXPERTS)), axis=0, keepdims=True)
        hit = eid == sel
        chosen = jnp.where(hit, 1.0, chosen)
        lg = jnp.where(hit, -jnp.inf, lg)
        vals.append(mx)
        ids.append(sel)
    ex = [jnp.exp(v - vals[0]) for v in vals]
    den = ex[0] + ex[1] + ex[2] + ex[3]
    ri = lax.broadcasted_iota(jnp.int32, (tn, tn), 0)
    ci = lax.broadcasted_iota(jnp.int32, (tn, tn), 1)
    upper = jnp.where(ri <= ci, 1.0, 0.0).astype(BF16)
    incl = _dot(chosen.astype(BF16), upper)
    rank_e = incl - chosen + carry_ref[...]
    carry_ref[...] = carry_ref[...] + incl[:, tn - 1:tn]
    for k in range(TOP_K):
        idx_ref[k:k + 1, :] = ids[k].astype(jnp.int32)
        w_ref[k:k + 1, :] = ex[k] / den
        rank_ref[k:k + 1, :] = jnp.sum(jnp.where(eid == ids[k], rank_e, 0.0), axis=0,
                                       keepdims=True).astype(jnp.int32)
    count_ref[...] = carry_ref[...].astype(jnp.int32)
    tile_ref[0] = incl[:, tn - 1:tn].astype(jnp.int32)


def _route(logits_t, bias_col, tn):
    t = logits_t.shape[1]
    tok = pl.BlockSpec((TOP_K, tn), lambda i: (0, i))
    return pl.pallas_call(
        _route_kernel,
        grid=(t // tn,),
        in_specs=[pl.BlockSpec((N_EXPERTS, tn), lambda i: (0, i)),
                  pl.BlockSpec((N_EXPERTS, 1), lambda i: (0, 0))],
        out_specs=[tok, tok, tok, pl.BlockSpec((N_EXPERTS, 1), lambda i: (0, 0)),
                   pl.BlockSpec((1, N_EXPERTS, 1), lambda i: (i, 0, 0))],
        out_shape=[
            jax.ShapeDtypeStruct((TOP_K, t), jnp.int32),
            jax.ShapeDtypeStruct((TOP_K, t), F32),
            jax.ShapeDtypeStruct((TOP_K, t), jnp.int32),
            jax.ShapeDtypeStruct((N_EXPERTS, 1), jnp.int32),
            jax.ShapeDtypeStruct((t // tn, N_EXPERTS, 1), jnp.int32),
        ],
        scratch_shapes=[pltpu.VMEM((N_EXPERTS, 1), F32)],
        compiler_params=_cparams(("arbitrary",)),
        name="route",
    )(logits_t, bias_col)


RUN_ALIGN = SUBLANES
RUN_BITS = (256, 128, 64, 32, 16, 8)


def _copy_run(src_ref, dst_ref, src_start, dst_start, count, sem, start=True, wait=True):
    for bit in RUN_BITS:
        take = count & bit

        @pl.when(take != 0)
        def _():
            cp = pltpu.make_async_copy(src_ref.at[pl.ds(pl.multiple_of(src_start, RUN_ALIGN), bit), :],
                                       dst_ref.at[pl.ds(pl.multiple_of(dst_start, RUN_ALIGN), bit), :], sem)
            if start:
                cp.start()
            if wait:
                cp.wait()

        src_start = src_start + take
        dst_start = dst_start + take


def _tile_positions_rows(idx_ref, off_col, tri_ref):
    tn = idx_ref.shape[1]
    eid = lax.broadcasted_iota(jnp.int32, (N_EXPERTS, tn), 0)
    hits = [eid == idx_ref[k:k + 1, :] for k in range(TOP_K)]
    chosen = jnp.where(hits[0] | hits[1] | hits[2] | hits[3], 1.0, 0.0)
    incl = _dot(chosen.astype(BF16), tri_ref[...])
    slot = off_col + incl - chosen
    return [jnp.sum(jnp.where(h, slot, 0.0), axis=0, keepdims=True).astype(jnp.int32) for h in hits]


def _dispatch_kernel(seg_lo_ref, seg_hi_ref, tab_ref, tab_prev_ref, idx_ref, off_ref, tri_ref, hn_ref, xs_ref,
                     sorted_ref, zero_ref, sems, sem, *, bm):
    i = pl.program_id(0)
    last = pl.num_programs(0) - 1
    slot = i % 2
    tn = idx_ref.shape[1]
    n_sorted = sorted_ref.shape[1]

    @pl.when(i == 0)
    def _():
        zero_ref[...] = jnp.zeros(zero_ref.shape, F32)

        def segment(e, carry):
            lo = seg_lo_ref[e]
            count = seg_hi_ref[e] - lo

            def whole(j, c):
                cp = pltpu.make_async_copy(zero_ref, xs_ref.at[pl.ds(pl.multiple_of(lo + j * bm, RUN_ALIGN), bm), :], sem)
                cp.start()
                cp.wait()
                return c

            lax.fori_loop(0, count // bm, whole, 0)
            _copy_run(zero_ref, xs_ref, 0, lo + count // bm * bm, count % bm, sem)
            return carry

        lax.fori_loop(0, N_EXPERTS + 1, segment, 0)

    lpos = _tile_positions_rows(idx_ref, off_ref[0].astype(F32), tri_ref)
    pos = lax.broadcasted_iota(jnp.int32, (n_sorted, tn), 0)
    onehot = (pos == lpos[0]) | (pos == lpos[1]) | (pos == lpos[2]) | (pos == lpos[3])
    sorted_ref[slot] = _dot(jnp.where(onehot, 1.0, 0.0).astype(BF16), hn_ref[...].astype(BF16))

    def runs(tab, buf, start):
        def body(e, carry):
            _copy_run(sorted_ref.at[buf], xs_ref, tab[0, 0, e], tab[0, 0, 2 * N_EXPERTS + e],
                      tab[0, 0, N_EXPERTS + e], sems.at[buf], start=start, wait=not start)
            return carry

        lax.fori_loop(0, N_EXPERTS, body, 0)

    runs(tab_ref, slot, True)

    @pl.when(i > 0)
    def _():
        runs(tab_prev_ref, 1 - slot, False)

    @pl.when(i == last)
    def _():
        runs(tab_ref, slot, False)


def _dispatch(seg_lo, seg_hi, tab, idx_t, off_col, tri, hn, n_rows, n_sorted, bm, tn):
    t, d = hn.shape
    return pl.pallas_call(
        functools.partial(_dispatch_kernel, bm=bm),
        grid_spec=pltpu.PrefetchScalarGridSpec(
            num_scalar_prefetch=2,
            grid=(t // tn,),
            in_specs=[
                pl.BlockSpec((1, 1, 3 * N_EXPERTS), lambda i, lo, hi: (i, 0, 0), memory_space=pltpu.SMEM),
                pl.BlockSpec((1, 1, 3 * N_EXPERTS), lambda i, lo, hi: (jnp.maximum(i - 1, 0), 0, 0),
                             memory_space=pltpu.SMEM),
                pl.BlockSpec((TOP_K, tn), lambda i, lo, hi: (0, i)),
                pl.BlockSpec((1, N_EXPERTS, 1), lambda i, lo, hi: (i, 0, 0)),
                pl.BlockSpec((tn, tn), lambda i, lo, hi: (0, 0)),
                pl.BlockSpec((tn, d), lambda i, lo, hi: (i, 0)),
            ],
            out_specs=pl.BlockSpec(memory_space=pl.ANY),
            scratch_shapes=[pltpu.VMEM((2, n_sorted, d), F32), pltpu.VMEM((bm, d), F32),
                            pltpu.SemaphoreType.DMA((2,)), pltpu.SemaphoreType.DMA(())],
        ),
        out_shape=jax.ShapeDtypeStruct((n_rows, d), F32),
        compiler_params=_cparams(("arbitrary",)),
        name="dispatch",
    )(seg_lo, seg_hi, tab, tab, idx_t, off_col, tri, hn)


def _expert_kernel(block_e_ref, slot_ref, next_e_ref, nused_ref, xs_ref, wg_ref, bg_ref, wu_ref, bu_ref,
                   wd_ref, bd_ref, ys_ref, wbuf, w16, sems, *, base):
    b = pl.program_id(0)
    used = b < nused_ref[0]
    e = block_e_ref[b]
    slot = slot_ref[b]
    prev_e = block_e_ref[jnp.maximum(b - 1, 0)]
    fresh = jnp.logical_and(used, jnp.logical_or(b == 0, e != prev_e))

    def fetch(expert, buf):
        return [pltpu.make_async_copy(w_ref.at[base + expert], wbuf.at[buf, j], sems.at[buf])
                for j, w_ref in enumerate((wg_ref, wu_ref, wd_ref))]

    @pl.when(b == 0)
    def _():
        for cp in fetch(e, slot):
            cp.start()

    @pl.when(fresh)
    def _():
        for cp in fetch(e, slot):
            cp.wait()
        for j in range(3):
            w16[j] = wbuf[slot, j].astype(BF16)
        nxt = next_e_ref[b]

        @pl.when(nxt >= 0)
        def _():
            for cp in fetch(nxt, 1 - slot):
                cp.start()

    @pl.when(used)
    def _():
        x = xs_ref[...].astype(BF16)
        g = _dot(x, w16[0]) + bg_ref[0]
        u = _dot(x, w16[1]) + bu_ref[0]
        g = jnp.minimum(g, SWIGLU_LIMIT)
        u = jnp.clip(u, -SWIGLU_LIMIT, SWIGLU_LIMIT)
        act = g * _sigmoid(SWIGLU_ALPHA * g) * (u + 1.0)
        ys_ref[...] = _dot(act.astype(BF16), w16[2]) + bd_ref[0]

    @pl.when(jnp.logical_not(used))
    def _():
        ys_ref[...] = jnp.zeros(ys_ref.shape, F32)


def _experts(block_e, slot, next_e, nused, xs, wg, bg, wu, bu, wd, bd, bm, layer):
    n_rows, d = xs.shape
    nb = n_rows // bm
    base = layer * N_EXPERTS
    hbm = pl.BlockSpec(memory_space=pl.ANY)
    bspec = pl.BlockSpec((1, 1, d), lambda i, be, sl, ne, nu: (base + be[i], 0, 0))
    row = pl.BlockSpec((bm, d), lambda i, be, sl, ne, nu: (i, 0))
    return pl.pallas_call(
        functools.partial(_expert_kernel, base=base),
        grid_spec=pltpu.PrefetchScalarGridSpec(
            num_scalar_prefetch=4,
            grid=(nb,),
            in_specs=[row, hbm, bspec, hbm, bspec, hbm, bspec],
            out_specs=row,
            scratch_shapes=[pltpu.VMEM((2, 3, d, d), F32), pltpu.VMEM((3, d, d), BF16),
                            pltpu.SemaphoreType.DMA((2,))],
        ),
        out_shape=jax.ShapeDtypeStruct((n_rows, d), F32),
        compiler_params=_cparams(("arbitrary",)),
        name="experts",
    )(block_e, slot, next_e, nused, xs, wg, bg, wu, bu, wd, bd)


def _combine_kernel(tab_ref, tab_next_ref, idx_ref, w_ref, off_ref, tri_ref, x1_ref, ys_ref, out_ref,
                    sorted_ref, sems):
    i = pl.program_id(0)
    last = pl.num_programs(0) - 1
    cur = i % 2
    tn = x1_ref.shape[0]
    n_sorted = sorted_ref.shape[1]

    def runs(tab, buf, start):
        def body(e, carry):
            _copy_run(ys_ref, sorted_ref.at[buf], tab[0, 0, 2 * N_EXPERTS + e], tab[0, 0, e],
                      tab[0, 0, N_EXPERTS + e], sems.at[buf], start=start, wait=not start)
            return carry

        lax.fori_loop(0, N_EXPERTS, body, 0)

    @pl.when(i == 0)
    def _():
        runs(tab_ref, cur, True)

    @pl.when(i < last)
    def _():
        runs(tab_next_ref, 1 - cur, True)

    idx = idx_ref[...]
    eid = lax.broadcasted_iota(jnp.int32, (tn, N_EXPERTS), 1)
    hits = [eid == idx[:, k:k + 1] for k in range(TOP_K)]
    chosen = jnp.where(hits[0] | hits[1] | hits[2] | hits[3], 1.0, 0.0)
    incl = _dot(tri_ref[...], chosen.astype(BF16))
    slot = off_ref[0].astype(F32) + incl - chosen
    w = w_ref[...]
    pos = lax.broadcasted_iota(jnp.int32, (tn, n_sorted), 1)
    weights = jnp.zeros((tn, n_sorted), F32)
    for k in range(TOP_K):
        lpos = jnp.sum(jnp.where(hits[k], slot, 0.0), axis=1, keepdims=True).astype(jnp.int32)
        weights = jnp.where(pos == lpos, w[:, k:k + 1], weights)
    w_hi = weights.astype(BF16)
    w_lo = (weights - w_hi.astype(F32)).astype(BF16)

    runs(tab_ref, cur, False)
    row_id = lax.broadcasted_iota(jnp.int32, (n_sorted, 1), 0)
    y = jnp.where(row_id < tab_ref[0, 0, N_EXPERTS - 1] + tab_ref[0, 0, 2 * N_EXPERTS - 1], sorted_ref[cur], 0.0)
    y_hi = y.astype(BF16)
    y_lo = (y - y_hi.astype(F32)).astype(BF16)
    out_ref[...] = x1_ref[...] + _dot(w_hi, y_hi) + _dot(w_hi, y_lo) + _dot(w_lo, y_hi)


def _combine(tab, idx_tok, w_tok, off_row, tri, x1, ys, n_sorted, tn):
    t, d = x1.shape
    return pl.pallas_call(
        _combine_kernel,
        grid=(t // tn,),
        in_specs=[
            pl.BlockSpec((1, 1, 3 * N_EXPERTS), lambda i: (i, 0, 0), memory_space=pltpu.SMEM),
            pl.BlockSpec((1, 1, 3 * N_EXPERTS), lambda i: (jnp.minimum(i + 1, t // tn - 1), 0, 0),
                         memory_space=pltpu.SMEM),
            pl.BlockSpec((tn, TOP_K), lambda i: (i, 0)),
            pl.BlockSpec((tn, TOP_K), lambda i: (i, 0)),
            pl.BlockSpec((1, 1, N_EXPERTS), lambda i: (i, 0, 0)),
            pl.BlockSpec((tn, tn), lambda i: (0, 0)),
            pl.BlockSpec((tn, d), lambda i: (i, 0)),
            pl.BlockSpec(memory_space=pl.ANY),
        ],
        out_specs=pl.BlockSpec((tn, d), lambda i: (i, 0)),
        out_shape=jax.ShapeDtypeStruct((t, d), F32),
        scratch_shapes=[pltpu.VMEM((2, n_sorted, d), F32), pltpu.SemaphoreType.DMA((2,))],
        compiler_params=_cparams(("arbitrary",)),
        name="combine",
    )(tab, tab, idx_tok, w_tok, off_row, tri, x1, ys)


def _pad_heads(w, n_heads):
    d = w.shape[0]
    w = w.reshape(d, n_heads, HEAD_DIM)
    return jnp.pad(w, ((0, 0), (0, 0), (0, LANES - HEAD_DIM))).reshape(d, n_heads * LANES)


def _rope_tables(seq_len):
    rows = seq_len // GRID_W
    inv_freq = ROPE_THETA ** (-jnp.arange(0, ROPE_AXIS_DIM, 2, dtype=F32) / ROPE_AXIS_DIM)
    ang_r = jnp.arange(rows, dtype=F32)[:, None] * inv_freq
    ang_c = jnp.arange(GRID_W, dtype=F32)[:, None] * inv_freq
    expand_r = lambda a: jnp.repeat(a, GRID_W, axis=0)
    expand_c = lambda a: jnp.tile(a, (rows, 1))
    cr, sr = expand_r(jnp.cos(ang_r)), expand_r(jnp.sin(ang_r))
    cc, sc = expand_c(jnp.cos(ang_c)), expand_c(jnp.sin(ang_c))
    zeros = jnp.zeros_like(cr)
    pad = jnp.zeros((seq_len, LANES - HEAD_DIM), F32)
    cos = jnp.concatenate([cr, cr, cc, cc, pad], axis=1)
    s_up = jnp.concatenate([-sr, zeros, -sc, zeros, pad], axis=1)
    s_dn = jnp.concatenate([zeros, sr, zeros, sc, pad], axis=1)
    cos_t = jnp.concatenate([cr, cr, cc, cc], axis=1).T
    sin_t = jnp.concatenate([-sr, sr, -sc, sc], axis=1).T
    return cos, s_up, s_dn, cos_t, sin_t


def _tiles(seq_len, n_tok):
    return dict(
        tm_in=min(512, seq_len),
        tq=min(256, seq_len),
        tk=min(512, seq_len // 2),
        tm_out=min(512, seq_len),
        tn_route=min(256, n_tok),
        bm=256,
    )


def _layer(x, p, i, tabs, tl):
    b, l, d = x.shape
    t = b * l
    w_in = p["w_in"][i]
    o_q, o_k, o_v, o_z, o_xbc = ATTN_WIDTH, ATTN_WIDTH + KV_WIDTH, ATTN_WIDTH + 2 * KV_WIDTH, \
        ATTN_WIDTH + 2 * KV_WIDTH + SSM_WIDTH, ATTN_WIDTH + 2 * KV_WIDTH + SSM_WIDTH + XBC_WIDTH
    wq_t = w_in[:, :o_q].T.astype(BF16)
    wk = _pad_heads(w_in[:, o_q:o_k], N_KV).astype(BF16)
    wv_t = w_in[:, o_k:o_v].T.astype(BF16)
    wz = w_in[:, o_v:o_z].astype(BF16)
    wxbc = w_in[:, o_z:o_xbc].astype(BF16)
    wdt = w_in[:, o_xbc:].astype(BF16)
    q4, k4, v4, z, xbc, dt, dt_t = _inproj(
        x, p["norm_mix"][i].reshape(1, d), wq_t, wk, wv_t, wz, wxbc, wdt, wdt.T,
        p["q_norm"][i].reshape(HEAD_DIM, 1), jnp.pad(p["k_norm"][i], (0, LANES - HEAD_DIM)).reshape(1, LANES),
        tabs, tl["tm_in"])

    attn = _attention(q4, k4, v4, tl["tq"], tl["tk"])

    dt_dir = dt.reshape(b, l, 2, SSM_HEADS).transpose(2, 0, 1, 3)
    dtt_dir = dt_t.reshape(b, 2, SSM_HEADS, l).transpose(1, 0, 2, 3)
    ssd_common = (xbc, dt_dir, dtt_dir, p["conv_w"][i], p["conv_b"][i].reshape(1, XBC_WIDTH),
                  p["dt_bias"][i].reshape(2, 1, SSM_HEADS), p["dt_bias"][i].reshape(2, SSM_HEADS, 1),
                  p["a_log"][i].reshape(2, 1, SSM_HEADS), p["a_log"][i].reshape(2, SSM_HEADS, 1))
    y_fwd = _ssd(*ssd_common, 0)
    ssm = _ssd(*ssd_common, 1, y_fwd=y_fwd, z=z,
               dskip=jnp.repeat(p["d_skip"][i], SSM_HEAD_DIM).reshape(1, SSM_WIDTH),
               gain=p["ssm_norm"][i].reshape(1, SSM_WIDTH))

    w_out = p["w_out"][i]
    wa = jnp.pad(w_out[:ATTN_WIDTH].reshape(N_HEADS, HEAD_DIM, d),
                 ((0, 0), (0, LANES - HEAD_DIM), (0, 0))).reshape(Q_PAD, d).astype(BF16)
    ws = w_out[ATTN_WIDTH:].astype(BF16)
    again = jnp.pad(p["attn_norm"][i].reshape(N_HEADS, HEAD_DIM),
                    ((0, 0), (0, LANES - HEAD_DIM))).reshape(1, Q_PAD)
    wr_t = p["w_router"][i].T
    wr_hi = wr_t.astype(BF16)
    wr_lo = (wr_t - wr_hi.astype(F32)).astype(BF16)
    x1, hn, logits_t = _outproj(attn, ssm.reshape(t, SSM_WIDTH), x.reshape(t, d), again, wa, ws,
                                p["norm_ffn"][i].reshape(1, d), wr_hi, wr_lo, tl["tm_out"])

    tn = tl["tn_route"]
    idx_t, w_t, rank_t, counts, tile_counts = _route(logits_t, p["b_router"][i].reshape(N_EXPERTS, 1), tn)
    del rank_t

    bm = tl["bm"]
    nt = t // tn
    del counts
    tile_counts = tile_counts.reshape(nt, N_EXPERTS)
    run_len = (tile_counts + RUN_ALIGN - 1) // RUN_ALIGN * RUN_ALIGN
    run_off = jnp.cumsum(run_len, axis=1) - run_len
    rows_e = jnp.sum(run_len, axis=0)
    padded = (rows_e + bm - 1) // bm * bm
    pad_end = jnp.cumsum(padded)
    pad_start = pad_end - padded
    run_row = pad_start[None, :] + jnp.cumsum(run_len, axis=0) - run_len
    n_sorted = TOP_K * tn + N_EXPERTS * RUN_ALIGN
    n_rows = t * TOP_K + nt * N_EXPERTS * RUN_ALIGN + N_EXPERTS * bm
    nb = n_rows // bm
    block_row = jnp.arange(nb, dtype=jnp.int32) * bm
    block_e = jnp.minimum(jnp.sum(pad_end[None, :] <= block_row[:, None], axis=1), N_EXPERTS - 1).astype(jnp.int32)
    nused = (pad_end[-1:] // bm).astype(jnp.int32)
    has_rows = padded > 0
    order = jnp.cumsum(has_rows.astype(jnp.int32)) - 1
    expert_ids = jnp.arange(N_EXPERTS, dtype=jnp.int32)
    later = jnp.where(has_rows[None, :] & (expert_ids[None, :] > expert_ids[:, None]), expert_ids[None, :], N_EXPERTS)
    next_rows = jnp.min(later, axis=1)
    next_rows = jnp.where(next_rows < N_EXPERTS, next_rows, -1).astype(jnp.int32)
    block_slot = (order[block_e] % 2).astype(jnp.int32)
    block_next = next_rows[block_e]
    seg_lo = jnp.concatenate([pad_start + rows_e, pad_end[-1:]]).astype(jnp.int32)
    seg_hi = jnp.concatenate([pad_end, jnp.full((1,), n_rows, jnp.int32)]).astype(jnp.int32)
    tab = jnp.concatenate([run_off, run_len, run_row], axis=1).astype(jnp.int32).reshape(nt, 1, 3 * N_EXPERTS)
    ones = jnp.ones((tn, tn), BF16)

    xs = _dispatch(seg_lo, seg_hi, tab, idx_t, run_off.reshape(nt, N_EXPERTS, 1).astype(jnp.int32),
                   jnp.triu(ones), hn, n_rows, n_sorted, bm, tn)
    n_all = p["w_gate"].shape[0] * N_EXPERTS
    ys = _experts(block_e, block_slot, block_next, nused, xs,
                  p["w_gate"].reshape(n_all, d, d), p["b_gate"].reshape(n_all, 1, d),
                  p["w_up"].reshape(n_all, d, d), p["b_up"].reshape(n_all, 1, d),
                  p["w_down"].reshape(n_all, d, d), p["b_down"].reshape(n_all, 1, d), bm, i)
    x2 = _combine(tab, idx_t.T, w_t.T, run_off.reshape(nt, 1, N_EXPERTS).astype(jnp.int32),
                  jnp.tril(ones), x1, ys, n_sorted, tn)
    return x2.reshape(b, l, d)


_PARAM_NAMES = ("norm_mix", "w_in", "q_norm", "k_norm", "conv_w", "conv_b", "dt_bias", "a_log", "d_skip",
                "ssm_norm", "attn_norm", "w_out", "norm_ffn", "w_router", "b_router", "w_gate", "b_gate",
                "w_up", "b_up", "w_down", "b_down")


def kernel(x, norm_mix, w_in, q_norm, k_norm, conv_w, conv_b, dt_bias, a_log, d_skip, ssm_norm, attn_norm,
           w_out, norm_ffn, w_router, b_router, w_gate, b_gate, w_up, b_up, w_down, b_down):
    params = dict(zip(_PARAM_NAMES, (norm_mix, w_in, q_norm, k_norm, conv_w, conv_b, dt_bias, a_log, d_skip,
                                     ssm_norm, attn_norm, w_out, norm_ffn, w_router, b_router, w_gate, b_gate,
                                     w_up, b_up, w_down, b_down)))
    b, l, _ = x.shape
    tabs = _rope_tables(l)
    tl = _tiles(l, b * l)
    for i in range(norm_mix.shape[0]):
        x = _layer(x, params, i, tabs, tl)
    return x
```

```python
import functools
import math

import jax
import jax.numpy as jnp
from jax import lax
from jax.experimental import pallas as pl
from jax.experimental.pallas import tpu as pltpu

F32 = jnp.float32
BF16 = jnp.bfloat16

D_MODEL = 1024
GRID_W = 64
HEAD_DIM = 64
N_HEADS = 8
N_KV = 2
HEADS_PER_KV = N_HEADS // N_KV
ATTN_WIDTH = N_HEADS * HEAD_DIM
KV_WIDTH = N_KV * HEAD_DIM
SSM_WIDTH = 512
SSM_HEADS = 8
SSM_HEAD_DIM = 64
SSM_GROUPS = 2
SSM_HEADS_PER_GROUP = SSM_HEADS // SSM_GROUPS
SSM_STATE = 128
CHUNK = 128
CONV_K = 5
XBC_WIDTH = SSM_WIDTH + 2 * SSM_GROUPS * SSM_STATE
ROPE_THETA = 10000.0
ROPE_AXIS_DIM = HEAD_DIM // 2
N_EXPERTS = 32
TOP_K = 4
SWIGLU_LIMIT = 7.0
SWIGLU_ALPHA = 1.702
EPS = 1e-6

LANES = 128
SUBLANES = 8
VMEM_LIMIT = 56 * 1024 * 1024

Q_PAD = N_HEADS * LANES
ATTN_SCORE_BUFFERS = 2
ATTN_ACC_ROWS = HEAD_DIM + 2 * SUBLANES


def _cparams(sem):
    return pltpu.CompilerParams(dimension_semantics=sem, vmem_limit_bytes=VMEM_LIMIT)


def _dot(a, b):
    return jnp.dot(a, b, preferred_element_type=F32)


def _dot_nt(a, b):
    return lax.dot_general(a, b, (((1,), (1,)), ((), ())), preferred_element_type=F32)


def _split3(x):
    hi = x.astype(BF16)
    r1 = x - hi.astype(F32)
    mid = r1.astype(BF16)
    lo = (r1 - mid.astype(F32)).astype(BF16)
    return hi, mid, lo


def _sigmoid(x):
    return 1.0 / (1.0 + jnp.exp(-x))


def _softplus(x):
    return jnp.maximum(x, 0.0) + jnp.log(1.0 + jnp.exp(-jnp.abs(x)))


def _inproj_kernel(x_ref, g_ref, wqT_ref, wk_ref, wvT_ref, wz_ref, wxbc_ref, wdt_ref, wdtT_ref,
                   qg_ref, kg_ref, cos_ref, s1_ref, s2_ref, cosT_ref, sinT_ref,
                   q_ref, k_ref, v_ref, z_ref, xbc_ref, dt_ref, dtT_ref):
    x = x_ref[0]
    tm = x.shape[0]
    hn = (x * lax.rsqrt(jnp.mean(x * x, axis=-1, keepdims=True) + EPS) * g_ref[...]).astype(BF16)

    cos_t = cosT_ref[...]
    sin_t = sinT_ref[...]
    q_scale = math.log2(math.e) / math.sqrt(HEAD_DIM)
    acc_qt = _dot_nt(wqT_ref[...], hn)
    half = ROPE_AXIS_DIM // 2
    for h in range(N_HEADS):
        a = acc_qt[HEAD_DIM * h:HEAD_DIM * (h + 1)]
        r = lax.rsqrt(jnp.sum(a * a, axis=0, keepdims=True) * (1.0 / HEAD_DIM) + EPS)
        y = a * r * qg_ref[...]
        swapped = jnp.concatenate([y[half:2 * half], y[0:half], y[3 * half:4 * half], y[2 * half:3 * half]], axis=0)
        y = (y * cos_t + swapped * sin_t) * q_scale
        q_ref[0, h, 0:HEAD_DIM, :] = y.astype(BF16)
        q_ref[0, h, HEAD_DIM:, :] = jnp.zeros((LANES - HEAD_DIM, tm), BF16)

    cos = cos_ref[...]
    s1 = s1_ref[...]
    s2 = s2_ref[...]
    acc_k = _dot(hn, wk_ref[...])
    for g in range(N_KV):
        a = acc_k[:, LANES * g:LANES * (g + 1)]
        r = lax.rsqrt(jnp.sum(a * a, axis=-1, keepdims=True) * (1.0 / HEAD_DIM) + EPS)
        y = a * r * kg_ref[...]
        y = y * cos + pltpu.roll(y, LANES - 16, 1) * s1 + pltpu.roll(y, 16, 1) * s2
        k_ref[0, g] = y.astype(BF16)

    acc_vt = _dot_nt(wvT_ref[...], hn)
    ones_row = jnp.where(lax.broadcasted_iota(jnp.int32, (LANES - HEAD_DIM, tm), 0) == 0, 1.0, 0.0).astype(BF16)
    for g in range(N_KV):
        v_ref[0, g, 0:HEAD_DIM, :] = acc_vt[HEAD_DIM * g:HEAD_DIM * (g + 1)].astype(BF16)
        v_ref[0, g, HEAD_DIM:, :] = ones_row

    z_ref[0] = _dot(hn, wz_ref[...])
    xbc_ref[0] = _dot(hn, wxbc_ref[...])
    dt_ref[0] = _dot(hn, wdt_ref[...])
    dtT_ref[0] = _dot_nt(wdtT_ref[...], hn)


def _inproj(x, gain, wqT, wk, wvT, wz, wxbc, wdt, wdtT, qg, kg, tabs, tm):
    b, l, d = x.shape
    nt = l // tm
    const = lambda shape: pl.BlockSpec(shape, lambda bi, i: (0,) * len(shape))
    tab = pl.BlockSpec((tm, LANES), lambda bi, i: (i, 0))
    tab_t = pl.BlockSpec((HEAD_DIM, tm), lambda bi, i: (0, i))
    return pl.pallas_call(
        _inproj_kernel,
        grid=(b, nt),
        in_specs=[
            pl.BlockSpec((1, tm, d), lambda bi, i: (bi, i, 0)),
            const((1, d)), const(wqT.shape), const(wk.shape), const(wvT.shape), const(wz.shape),
            const(wxbc.shape), const(wdt.shape), const(wdtT.shape), const((HEAD_DIM, 1)), const((1, LANES)),
            tab, tab, tab, tab_t, tab_t,
        ],
        out_specs=[
            pl.BlockSpec((1, N_HEADS, LANES, tm), lambda bi, i: (bi, 0, 0, i)),
            pl.BlockSpec((1, N_KV, tm, LANES), lambda bi, i: (bi, 0, i, 0)),
            pl.BlockSpec((1, N_KV, LANES, tm), lambda bi, i: (bi, 0, 0, i)),
            pl.BlockSpec((1, tm, SSM_WIDTH), lambda bi, i: (bi, i, 0)),
            pl.BlockSpec((1, tm, XBC_WIDTH), lambda bi, i: (bi, i, 0)),
            pl.BlockSpec((1, tm, 2 * SSM_HEADS), lambda bi, i: (bi, i, 0)),
            pl.BlockSpec((1, 2 * SSM_HEADS, tm), lambda bi, i: (bi, 0, i)),
        ],
        out_shape=[
            jax.ShapeDtypeStruct((b, N_HEADS, LANES, l), BF16),
            jax.ShapeDtypeStruct((b, N_KV, l, LANES), BF16),
            jax.ShapeDtypeStruct((b, N_KV, LANES, l), BF16),
            jax.ShapeDtypeStruct((b, l, SSM_WIDTH), F32),
            jax.ShapeDtypeStruct((b, l, XBC_WIDTH), F32),
            jax.ShapeDtypeStruct((b, l, 2 * SSM_HEADS), F32),
            jax.ShapeDtypeStruct((b, 2 * SSM_HEADS, l), F32),
        ],
        compiler_params=_cparams(("parallel", "parallel")),
        name="inproj",
    )(x, gain, wqT, wk, wvT, wz, wxbc, wdt, wdtT, qg, kg, *tabs)


def _attn_kernel(q_ref, k_ref, v_ref, o_ref, qcat_ref, m_ref, acc_ref, *bufs, tk):
    tq = q_ref.shape[3]
    seq = k_ref.shape[2]
    cols = HEADS_PER_KV * tq
    n_chunks = seq // tk
    for hh in range(HEADS_PER_KV):
        qcat_ref[:, hh * tq:(hh + 1) * tq] = q_ref[0, hh]
    m_ref[...] = jnp.full((1, cols), -jnp.inf, F32)
    acc_ref[...] = jnp.zeros(acc_ref.shape, F32)

    s_refs, top_refs = bufs[:len(bufs) // 2], bufs[len(bufs) // 2:]

    def scores(j, u):
        off = pl.multiple_of(j * tk, tk)
        s = _dot(k_ref[0, 0, pl.ds(off, tk), :], qcat_ref[...])
        s_refs[u][...] = s
        top_refs[u][...] = jnp.max(s, axis=0, keepdims=True)

    def accumulate(j, u):
        off = pl.multiple_of(j * tk, tk)
        vc = v_ref[0, 0, 0:ATTN_ACC_ROWS, pl.ds(off, tk)]
        s = s_refs[u][...]
        m_old = m_ref[...]
        m_new = jnp.maximum(m_old, top_refs[u][...])
        p = jnp.exp2(s - m_new).astype(BF16)
        acc_ref[...] = jnp.exp2(m_old - m_new) * acc_ref[...] + _dot(vc, p)
        m_ref[...] = m_new

    nbuf = len(s_refs)
    scores(0, 0)

    def body(i, carry):
        j = nbuf * i
        for u in range(nbuf):
            scores(jnp.minimum(j + u + 1, n_chunks - 1), (u + 1) % nbuf)
            accumulate(j + u, u)
        return carry

    lax.fori_loop(0, n_chunks // nbuf, body, 0)
    acc = acc_ref[...]
    out = acc[0:HEAD_DIM] / acc[HEAD_DIM:HEAD_DIM + 1]
    out = jnp.concatenate([out, jnp.zeros((LANES - HEAD_DIM, cols), F32)], axis=0)
    for hh in range(HEADS_PER_KV):
        o_ref[0, hh] = out[:, hh * tq:(hh + 1) * tq].T


def _attention(q, k, v, tq, tk):
    b, _, _, l = q.shape
    cols = HEADS_PER_KV * tq
    nbuf = min(ATTN_SCORE_BUFFERS, l // tk)
    return pl.pallas_call(
        functools.partial(_attn_kernel, tk=tk),
        grid=(b, N_KV, l // tq),
        in_specs=[
            pl.BlockSpec((1, HEADS_PER_KV, LANES, tq), lambda bi, g, i: (bi, g, 0, i)),
            pl.BlockSpec((1, 1, l, LANES), lambda bi, g, i: (bi, g, 0, 0)),
            pl.BlockSpec((1, 1, LANES, l), lambda bi, g, i: (bi, g, 0, 0)),
        ],
        out_specs=pl.BlockSpec((1, HEADS_PER_KV, tq, LANES), lambda bi, g, i: (bi, g, i, 0)),
        out_shape=jax.ShapeDtypeStruct((b, N_HEADS, l, LANES), F32),
        scratch_shapes=[
            pltpu.VMEM((LANES, cols), BF16),
            pltpu.VMEM((1, cols), F32),
            pltpu.VMEM((ATTN_ACC_ROWS, cols), F32),
        ] + [pltpu.VMEM((tk, cols), F32)] * nbuf + [pltpu.VMEM((1, cols), F32)] * nbuf,
        compiler_params=_cparams(("parallel", "parallel", "parallel")),
        name="attn",
    )(q, k, v)


def _ssd_kernel(*refs, reverse):
    if reverse:
        (cur_ref, prev_ref, next_ref, dt_ref, dtT_ref, convw_ref, convb_ref, dtb_row_ref, dtb_col_ref,
         alog_row_ref, alog_col_ref, yin_ref, z_ref, dskip_ref, gain_ref,
         y_ref, pad_ref, state_ref) = refs
    else:
        (cur_ref, prev_ref, next_ref, dt_ref, dtT_ref, convw_ref, convb_ref, dtb_row_ref, dtb_col_ref,
         alog_row_ref, alog_col_ref, y_ref, pad_ref, state_ref) = refs
    c = pl.program_id(1)
    nc = pl.num_programs(1)
    chunk = (nc - 1 - c) if reverse else c

    @pl.when(c == 0)
    def _():
        state_ref[...] = jnp.zeros(state_ref.shape, F32)

    halo = SUBLANES
    pad_ref[0:halo, :] = jnp.where(chunk > 0, prev_ref[0], 0.0)
    pad_ref[halo:halo + CHUNK, :] = cur_ref[0]
    pad_ref[halo + CHUNK:, :] = jnp.where(chunk < nc - 1, next_ref[0], 0.0)
    padded = pad_ref[...]
    rows_padded = CHUNK + 2 * halo
    conv = jnp.broadcast_to(convb_ref[...], (CHUNK, XBC_WIDTH))
    for j in range(CONV_K):
        shifted = pltpu.roll(padded, (CONV_K // 2 - j) % rows_padded, 0) if j != CONV_K // 2 else padded
        conv = conv + convw_ref[j:j + 1, :] * shifted[halo:halo + CHUNK]
    xc = conv * _sigmoid(conv)
    xs = xc[:, :SSM_WIDTH]

    dt_col = _softplus(dt_ref[0, 0] + dtb_row_ref[0])
    dt_row = _softplus(dtT_ref[0, 0] + dtb_col_ref[0])
    da_col = dt_col * -jnp.exp(alog_row_ref[0])
    da_row = dt_row * -jnp.exp(alog_col_ref[0])

    ri = lax.broadcasted_iota(jnp.int32, (CHUNK, CHUNK), 0)
    ci = lax.broadcasted_iota(jnp.int32, (CHUNK, CHUNK), 1)
    low = ci <= ri
    up = ci >= ri
    low_m = jnp.where(low, 1.0, 0.0).astype(BF16)
    up_m = jnp.where(up, 1.0, 0.0).astype(BF16)
    col_m, row_m, mask = (up_m, low_m, up) if reverse else (low_m, up_m, low)
    ch, cm, cl = _split3(da_col)
    acum_col = _dot(col_m, ch) + _dot(col_m, cm) + _dot(col_m, cl)
    rh, rm, rl = _split3(da_row)
    acum_row = _dot(rh, row_m) + _dot(rm, row_m) + _dot(rl, row_m)
    end = 0 if reverse else CHUNK - 1
    a_end_row = acum_col[end:end + 1, :]
    w_col = dt_col * jnp.exp(a_end_row - acum_col)
    e_col = jnp.exp(acum_col)
    chunk_decay = jnp.exp(a_end_row)

    pieces = []
    for g in range(SSM_GROUPS):
        bm = xc[:, SSM_WIDTH + g * SSM_STATE:SSM_WIDTH + (g + 1) * SSM_STATE]
        cmat = xc[:, SSM_WIDTH + (SSM_GROUPS + g) * SSM_STATE:SSM_WIDTH + (SSM_GROUPS + g + 1) * SSM_STATE]
        bm16 = bm.astype(BF16)
        cm16 = cmat.astype(BF16)
        cb = _dot_nt(cm16, bm16)
        bt16 = bm.T.astype(BF16)
        for hh in range(SSM_HEADS_PER_GROUP):
            h = g * SSM_HEADS_PER_GROUP + hh
            xh = xs[:, h * SSM_HEAD_DIM:(h + 1) * SSM_HEAD_DIM]
            seg = acum_col[:, h:h + 1] - acum_row[h:h + 1, :]
            decay = jnp.exp(jnp.where(mask, seg, -jnp.inf))
            mat = cb * decay * dt_row[h:h + 1, :]
            y_diag = _dot(mat.astype(BF16), xh.astype(BF16))
            prev_state = state_ref[h]
            y_off = _dot(cm16, prev_state.astype(BF16)) * e_col[:, h:h + 1]
            pieces.append(y_diag + y_off)
            xw = xh * w_col[:, h:h + 1]
            state_ref[h] = prev_state * chunk_decay[:, h:h + 1] + _dot(bt16, xw.astype(BF16))
    y = jnp.concatenate(pieces, axis=1)

    if not reverse:
        y_ref[0] = y
    else:
        y = y + yin_ref[0] + xs * dskip_ref[...]
        zz = z_ref[0]
        y = y * (zz * _sigmoid(zz))
        gw = SSM_WIDTH // SSM_GROUPS
        outs = []
        for g in range(SSM_GROUPS):
            yg = y[:, g * gw:(g + 1) * gw]
            outs.append(yg * lax.rsqrt(jnp.mean(yg * yg, axis=-1, keepdims=True) + EPS))
        y_ref[0] = jnp.concatenate(outs, axis=1) * gain_ref[...]


def _ssd(xbc, dt, dtT, convw, convb, dtb_row, dtb_col, alog_row, alog_col, direction,
         y_fwd=None, z=None, dskip=None, gain=None):
    b, l, _ = xbc.shape
    nc = l // CHUNK
    reverse = direction == 1
    blocks_per_chunk = CHUNK // SUBLANES
    nblk = l // SUBLANES

    def pos(c):
        return (nc - 1 - c) if reverse else c

    const = lambda shape: pl.BlockSpec(shape, lambda bi, c: (0,) * len(shape))
    dsel = lambda shape: pl.BlockSpec(shape, lambda bi, c: (direction,) + (0,) * (len(shape) - 1))
    in_specs = [
        pl.BlockSpec((1, CHUNK, XBC_WIDTH), lambda bi, c: (bi, pos(c), 0)),
        pl.BlockSpec((1, SUBLANES, XBC_WIDTH),
                     lambda bi, c: (bi, jnp.maximum(pos(c) * blocks_per_chunk - 1, 0), 0)),
        pl.BlockSpec((1, SUBLANES, XBC_WIDTH),
                     lambda bi, c: (bi, jnp.minimum((pos(c) + 1) * blocks_per_chunk, nblk - 1), 0)),
        pl.BlockSpec((1, 1, CHUNK, SSM_HEADS), lambda bi, c: (direction, bi, pos(c), 0)),
        pl.BlockSpec((1, 1, SSM_HEADS, CHUNK), lambda bi, c: (direction, bi, 0, pos(c))),
        const((CONV_K, XBC_WIDTH)), const((1, XBC_WIDTH)),
        dsel((1, 1, SSM_HEADS)), dsel((1, SSM_HEADS, 1)), dsel((1, 1, SSM_HEADS)), dsel((1, SSM_HEADS, 1)),
    ]
    args = [xbc, xbc, xbc, dt, dtT, convw, convb, dtb_row, dtb_col, alog_row, alog_col]
    if reverse:
        in_specs += [
            pl.BlockSpec((1, CHUNK, SSM_WIDTH), lambda bi, c: (bi, pos(c), 0)),
            pl.BlockSpec((1, CHUNK, SSM_WIDTH), lambda bi, c: (bi, pos(c), 0)),
            const((1, SSM_WIDTH)), const((1, SSM_WIDTH)),
        ]
        args += [y_fwd, z, dskip, gain]
    return pl.pallas_call(
        functools.partial(_ssd_kernel, reverse=reverse),
        grid=(b, nc),
        in_specs=in_specs,
        out_specs=pl.BlockSpec((1, CHUNK, SSM_WIDTH), lambda bi, c: (bi, pos(c), 0)),
        out_shape=jax.ShapeDtypeStruct((b, l, SSM_WIDTH), F32),
        scratch_shapes=[
            pltpu.VMEM((CHUNK + 2 * SUBLANES, XBC_WIDTH), F32),
            pltpu.VMEM((SSM_HEADS, SSM_STATE, SSM_HEAD_DIM), F32),
        ],
        compiler_params=_cparams(("parallel", "arbitrary")),
        name="ssd_bwd" if reverse else "ssd_fwd",
    )(*args)


def _outproj_kernel(attn_ref, ssm_ref, x_ref, again_ref, wa_ref, ws_ref, fgain_ref, wr_hi_ref, wr_lo_ref,
                    x1_ref, hn_ref, logit_ref):
    a = jnp.concatenate([attn_ref[0, h] for h in range(N_HEADS)], axis=1)
    r = lax.rsqrt(jnp.sum(a * a, axis=-1, keepdims=True) * (1.0 / ATTN_WIDTH) + EPS)
    an = (a * r * again_ref[...]).astype(BF16)
    x1 = x_ref[...] + _dot(an, wa_ref[...]) + _dot(ssm_ref[...].astype(BF16), ws_ref[...])
    x1_ref[...] = x1
    hn = x1 * lax.rsqrt(jnp.mean(x1 * x1, axis=-1, keepdims=True) + EPS) * fgain_ref[...]
    hn_ref[...] = hn
    h_hi = hn.astype(BF16)
    h_lo = (hn - h_hi.astype(F32)).astype(BF16)
    w_hi = wr_hi_ref[...]
    logit_ref[...] = _dot_nt(w_hi, h_hi) + _dot_nt(w_hi, h_lo) + _dot_nt(wr_lo_ref[...], h_hi)


def _outproj(attn, ssm, x, again, wa, ws, fgain, wr_hi, wr_lo, tm):
    b, _, l, _ = attn.shape
    t = b * l
    nt = l // tm
    const = lambda shape: pl.BlockSpec(shape, lambda i: (0,) * len(shape))
    return pl.pallas_call(
        _outproj_kernel,
        grid=(t // tm,),
        in_specs=[
            pl.BlockSpec((1, N_HEADS, tm, LANES), lambda i: (i // nt, 0, i % nt, 0)),
            pl.BlockSpec((tm, SSM_WIDTH), lambda i: (i, 0)),
            pl.BlockSpec((tm, D_MODEL), lambda i: (i, 0)),
            const((1, Q_PAD)), const(wa.shape), const(ws.shape), const((1, D_MODEL)),
            const(wr_hi.shape), const(wr_lo.shape),
        ],
        out_specs=[
            pl.BlockSpec((tm, D_MODEL), lambda i: (i, 0)),
            pl.BlockSpec((tm, D_MODEL), lambda i: (i, 0)),
            pl.BlockSpec((N_EXPERTS, tm), lambda i: (0, i)),
        ],
        out_shape=[
            jax.ShapeDtypeStruct((t, D_MODEL), F32),
            jax.ShapeDtypeStruct((t, D_MODEL), F32),
            jax.ShapeDtypeStruct((N_EXPERTS, t), F32),
        ],
        compiler_params=_cparams(("parallel",)),
        name="outproj",
    )(attn, ssm, x, again, wa, ws, fgain, wr_hi, wr_lo)


def _route_kernel(logit_ref, bias_ref, idx_ref, w_ref, rank_ref, count_ref, tile_ref, carry_ref):
    i = pl.program_id(0)
    tn = logit_ref.shape[1]

    @pl.when(i == 0)
    def _():
        carry_ref[...] = jnp.zeros(carry_ref.shape, F32)

    lg = logit_ref[...] + bias_ref[...]
    eid = lax.broadcasted_iota(jnp.int32, (N_EXPERTS, tn), 0).astype(F32)
    chosen = jnp.zeros((N_EXPERTS, tn), F32)
    vals, ids = [], []
    for _ in range(TOP_K):
        mx = jnp.max(lg, axis=0, keepdims=True)
        sel = jnp.min(jnp.where(lg == mx, eid, float(N_EXPERTS)), axis=0, keepdims=True)
        hit = eid == sel
        chosen = jnp.where(hit, 1.0, chosen)
        lg = jnp.where(hit, -jnp.inf, lg)
        vals.append(mx)
        ids.append(sel)
    ex = [jnp.exp(v - vals[0]) for v in vals]
    den = ex[0] + ex[1] + ex[2] + ex[3]
    ri = lax.broadcasted_iota(jnp.int32, (tn, tn), 0)
    ci = lax.broadcasted_iota(jnp.int32, (tn, tn), 1)
    upper = jnp.where(ri <= ci, 1.0, 0.0).astype(BF16)
    incl = _dot(chosen.astype(BF16), upper)
    rank_e = incl - chosen + carry_ref[...]
    carry_ref[...] = carry_ref[...] + incl[:, tn - 1:tn]
    for k in range(TOP_K):
        idx_ref[k:k + 1, :] = ids[k].astype(jnp.int32)
        w_ref[k:k + 1, :] = ex[k] / den
        rank_ref[k:k + 1, :] = jnp.sum(jnp.where(eid == ids[k], rank_e, 0.0), axis=0,
                                       keepdims=True).astype(jnp.int32)
    count_ref[...] = carry_ref[...].astype(jnp.int32)
    tile_ref[0] = incl[:, tn - 1:tn].astype(jnp.int32)


def _route(logits_t, bias_col, tn):
    t = logits_t.shape[1]
    tok = pl.BlockSpec((TOP_K, tn), lambda i: (0, i))
    return pl.pallas_call(
        _route_kernel,
        grid=(t // tn,),
        in_specs=[pl.BlockSpec((N_EXPERTS, tn), lambda i: (0, i)),
                  pl.BlockSpec((N_EXPERTS, 1), lambda i: (0, 0))],
        out_specs=[tok, tok, tok, pl.BlockSpec((N_EXPERTS, 1), lambda i: (0, 0)),
                   pl.BlockSpec((1, N_EXPERTS, 1), lambda i: (i, 0, 0))],
        out_shape=[
            jax.ShapeDtypeStruct((TOP_K, t), jnp.int32),
            jax.ShapeDtypeStruct((TOP_K, t), F32),
            jax.ShapeDtypeStruct((TOP_K, t), jnp.int32),
            jax.ShapeDtypeStruct((N_EXPERTS, 1), jnp.int32),
            jax.ShapeDtypeStruct((t // tn, N_EXPERTS, 1), jnp.int32),
        ],
        scratch_shapes=[pltpu.VMEM((N_EXPERTS, 1), F32)],
        compiler_params=_cparams(("arbitrary",)),
        name="route",
    )(logits_t, bias_col)


N_DMA_PRIORITIES = 2
RUN_ALIGN = SUBLANES
RUN_BITS = (256, 128, 64, 32, 16, 8)


def _copy_run(src_ref, dst_ref, src_start, dst_start, count, sem, start=True, wait=True):
    for n, bit in enumerate(RUN_BITS):
        take = count & bit

        @pl.when(take != 0)
        def _():
            cp = pltpu.make_async_copy(src_ref.at[pl.ds(pl.multiple_of(src_start, RUN_ALIGN), bit), :],
                                       dst_ref.at[pl.ds(pl.multiple_of(dst_start, RUN_ALIGN), bit), :], sem)
            if start:
                cp.start(priority=n % N_DMA_PRIORITIES)
            if wait:
                cp.wait()

        src_start = src_start + take
        dst_start = dst_start + take


def _tile_positions_rows(idx_ref, off_col, tri_ref):
    tn = idx_ref.shape[1]
    eid = lax.broadcasted_iota(jnp.int32, (N_EXPERTS, tn), 0)
    hits = [eid == idx_ref[k:k + 1, :] for k in range(TOP_K)]
    chosen = jnp.where(hits[0] | hits[1] | hits[2] | hits[3], 1.0, 0.0)
    incl = _dot(chosen.astype(BF16), tri_ref[...])
    slot = off_col + incl - chosen
    return [jnp.sum(jnp.where(h, slot, 0.0), axis=0, keepdims=True).astype(jnp.int32) for h in hits]


def _dispatch_kernel(seg_lo_ref, seg_hi_ref, tab_ref, tab_prev_ref, idx_ref, off_ref, tri_ref, hn_ref, xs_ref,
                     sorted_ref, zero_ref, sems, sem, *, bm):
    i = pl.program_id(0)
    last = pl.num_programs(0) - 1
    slot = i % 2
    tn = idx_ref.shape[1]
    n_sorted = sorted_ref.shape[1]

    @pl.when(i == 0)
    def _():
        zero_ref[...] = jnp.zeros(zero_ref.shape, F32)

        def segment(e, carry):
            lo = seg_lo_ref[e]
            count = seg_hi_ref[e] - lo

            def whole(j, c):
                cp = pltpu.make_async_copy(zero_ref, xs_ref.at[pl.ds(pl.multiple_of(lo + j * bm, RUN_ALIGN), bm), :], sem)
                cp.start()
                cp.wait()
                return c

            lax.fori_loop(0, count // bm, whole, 0)
            _copy_run(zero_ref, xs_ref, 0, lo + count // bm * bm, count % bm, sem)
            return carry

        lax.fori_loop(0, N_EXPERTS + 1, segment, 0)

    lpos = _tile_positions_rows(idx_ref, off_ref[0].astype(F32), tri_ref)
    pos = lax.broadcasted_iota(jnp.int32, (n_sorted, tn), 0)
    onehot = (pos == lpos[0]) | (pos == lpos[1]) | (pos == lpos[2]) | (pos == lpos[3])
    sorted_ref[slot] = _dot(jnp.where(onehot, 1.0, 0.0).astype(BF16), hn_ref[...].astype(BF16))

    def runs(tab, buf, start):
        def body(e, carry):
            _copy_run(sorted_ref.at[buf], xs_ref, tab[0, 0, e], tab[0, 0, 2 * N_EXPERTS + e],
                      tab[0, 0, N_EXPERTS + e], sems.at[buf], start=start, wait=not start)
            return carry

        lax.fori_loop(0, N_EXPERTS, body, 0)

    runs(tab_ref, slot, True)

    @pl.when(i > 0)
    def _():
        runs(tab_prev_ref, 1 - slot, False)

    @pl.when(i == last)
    def _():
        runs(tab_ref, slot, False)


def _dispatch(seg_lo, seg_hi, tab, idx_t, off_col, tri, hn, n_rows, n_sorted, bm, tn):
    t, d = hn.shape
    return pl.pallas_call(
        functools.partial(_dispatch_kernel, bm=bm),
        grid_spec=pltpu.PrefetchScalarGridSpec(
            num_scalar_prefetch=2,
            grid=(t // tn,),
            in_specs=[
                pl.BlockSpec((1, 1, 3 * N_EXPERTS), lambda i, lo, hi: (i, 0, 0), memory_space=pltpu.SMEM),
                pl.BlockSpec((1, 1, 3 * N_EXPERTS), lambda i, lo, hi: (jnp.maximum(i - 1, 0), 0, 0),
                             memory_space=pltpu.SMEM),
                pl.BlockSpec((TOP_K, tn), lambda i, lo, hi: (0, i)),
                pl.BlockSpec((1, N_EXPERTS, 1), lambda i, lo, hi: (i, 0, 0)),
                pl.BlockSpec((tn, tn), lambda i, lo, hi: (0, 0)),
                pl.BlockSpec((tn, d), lambda i, lo, hi: (i, 0)),
            ],
            out_specs=pl.BlockSpec(memory_space=pl.ANY),
            scratch_shapes=[pltpu.VMEM((2, n_sorted, d), F32), pltpu.VMEM((bm, d), F32),
                            pltpu.SemaphoreType.DMA((2,)), pltpu.SemaphoreType.DMA(())],
        ),
        out_shape=jax.ShapeDtypeStruct((n_rows, d), F32),
        compiler_params=_cparams(("arbitrary",)),
        name="dispatch",
    )(seg_lo, seg_hi, tab, tab, idx_t, off_col, tri, hn)


def _expert_kernel(block_e_ref, slot_ref, next_e_ref, nused_ref, xs_ref, wg_ref, bg_ref, wu_ref, bu_ref,
                   wd_ref, bd_ref, ys_ref, wbuf, w16, sems, *, base):
    b = pl.program_id(0)
    used = b < nused_ref[0]
    e = block_e_ref[b]
    slot = slot_ref[b]
    prev_e = block_e_ref[jnp.maximum(b - 1, 0)]
    fresh = jnp.logical_and(used, jnp.logical_or(b == 0, e != prev_e))

    def fetch(expert, buf):
        return [pltpu.make_async_copy(w_ref.at[base + expert], wbuf.at[buf, j], sems.at[buf])
                for j, w_ref in enumerate((wg_ref, wu_ref, wd_ref))]

    @pl.when(b == 0)
    def _():
        for cp in fetch(e, slot):
            cp.start(priority=1)

    @pl.when(fresh)
    def _():
        for cp in fetch(e, slot):
            cp.wait()
        for j in range(3):
            w16[j] = wbuf[slot, j].astype(BF16)
        nxt = next_e_ref[b]

        @pl.when(nxt >= 0)
        def _():
            for cp in fetch(nxt, 1 - slot):
                cp.start(priority=1)

    @pl.when(used)
    def _():
        x = xs_ref[...].astype(BF16)
        g = _dot(x, w16[0]) + bg_ref[0]
        u = _dot(x, w16[1]) + bu_ref[0]
        g = jnp.minimum(g, SWIGLU_LIMIT)
        u = jnp.clip(u, -SWIGLU_LIMIT, SWIGLU_LIMIT)
        act = g * _sigmoid(SWIGLU_ALPHA * g) * (u + 1.0)
        ys_ref[...] = _dot(act.astype(BF16), w16[2]) + bd_ref[0]

    @pl.when(jnp.logical_not(used))
    def _():
        ys_ref[...] = jnp.zeros(ys_ref.shape, F32)


def _experts(block_e, slot, next_e, nused, xs, wg, bg, wu, bu, wd, bd, bm, layer):
    n_rows, d = xs.shape
    nb = n_rows // bm
    base = layer * N_EXPERTS
    hbm = pl.BlockSpec(memory_space=pl.ANY)
    bspec = pl.BlockSpec((1, 1, d), lambda i, be, sl, ne, nu: (base + be[i], 0, 0))
    row = pl.BlockSpec((bm, d), lambda i, be, sl, ne, nu: (i, 0))
    return pl.pallas_call(
        functools.partial(_expert_kernel, base=base),
        grid_spec=pltpu.PrefetchScalarGridSpec(
            num_scalar_prefetch=4,
            grid=(nb,),
            in_specs=[row, hbm, bspec, hbm, bspec, hbm, bspec],
            out_specs=row,
            scratch_shapes=[pltpu.VMEM((2, 3, d, d), F32), pltpu.VMEM((3, d, d), BF16),
                            pltpu.SemaphoreType.DMA((2,))],
        ),
        out_shape=jax.ShapeDtypeStruct((n_rows, d), F32),
        compiler_params=_cparams(("arbitrary",)),
        name="experts",
    )(block_e, slot, next_e, nused, xs, wg, bg, wu, bu, wd, bd)


def _combine_kernel(tab_ref, tab_next_ref, idx_ref, w_ref, off_ref, tri_ref, x1_ref, ys_ref, out_ref,
                    sorted_ref, sems):
    i = pl.program_id(0)
    last = pl.num_programs(0) - 1
    cur = i % 2
    tn = x1_ref.shape[0]
    n_sorted = sorted_ref.shape[1]

    def runs(tab, buf, start):
        def body(e, carry):
            _copy_run(ys_ref, sorted_ref.at[buf], tab[0, 0, 2 * N_EXPERTS + e], tab[0, 0, e],
                      tab[0, 0, N_EXPERTS + e], sems.at[buf], start=start, wait=not start)
            return carry

        lax.fori_loop(0, N_EXPERTS, body, 0)

    @pl.when(i == 0)
    def _():
        runs(tab_ref, cur, True)

    @pl.when(i < last)
    def _():
        runs(tab_next_ref, 1 - cur, True)

    idx = idx_ref[...]
    eid = lax.broadcasted_iota(jnp.int32, (tn, N_EXPERTS), 1)
    hits = [eid == idx[:, k:k + 1] for k in range(TOP_K)]
    chosen = jnp.where(hits[0] | hits[1] | hits[2] | hits[3], 1.0, 0.0)
    incl = _dot(tri_ref[...], chosen.astype(BF16))
    slot = off_ref[0].astype(F32) + incl - chosen
    w = w_ref[...]
    pos = lax.broadcasted_iota(jnp.int32, (tn, n_sorted), 1)
    weights = jnp.zeros((tn, n_sorted), F32)
    for k in range(TOP_K):
        lpos = jnp.sum(jnp.where(hits[k], slot, 0.0), axis=1, keepdims=True).astype(jnp.int32)
        weights = jnp.where(pos == lpos, w[:, k:k + 1], weights)
    w_hi = weights.astype(BF16)
    w_lo = (weights - w_hi.astype(F32)).astype(BF16)

    runs(tab_ref, cur, False)
    row_id = lax.broadcasted_iota(jnp.int32, (n_sorted, 1), 0)
    y = jnp.where(row_id < tab_ref[0, 0, N_EXPERTS - 1] + tab_ref[0, 0, 2 * N_EXPERTS - 1], sorted_ref[cur], 0.0)
    y_hi = y.astype(BF16)
    y_lo = (y - y_hi.astype(F32)).astype(BF16)
    out_ref[...] = x1_ref[...] + _dot(w_hi, y_hi) + _dot(w_hi, y_lo) + _dot(w_lo, y_hi)


def _combine(tab, idx_tok, w_tok, off_row, tri, x1, ys, n_sorted, tn):
    t, d = x1.shape
    return pl.pallas_call(
        _combine_kernel,
        grid=(t // tn,),
        in_specs=[
            pl.BlockSpec((1, 1, 3 * N_EXPERTS), lambda i: (i, 0, 0), memory_space=pltpu.SMEM),
            pl.BlockSpec((1, 1, 3 * N_EXPERTS), lambda i: (jnp.minimum(i + 1, t // tn - 1), 0, 0),
                         memory_space=pltpu.SMEM),
            pl.BlockSpec((tn, TOP_K), lambda i: (i, 0)),
            pl.BlockSpec((tn, TOP_K), lambda i: (i, 0)),
            pl.BlockSpec((1, 1, N_EXPERTS), lambda i: (i, 0, 0)),
            pl.BlockSpec((tn, tn), lambda i: (0, 0)),
            pl.BlockSpec((tn, d), lambda i: (i, 0)),
            pl.BlockSpec(memory_space=pl.ANY),
        ],
        out_specs=pl.BlockSpec((tn, d), lambda i: (i, 0)),
        out_shape=jax.ShapeDtypeStruct((t, d), F32),
        scratch_shapes=[pltpu.VMEM((2, n_sorted, d), F32), pltpu.SemaphoreType.DMA((2,))],
        compiler_params=_cparams(("arbitrary",)),
        name="combine",
    )(tab, tab, idx_tok, w_tok, off_row, tri, x1, ys)


def _pad_heads(w, n_heads):
    d = w.shape[0]
    w = w.reshape(d, n_heads, HEAD_DIM)
    return jnp.pad(w, ((0, 0), (0, 0), (0, LANES - HEAD_DIM))).reshape(d, n_heads * LANES)


def _rope_tables(seq_len):
    rows = seq_len // GRID_W
    inv_freq = ROPE_THETA ** (-jnp.arange(0, ROPE_AXIS_DIM, 2, dtype=F32) / ROPE_AXIS_DIM)
    ang_r = jnp.arange(rows, dtype=F32)[:, None] * inv_freq
    ang_c = jnp.arange(GRID_W, dtype=F32)[:, None] * inv_freq
    expand_r = lambda a: jnp.repeat(a, GRID_W, axis=0)
    expand_c = lambda a: jnp.tile(a, (rows, 1))
    cr, sr = expand_r(jnp.cos(ang_r)), expand_r(jnp.sin(ang_r))
    cc, sc = expand_c(jnp.cos(ang_c)), expand_c(jnp.sin(ang_c))
    zeros = jnp.zeros_like(cr)
    pad = jnp.zeros((seq_len, LANES - HEAD_DIM), F32)
    cos = jnp.concatenate([cr, cr, cc, cc, pad], axis=1)
    s_up = jnp.concatenate([-sr, zeros, -sc, zeros, pad], axis=1)
    s_dn = jnp.concatenate([zeros, sr, zeros, sc, pad], axis=1)
    cos_t = jnp.concatenate([cr, cr, cc, cc], axis=1).T
    sin_t = jnp.concatenate([-sr, sr, -sc, sc], axis=1).T
    return cos, s_up, s_dn, cos_t, sin_t


def _tiles(seq_len, n_tok):
    return dict(
        tm_in=min(512, seq_len),
        tq=min(256, seq_len),
        tk=min(512, seq_len // 2),
        tm_out=min(512, seq_len),
        tn_route=min(256, n_tok),
        bm=256,
    )


def _layer(x, p, i, tabs, tl):
    b, l, d = x.shape
    t = b * l
    w_in = p["w_in"][i]
    o_q, o_k, o_v, o_z, o_xbc = ATTN_WIDTH, ATTN_WIDTH + KV_WIDTH, ATTN_WIDTH + 2 * KV_WIDTH, \
        ATTN_WIDTH + 2 * KV_WIDTH + SSM_WIDTH, ATTN_WIDTH + 2 * KV_WIDTH + SSM_WIDTH + XBC_WIDTH
    wq_t = w_in[:, :o_q].T.astype(BF16)
    wk = _pad_heads(w_in[:, o_q:o_k], N_KV).astype(BF16)
    wv_t = w_in[:, o_k:o_v].T.astype(BF16)
    wz = w_in[:, o_v:o_z].astype(BF16)
    wxbc = w_in[:, o_z:o_xbc].astype(BF16)
    wdt = w_in[:, o_xbc:].astype(BF16)
    q4, k4, v4, z, xbc, dt, dt_t = _inproj(
        x, p["norm_mix"][i].reshape(1, d), wq_t, wk, wv_t, wz, wxbc, wdt, wdt.T,
        p["q_norm"][i].reshape(HEAD_DIM, 1), jnp.pad(p["k_norm"][i], (0, LANES - HEAD_DIM)).reshape(1, LANES),
        tabs, tl["tm_in"])

    attn = _attention(q4, k4, v4, tl["tq"], tl["tk"])

    dt_dir = dt.reshape(b, l, 2, SSM_HEADS).transpose(2, 0, 1, 3)
    dtt_dir = dt_t.reshape(b, 2, SSM_HEADS, l).transpose(1, 0, 2, 3)
    ssd_common = (xbc, dt_dir, dtt_dir, p["conv_w"][i], p["conv_b"][i].reshape(1, XBC_WIDTH),
                  p["dt_bias"][i].reshape(2, 1, SSM_HEADS), p["dt_bias"][i].reshape(2, SSM_HEADS, 1),
                  p["a_log"][i].reshape(2, 1, SSM_HEADS), p["a_log"][i].reshape(2, SSM_HEADS, 1))
    y_fwd = _ssd(*ssd_common, 0)
    ssm = _ssd(*ssd_common, 1, y_fwd=y_fwd, z=z,
               dskip=jnp.repeat(p["d_skip"][i], SSM_HEAD_DIM).reshape(1, SSM_WIDTH),
               gain=p["ssm_norm"][i].reshape(1, SSM_WIDTH))

    w_out = p["w_out"][i]
    wa = jnp.pad(w_out[:ATTN_WIDTH].reshape(N_HEADS, HEAD_DIM, d),
                 ((0, 0), (0, LANES - HEAD_DIM), (0, 0))).reshape(Q_PAD, d).astype(BF16)
    ws = w_out[ATTN_WIDTH:].astype(BF16)
    again = jnp.pad(p["attn_norm"][i].reshape(N_HEADS, HEAD_DIM),
                    ((0, 0), (0, LANES - HEAD_DIM))).reshape(1, Q_PAD)
    wr_t = p["w_router"][i].T
    wr_hi = wr_t.astype(BF16)
    wr_lo = (wr_t - wr_hi.astype(F32)).astype(BF16)
    x1, hn, logits_t = _outproj(attn, ssm.reshape(t, SSM_WIDTH), x.reshape(t, d), again, wa, ws,
                                p["norm_ffn"][i].reshape(1, d), wr_hi, wr_lo, tl["tm_out"])

    tn = tl["tn_route"]
    idx_t, w_t, rank_t, counts, tile_counts = _route(logits_t, p["b_router"][i].reshape(N_EXPERTS, 1), tn)
    del rank_t

    bm = tl["bm"]
    nt = t // tn
    del counts
    tile_counts = tile_counts.reshape(nt, N_EXPERTS)
    run_len = (tile_counts + RUN_ALIGN - 1) // RUN_ALIGN * RUN_ALIGN
    run_off = jnp.cumsum(run_len, axis=1) - run_len
    rows_e = jnp.sum(run_len, axis=0)
    padded = (rows_e + bm - 1) // bm * bm
    pad_end = jnp.cumsum(padded)
    pad_start = pad_end - padded
    run_row = pad_start[None, :] + jnp.cumsum(run_len, axis=0) - run_len
    n_sorted = TOP_K * tn + N_EXPERTS * RUN_ALIGN
    n_rows = t * TOP_K + nt * N_EXPERTS * RUN_ALIGN + N_EXPERTS * bm
    nb = n_rows // bm
    block_row = jnp.arange(nb, dtype=jnp.int32) * bm
    block_e = jnp.minimum(jnp.sum(pad_end[None, :] <= block_row[:, None], axis=1), N_EXPERTS - 1).astype(jnp.int32)
    nused = (pad_end[-1:] // bm).astype(jnp.int32)
    has_rows = padded > 0
    order = jnp.cumsum(has_rows.astype(jnp.int32)) - 1
    expert_ids = jnp.arange(N_EXPERTS, dtype=jnp.int32)
    later = jnp.where(has_rows[None, :] & (expert_ids[None, :] > expert_ids[:, None]), expert_ids[None, :], N_EXPERTS)
    next_rows = jnp.min(later, axis=1)
    next_rows = jnp.where(next_rows < N_EXPERTS, next_rows, -1).astype(jnp.int32)
    block_slot = (order[block_e] % 2).astype(jnp.int32)
    block_next = next_rows[block_e]
    seg_lo = jnp.concatenate([pad_start + rows_e, pad_end[-1:]]).astype(jnp.int32)
    seg_hi = jnp.concatenate([pad_end, jnp.full((1,), n_rows, jnp.int32)]).astype(jnp.int32)
    tab = jnp.concatenate([run_off, run_len, run_row], axis=1).astype(jnp.int32).reshape(nt, 1, 3 * N_EXPERTS)
    ones = jnp.ones((tn, tn), BF16)

    xs = _dispatch(seg_lo, seg_hi, tab, idx_t, run_off.reshape(nt, N_EXPERTS, 1).astype(jnp.int32),
                   jnp.triu(ones), hn, n_rows, n_sorted, bm, tn)
    n_all = p["w_gate"].shape[0] * N_EXPERTS
    ys = _experts(block_e, block_slot, block_next, nused, xs,
                  p["w_gate"].reshape(n_all, d, d), p["b_gate"].reshape(n_all, 1, d),
                  p["w_up"].reshape(n_all, d, d), p["b_up"].reshape(n_all, 1, d),
                  p["w_down"].reshape(n_all, d, d), p["b_down"].reshape(n_all, 1, d), bm, i)
    x2 = _combine(tab, idx_t.T, w_t.T, run_off.reshape(nt, 1, N_EXPERTS).astype(jnp.int32),
                  jnp.tril(ones), x1, ys, n_sorted, tn)
    return x2.reshape(b, l, d)


_PARAM_NAMES = ("norm_mix", "w_in", "q_norm", "k_norm", "conv_w", "conv_b", "dt_bias", "a_log", "d_skip",
                "ssm_norm", "attn_norm", "w_out", "norm_ffn", "w_router", "b_router", "w_gate", "b_gate",
                "w_up", "b_up", "w_down", "b_down")


def kernel(x, norm_mix, w_in, q_norm, k_norm, conv_w, conv_b, dt_bias, a_log, d_skip, ssm_norm, attn_norm,
           w_out, norm_ffn, w_router, b_router, w_gate, b_gate, w_up, b_up, w_down, b_down):
    params = dict(zip(_PARAM_NAMES, (norm_mix, w_in, q_norm, k_norm, conv_w, conv_b, dt_bias, a_log, d_skip,
                                     ssm_norm, attn_norm, w_out, norm_ffn, w_router, b_router, w_gate, b_gate,
                                     w_up, b_up, w_down, b_down)))
    b, l, _ = x.shape
    tabs = _rope_tables(l)
    tl = _tiles(l, b * l)
    for i in range(norm_mix.shape[0]):
        x = _layer(x, params, i, tabs, tl)
    return x
```
